```python
import math
import jax
import jax.numpy as jnp
from jax import lax
import numpy as np

D_MODEL = 1024
BATCH = 2
SEQ = 16384
DEPTH = 2

GRID_W = 64
CTX_LEN = 256
N_BRANCH = 4
WA = 256
CONF_K = 31
NA_HEADS = 4
NA_HEAD_DIM = 64
WB = NA_HEADS * NA_HEAD_DIM
NA_WIN_ROWS = 8
NA_WIN_COLS = 16
ATTN_SCALE = NA_HEAD_DIM ** -0.5
WC = 256
SC_K = 3
WD = 256
HY_ORDER = 2
HY_SHORT_K = 3
HY_PE_BANDS = 16
HY_PE_DIM = 1 + 2 * HY_PE_BANDS
HY_FILT_HID = 64
HY_FAST_DECAY = 0.3
HY_SLOW_DECAY = 1.5
HY_DECAY_TARGET = 1e-2
D_FF = 2816
N_MOD = 9
LN_EPS = 1e-5
ALPHA = (2 * DEPTH) ** 0.25
BETA = (8 * DEPTH) ** -0.25

OFF_A = 0
OFF_Q = OFF_A + 2 * WA
OFF_K = OFF_Q + WB
OFF_V = OFF_K + WB
OFF_SB = OFF_V + WB
OFF_SC = OFF_SB + WC
OFF_SX = OFF_SC + WC
OFF_HV = OFF_SX + WC
OFF_G = OFF_HV + (1 + HY_ORDER) * WD
P_IN = OFF_G + N_BRANCH * D_MODEL

kernel_name = 'hybrid_conv_na_hyena_diffusion_block'


def layer_norm(x, g=None, b=None):
    xf = x.astype(jnp.float32)
    mu = jnp.mean(xf, -1, keepdims=True)
    var = jnp.mean(jnp.square(xf - mu), -1, keepdims=True)
    y = (xf - mu) * lax.rsqrt(var + LN_EPS)
    if g is not None:
        y = y * g.astype(jnp.float32) + b.astype(jnp.float32)
    return y.astype(x.dtype)


def modulate(x, shift, scale):
    return layer_norm(x) * (1.0 + scale) + shift


def post_norm(x, sub, g, b):
    return layer_norm(ALPHA * x + sub, g, b)


def swiglu(h, w_in, w_out):
    a, g = jnp.split(h @ w_in, 2, axis=-1)
    return (jax.nn.silu(g) * a) @ w_out


def depthwise_conv(u, w, b=None):
    k = w.shape[0]
    y = lax.conv_general_dilated(u, w[:, None, :].astype(u.dtype), window_strides=(1,),
                                 padding=[(k // 2, k // 2)],
                                 dimension_numbers=('NWC', 'WIO', 'NWC'),
                                 feature_group_count=u.shape[-1])
    if b is not None:
        y = y + b
    return y


def heads(t):
    return t.reshape(*t.shape[:-1], NA_HEADS, NA_HEAD_DIM)


def conformer_conv(u2, dw_w, dw_b, ln_g, ln_b, w_proj):
    a, g = jnp.split(u2, 2, axis=-1)
    u = depthwise_conv(a * jax.nn.sigmoid(g), dw_w, dw_b)
    return jax.nn.silu(layer_norm(u, ln_g, ln_b)) @ w_proj


def short_gated_conv(gate_b, gate_c, u, conv_w, w_proj):
    return (gate_b * depthwise_conv(gate_c * u, conv_w)) @ w_proj


def hyena_filters(length, w1, b1, w2, b2, freq, w3):
    t = jnp.arange(length, dtype=jnp.float32)
    t_norm = t / max(length - 1, 1)
    bands = jnp.linspace(1e-4, HY_PE_BANDS - 1, HY_PE_BANDS, dtype=jnp.float32)
    ang = (2.0 * math.pi / length) * t[:, None] * bands[None, :]
    z = jnp.concatenate([t_norm[:, None], jnp.cos(ang), -jnp.sin(ang)], axis=-1)
    h = jnp.sin(freq[0] * (z @ w1 + b1))
    h = jnp.sin(freq[1] * (h @ w2 + b2))
    h = (h @ w3).astype(jnp.float32).reshape(length, HY_ORDER, 2, WD)
    deltas = jnp.linspace(math.log(HY_DECAY_TARGET) / HY_SLOW_DECAY,
                          math.log(HY_DECAY_TARGET) / HY_FAST_DECAY, WD, dtype=jnp.float32)
    decay = jnp.exp(-t_norm[:, None] * jnp.abs(deltas)[None, :])
    return h * decay[:, None, None, :]


def bidirectional_fft_conv(u, h_fwd, h_bwd, bias):
    length = u.shape[1]
    n = 2 * length
    k = jnp.concatenate([h_fwd, jnp.zeros_like(h_fwd[:1]), h_bwd[1:][::-1]], axis=0)
    k = k * lax.rsqrt(jnp.sum(jnp.square(k), axis=0, keepdims=True) + 1e-6)
    uf = u.astype(jnp.float32)
    spec = jnp.fft.rfft(uf, n=n, axis=1) * jnp.fft.rfft(k, n=n, axis=0)[None]
    y = jnp.fft.irfft(spec, n=n, axis=1)[:, :length]
    return (y + uf * bias.astype(jnp.float32)).astype(u.dtype)


def hyena_mixer(u3, sconv_w, sconv_b, filt, bias, w_proj):
    parts = jnp.split(depthwise_conv(u3, sconv_w, sconv_b), 1 + HY_ORDER, axis=-1)
    z = parts[0]
    for o in range(HY_ORDER):
        z = parts[1 + o] * bidirectional_fft_conv(z, filt[:, o, 0], filt[:, o, 1], bias[o])
    return z @ w_proj


def neighbourhood_attention(q, k, v, kc, vc, rpb):
    bsz, n_tok, n_h, dh = q.shape
    rows = n_tok // GRID_W
    wr = min(NA_WIN_ROWS, rows)
    wc = NA_WIN_COLS
    qg = (q * ATTN_SCALE).reshape(bsz, rows, GRID_W, n_h, dh).transpose(1, 0, 2, 3, 4)
    kg = k.reshape(bsz, rows, GRID_W, n_h, dh)
    vg = v.reshape(bsz, rows, GRID_W, n_h, dh)
    cols = jnp.arange(GRID_W)
    c0 = jnp.clip(cols - wc // 2, 0, GRID_W - wc)
    col_idx = c0[:, None] + jnp.arange(wc)[None, :]
    dc = col_idx - cols[:, None] + (NA_WIN_COLS - 1)

    def row_fn(args):
        r, q_row = args
        r0 = jnp.clip(r - wr // 2, 0, rows - wr)
        k_nb = lax.dynamic_slice_in_dim(kg, r0, wr, axis=1)[:, :, col_idx]
        v_nb = lax.dynamic_slice_in_dim(vg, r0, wr, axis=1)[:, :, col_idx]
        dr = r0 + jnp.arange(wr) - r + (NA_WIN_ROWS - 1)
        bias = rpb[:, dr[None, :, None], dc[:, None, :]]
        s_nb = jnp.einsum('bqhd,brqjhd->bhqrj', q_row, k_nb).astype(jnp.float32) + bias.astype(jnp.float32)
        s_ctx = jnp.einsum('bqhd,bkhd->bhqk', q_row, kc).astype(jnp.float32)
        s = jnp.concatenate([s_nb.reshape(bsz, n_h, GRID_W, wr * wc), s_ctx], axis=-1)
        p = jax.nn.softmax(s, axis=-1).astype(v.dtype)
        p_nb = p[..., :wr * wc].reshape(bsz, n_h, GRID_W, wr, wc)
        p_ctx = p[..., wr * wc:]
        return (jnp.einsum('bhqrj,brqjhd->bqhd', p_nb, v_nb)
                + jnp.einsum('bhqk,bkhd->bqhd', p_ctx, vc))

    out = lax.map(row_fn, (jnp.arange(rows), qg))
    return out.transpose(1, 0, 2, 3, 4).reshape(bsz, n_tok, n_h * dh)


def context_attention(q, k, v):
    s = jnp.einsum('bqhd,bkhd->bhqk', q * ATTN_SCALE, k).astype(jnp.float32)
    p = jax.nn.softmax(s, axis=-1).astype(v.dtype)
    o = jnp.einsum('bhqk,bkhd->bqhd', p, v)
    return o.reshape(*o.shape[:2], WB)


def token_mixer(p, attn, filt, lw):
    ya = conformer_conv(p[..., OFF_A:OFF_Q], lw['conf_dw_w'], lw['conf_dw_b'],
                        lw['conf_ln_g'], lw['conf_ln_b'], lw['conf_w_proj'])
    yb = attn @ lw['na_w_proj']
    yc = short_gated_conv(p[..., OFF_SB:OFF_SC], p[..., OFF_SC:OFF_SX], p[..., OFF_SX:OFF_HV],
                          lw['sc_conv_w'], lw['sc_w_proj'])
    yd = hyena_mixer(p[..., OFF_HV:OFF_G], lw['hy_sconv_w'], lw['hy_sconv_b'], filt,
                     lw['hy_bias'], lw['hy_w_proj'])
    g = jax.nn.sigmoid(p[..., OFF_G:P_IN]).reshape(*p.shape[:-1], N_BRANCH, D_MODEL)
    m = g[..., 0, :] * ya + g[..., 1, :] * yb + g[..., 2, :] * yc + g[..., 3, :] * yd
    return m @ lw['w_out'] + lw['b_out']


def setup_inputs(seed: int = 0) -> dict:
    key = jax.random.key(seed)
    ks = iter(jax.random.split(key, 40))

    def nrm(shape, scale):
        return jax.random.normal(next(ks), shape, jnp.float32) * scale

    L = DEPTH
    return {
        'x': nrm((BATCH, SEQ, D_MODEL), 1.0),
        'c': nrm((BATCH, D_MODEL), 1.0),
        'ctx': nrm((BATCH, CTX_LEN, D_MODEL), 1.0),
        'c_ctx': nrm((D_MODEL,), 1.0),
        'w_mod': nrm((L, D_MODEL, N_MOD * D_MODEL), 0.5 * D_MODEL ** -0.5),
        'b_mod': nrm((L, N_MOD * D_MODEL), 0.01),
        'post_ln_g': 1.0 + nrm((L, 3, D_MODEL), 0.02),
        'post_ln_b': nrm((L, 3, D_MODEL), 0.02),
        'ffn_w_in': nrm((L, 2, D_MODEL, 2 * D_FF), D_MODEL ** -0.5),
        'ffn_w_out': nrm((L, 2, D_FF, D_MODEL), BETA * D_FF ** -0.5),
        'w_in': nrm((L, D_MODEL, P_IN), D_MODEL ** -0.5),
        'b_in': nrm((L, P_IN), 0.02),
        'conf_dw_w': nrm((L, CONF_K, WA), CONF_K ** -0.5),
        'conf_dw_b': nrm((L, WA), 0.02),
        'conf_ln_g': 1.0 + nrm((L, WA), 0.02),
        'conf_ln_b': nrm((L, WA), 0.02),
        'conf_w_proj': nrm((L, WA, D_MODEL), WA ** -0.5),
        'na_rpb': nrm((L, NA_HEADS, 2 * NA_WIN_ROWS - 1, 2 * NA_WIN_COLS - 1), 0.1),
        'na_w_proj': nrm((L, WB, D_MODEL), WB ** -0.5),
        'sc_conv_w': nrm((L, SC_K, WC), SC_K ** -0.5),
        'sc_w_proj': nrm((L, WC, D_MODEL), WC ** -0.5),
        'hy_sconv_w': nrm((L, HY_SHORT_K, (1 + HY_ORDER) * WD), HY_SHORT_K ** -0.5),
        'hy_sconv_b': nrm((L, (1 + HY_ORDER) * WD), 0.02),
        'hy_w1': nrm((L, HY_PE_DIM, HY_FILT_HID), HY_PE_DIM ** -0.5),
        'hy_b1': nrm((L, HY_FILT_HID), 0.02),
        'hy_w2': nrm((L, HY_FILT_HID, HY_FILT_HID), HY_FILT_HID ** -0.5),
        'hy_b2': nrm((L, HY_FILT_HID), 0.02),
        'hy_freq': 1.0 + nrm((L, 2, HY_FILT_HID), 0.1),
        'hy_w3': nrm((L, HY_FILT_HID, HY_ORDER * 2 * WD), HY_FILT_HID ** -0.5),
        'hy_bias': nrm((L, HY_ORDER, WD), 1.0),
        'hy_w_proj': nrm((L, WD, D_MODEL), WD ** -0.5),
        'w_out': nrm((L, D_MODEL, D_MODEL), BETA * D_MODEL ** -0.5),
        'b_out': nrm((L, D_MODEL), 0.02),
    }


def reference(x, c, ctx, c_ctx, w_mod, b_mod, post_ln_g, post_ln_b, ffn_w_in, ffn_w_out,
              w_in, b_in, conf_dw_w, conf_dw_b, conf_ln_g, conf_ln_b, conf_w_proj,
              na_rpb, na_w_proj, sc_conv_w, sc_w_proj, hy_sconv_w, hy_sconv_b,
              hy_w1, hy_b1, hy_w2, hy_b2, hy_freq, hy_w3, hy_bias, hy_w_proj, w_out, b_out):
    bsz, n_lat, _ = x.shape
    ctx_len = ctx.shape[1]
    s_lat = jax.nn.silu(c)
    s_ctx = jax.nn.silu(c_ctx)
    xc = ctx
    for l in range(DEPTH):
        last = l == DEPTH - 1
        lw = {
            'conf_dw_w': conf_dw_w[l], 'conf_dw_b': conf_dw_b[l], 'conf_ln_g': conf_ln_g[l],
            'conf_ln_b': conf_ln_b[l], 'conf_w_proj': conf_w_proj[l], 'na_w_proj': na_w_proj[l],
            'sc_conv_w': sc_conv_w[l], 'sc_w_proj': sc_w_proj[l], 'hy_sconv_w': hy_sconv_w[l],
            'hy_sconv_b': hy_sconv_b[l], 'hy_bias': hy_bias[l], 'hy_w_proj': hy_w_proj[l],
            'w_out': w_out[l], 'b_out': b_out[l],
        }
        mod_all = (s_lat @ w_mod[l] + b_mod[l]).reshape(bsz, N_MOD, 1, D_MODEL)
        mod = [mod_all[:, i] for i in range(N_MOD)]
        n_cm = 5 if last else N_MOD
        mod_c = (s_ctx @ w_mod[l][:, :n_cm * D_MODEL] + b_mod[l][:n_cm * D_MODEL]).reshape(n_cm, D_MODEL)

        x = post_norm(x, 0.5 * mod[2] * swiglu(modulate(x, mod[0], mod[1]), ffn_w_in[l, 0], ffn_w_out[l, 0]),
                      post_ln_g[l, 0], post_ln_b[l, 0])
        xc = post_norm(xc, 0.5 * mod_c[2] * swiglu(modulate(xc, mod_c[0], mod_c[1]), ffn_w_in[l, 0], ffn_w_out[l, 0]),
                       post_ln_g[l, 0], post_ln_b[l, 0])

        p = modulate(x, mod[3], mod[4]) @ w_in[l] + b_in[l]
        hc = modulate(xc, mod_c[3], mod_c[4])
        if last:
            kvc = hc @ w_in[l][:, OFF_K:OFF_SB] + b_in[l][OFF_K:OFF_SB]
        else:
            pc = hc @ w_in[l] + b_in[l]
            kvc = pc[..., OFF_K:OFF_SB]
        kc = heads(kvc[..., :WB])
        vc = heads(kvc[..., WB:])
        attn = neighbourhood_attention(heads(p[..., OFF_Q:OFF_K]), heads(p[..., OFF_K:OFF_V]),
                                       heads(p[..., OFF_V:OFF_SB]), kc, vc, na_rpb[l])
        filt = hyena_filters(n_lat, hy_w1[l], hy_b1[l], hy_w2[l], hy_b2[l], hy_freq[l], hy_w3[l])
        x = post_norm(x, mod[5] * token_mixer(p, attn, filt, lw), post_ln_g[l, 1], post_ln_b[l, 1])
        if not last:
            attn_c = context_attention(heads(pc[..., OFF_Q:OFF_K]), kc, vc)
            filt_c = hyena_filters(ctx_len, hy_w1[l], hy_b1[l], hy_w2[l], hy_b2[l], hy_freq[l], hy_w3[l])
            xc = post_norm(xc, mod_c[5] * token_mixer(pc, attn_c, filt_c, lw), post_ln_g[l, 1], post_ln_b[l, 1])

        x = post_norm(x, 0.5 * mod[8] * swiglu(modulate(x, mod[6], mod[7]), ffn_w_in[l, 1], ffn_w_out[l, 1]),
                      post_ln_g[l, 2], post_ln_b[l, 2])
        if not last:
            xc = post_norm(xc, 0.5 * mod_c[8] * swiglu(modulate(xc, mod_c[6], mod_c[7]), ffn_w_in[l, 1], ffn_w_out[l, 1]),
                           post_ln_g[l, 2], post_ln_b[l, 2])
    return x
```

```python
import functools
import math

import jax
import jax.numpy as jnp
from jax import lax
from jax.experimental import pallas as pl
from jax.experimental.pallas import tpu as pltpu

D_MODEL = 1024
GRID_W = 64
N_BRANCH = 4
WA = 256
CONF_K = 31
NA_HEADS = 4
NA_HEAD_DIM = 64
WB = NA_HEADS * NA_HEAD_DIM
NA_WIN_ROWS = 8
NA_WIN_COLS = 16
ATTN_SCALE = NA_HEAD_DIM ** -0.5
WC = 256
SC_K = 3
WD = 256
HY_ORDER = 2
HY_SHORT_K = 3
HY_PE_BANDS = 16
HY_PE_DIM = 1 + 2 * HY_PE_BANDS
HY_FILT_HID = 64
HY_FAST_DECAY = 0.3
HY_SLOW_DECAY = 1.5
HY_DECAY_TARGET = 1e-2
D_FF = 2816
N_MOD = 9
LN_EPS = 1e-5

OFF_A = 0
OFF_Q = OFF_A + 2 * WA
OFF_K = OFF_Q + WB
OFF_V = OFF_K + WB
OFF_SB = OFF_V + WB
OFF_SC = OFF_SB + WC
OFF_SX = OFF_SC + WC
OFF_HV = OFF_SX + WC
OFF_G = OFF_HV + (1 + HY_ORDER) * WD
P_IN = OFF_G + N_BRANCH * D_MODEL

NEW_G = 0
NEW_CONV = N_BRANCH * D_MODEL
CONV_W = 2 * WA + 3 * WC + 3 * WD
NEW_Q = NEW_CONV + CONV_W

BF = jnp.bfloat16
F32 = jnp.float32

VMEM_LIMIT_BYTES = 56 * 1024 * 1024
FFN_CHUNK = 1408
INPROJ_CHUNK = 1152
HALO = 16
NA_QROWS = 4
NA_TOK = NA_QROWS * GRID_W
FFT_N2 = 128
NEG_INF = -1e30


def _cparams(n_axes, semantics="parallel"):
    return pltpu.CompilerParams(dimension_semantics=(semantics,) * n_axes,
                                vmem_limit_bytes=VMEM_LIMIT_BYTES)


def _resident(shape):
    nd = len(shape)
    return pl.BlockSpec(shape, lambda *_: (0,) * nd, pipeline_mode=pl.Buffered(1))


def _ln(x):
    mu = jnp.mean(x, axis=-1, keepdims=True)
    xc = x - mu
    var = jnp.mean(xc * xc, axis=-1, keepdims=True)
    return xc * lax.rsqrt(var + LN_EPS)


def _dot(a, b):
    return jnp.dot(a, b, preferred_element_type=F32)


def _dot_nt(a, b):
    return lax.dot_general(a, b, (((1,), (1,)), ((), ())), preferred_element_type=F32)


def _mm_kernel(*refs, a_silu, has_bias, has_epi):
    a_ref, b_ref = refs[0], refs[1]
    pos = 2
    a = a_ref[...]
    if a_silu:
        a = a.astype(F32)
        a = a * jax.nn.sigmoid(a)
    acc = _dot(a.astype(BF), b_ref[...].astype(BF))
    if has_bias:
        acc = acc + refs[pos][...]
        pos += 1
    if has_epi:
        gate_ref, u_ref, row_ref = refs[pos], refs[pos + 1], refs[pos + 2]
        pos += 3
        acc = gate_ref[...].astype(F32) * (acc + u_ref[...].astype(F32) * row_ref[...])
    o_ref = refs[pos]
    o_ref[...] = acc.astype(o_ref.dtype)


def _mm(a, b, *, bn, out_dtype, bias=None, epi=None, a_silu=False):
    m, k = a.shape
    n = b.shape[1]
    bn = min(bn, n)
    ops = [a, b]
    specs = [_resident((m, k)), pl.BlockSpec((k, bn), lambda j: (0, j))]
    if bias is not None:
        ops.append(bias)
        specs.append(pl.BlockSpec((1, bn), lambda j: (0, j)))
    if epi is not None:
        gate, u, row = epi
        ops += [gate, u, row]
        specs += [pl.BlockSpec((m, bn), lambda j: (0, j)),
                  pl.BlockSpec((m, bn), lambda j: (0, j)),
                  pl.BlockSpec((1, bn), lambda j: (0, j))]
    kern = functools.partial(_mm_kernel, a_silu=a_silu, has_bias=bias is not None,
                             has_epi=epi is not None)
    return pl.pallas_call(
        kern, grid=(n // bn,), in_specs=specs,
        out_specs=pl.BlockSpec((m, bn), lambda j: (0, j)),
        out_shape=jax.ShapeDtypeStruct((m, n), out_dtype),
        compiler_params=_cparams(1))(*ops)


def _modulated(x, mod, m0):
    shift = mod[m0:m0 + 1]
    scale = mod[m0 + 1:m0 + 2]
    return _ln(x) * (1.0 + scale) + shift


def _ffn_kernel(x_ref, mod_ref, wi_ref, wo_ref, g_ref, b_ref, o_ref, *, m0, alpha):
    x = x_ref[...]
    mod = mod_ref[0]
    h = _modulated(x, mod, m0).astype(BF)
    acc = jnp.zeros(x.shape, F32)
    for c in range(D_FF // FFN_CHUNK):
        lo = c * FFN_CHUNK
        a = _dot(h, wi_ref[:, lo:lo + FFN_CHUNK])
        g = _dot(h, wi_ref[:, D_FF + lo:D_FF + lo + FFN_CHUNK])
        u = (g * jax.nn.sigmoid(g) * a).astype(BF)
        acc = acc + _dot(u, wo_ref[lo:lo + FFN_CHUNK, :])
    y = alpha * x + (0.5 * mod[m0 + 2:m0 + 3]) * acc
    o_ref[...] = _ln(y) * g_ref[...] + b_ref[...]


def _ffn(x, mod, w_in, w_out, g, b, *, m0, tm, group_tiles, alpha):
    n = x.shape[0]
    kern = functools.partial(_ffn_kernel, m0=m0, alpha=alpha)
    return pl.pallas_call(
        kern, grid=(n // tm,),
        in_specs=[pl.BlockSpec((tm, D_MODEL), lambda i: (i, 0)),
                  pl.BlockSpec((1, N_MOD, D_MODEL), lambda i: (i // group_tiles, 0, 0)),
                  _resident(w_in.shape), _resident(w_out.shape),
                  _resident((1, D_MODEL)), _resident((1, D_MODEL))],
        out_specs=pl.BlockSpec((tm, D_MODEL), lambda i: (i, 0)),
        out_shape=jax.ShapeDtypeStruct((n, D_MODEL), F32),
        compiler_params=_cparams(1))(x, mod, w_in, w_out, g, b)


def _inproj_kernel(x_ref, mod_ref, w_ref, b_ref, o_ref, *, m0):
    h = _modulated(x_ref[...], mod_ref[0], m0).astype(BF)
    for c in range(P_IN // INPROJ_CHUNK):
        lo = c * INPROJ_CHUNK
        o_ref[:, lo:lo + INPROJ_CHUNK] = (
            _dot(h, w_ref[:, lo:lo + INPROJ_CHUNK]) + b_ref[:, lo:lo + INPROJ_CHUNK]).astype(BF)


def _inproj(x, mod, w, b, *, tm, group_tiles):
    n = x.shape[0]
    return pl.pallas_call(
        functools.partial(_inproj_kernel, m0=3), grid=(n // tm,),
        in_specs=[pl.BlockSpec((tm, D_MODEL), lambda i: (i, 0)),
                  pl.BlockSpec((1, N_MOD, D_MODEL), lambda i: (i // group_tiles, 0, 0)),
                  _resident(w.shape), _resident((1, P_IN))],
        out_specs=pl.BlockSpec((tm, P_IN), lambda i: (i, 0)),
        out_shape=jax.ShapeDtypeStruct((n, P_IN), BF),
        compiler_params=_cparams(1))(x, mod, w, b)


def _prep_kernel(x_ref, prev_ref, next_ref, cw_ref, cb_ref, lg_ref, lb_ref, sw_ref, hw_ref, hb_ref,
                 ya_ref, yc_ref, hv_ref, h1_ref, h2_ref, buf_a, buf_c, buf_h, *, tm, seq_tiles):
    i = pl.program_id(0)
    pos = i % seq_tiles
    keep_prev = jnp.where(pos == 0, 0.0, 1.0)
    keep_next = jnp.where(pos == seq_tiles - 1, 0.0, 1.0)

    def fill(t, lo, hi):
        glu = t[:, 0:WA] * jax.nn.sigmoid(t[:, WA:2 * WA])
        buf_a[lo:hi, :] = glu
        buf_c[lo:hi, :] = t[:, 3 * WA:4 * WA] * t[:, 4 * WA:5 * WA]
        buf_h[lo:hi, :] = t[:, 5 * WA:8 * WA]

    main = x_ref[...].astype(F32)
    fill(prev_ref[...].astype(F32) * keep_prev, 0, HALO)
    fill(main, HALO, HALO + tm)
    fill(next_ref[...].astype(F32) * keep_next, HALO + tm, 2 * HALO + tm)

    acc = jnp.zeros((tm, WA), F32) + cb_ref[...]
    for j in range(CONF_K):
        off = HALO - CONF_K // 2 + j
        acc = acc + cw_ref[j:j + 1, :] * buf_a[off:off + tm, :]
    u = _ln(acc) * lg_ref[...] + lb_ref[...]
    ya_ref[...] = (u * jax.nn.sigmoid(u)).astype(BF)

    acc = jnp.zeros((tm, WC), F32)
    for j in range(SC_K):
        off = HALO - SC_K // 2 + j
        acc = acc + sw_ref[j:j + 1, :] * buf_c[off:off + tm, :]
    yc_ref[...] = (main[:, 2 * WA:3 * WA] * acc).astype(BF)

    acc = jnp.zeros((tm, 3 * WD), F32) + hb_ref[...]
    for j in range(HY_SHORT_K):
        off = HALO - HY_SHORT_K // 2 + j
        acc = acc + hw_ref[j:j + 1, :] * buf_h[off:off + tm, :]
    hv_ref[...] = acc[:, 0:WD].astype(BF)
    h1_ref[...] = acc[:, WD:2 * WD].astype(BF)
    h2_ref[...] = acc[:, 2 * WD:3 * WD].astype(BF)


def _prep(p, cw, cb, lg, lb, sw, hw, hb, *, tm, seq_len):
    n = p.shape[0]
    seq_tiles = seq_len // tm
    hb_per_tile = tm // HALO
    n_halo_blocks = n // HALO
    cblk = NEW_CONV // CONV_W
    kern = functools.partial(_prep_kernel, tm=tm, seq_tiles=seq_tiles)
    out = jax.ShapeDtypeStruct((n, WD), BF)
    ospec = pl.BlockSpec((tm, WD), lambda i: (i, 0))
    return pl.pallas_call(
        kern, grid=(n // tm,),
        in_specs=[pl.BlockSpec((tm, CONV_W), lambda i: (i, cblk)),
                  pl.BlockSpec((HALO, CONV_W),
                               lambda i: (jnp.maximum(i * hb_per_tile - 1, 0), cblk)),
                  pl.BlockSpec((HALO, CONV_W),
                               lambda i: (jnp.minimum((i + 1) * hb_per_tile, n_halo_blocks - 1), cblk)),
                  _resident(cw.shape), _resident(cb.shape), _resident(lg.shape), _resident(lb.shape),
                  _resident(sw.shape), _resident(hw.shape), _resident(hb.shape)],
        out_specs=[ospec] * 5, out_shape=[out] * 5,
        scratch_shapes=[pltpu.VMEM((tm + 2 * HALO, WA), F32),
                        pltpu.VMEM((tm + 2 * HALO, WC), F32),
                        pltpu.VMEM((tm + 2 * HALO, 3 * WD), F32)],
        compiler_params=_cparams(1))(p, p, p, cw, cb, lg, lb, sw, hw, hb)


def _na_kernel(q_ref, k0_ref, k1_ref, k2_ref, v0_ref, v1_ref, v2_ref, kc_ref, vc_ref, bias_ref, o_ref):
    q = q_ref[...] * ATTN_SCALE
    k = jnp.concatenate([k0_ref[...], k1_ref[...], k2_ref[...]], axis=0)
    v = jnp.concatenate([v0_ref[...], v1_ref[...], v2_ref[...]], axis=0)
    kc = kc_ref[...]
    vc = vc_ref[...]
    outs = []
    for h in range(NA_HEADS):
        sl = slice(h * NA_HEAD_DIM, (h + 1) * NA_HEAD_DIM)
        qh = q[:, sl]
        s_nb = _dot_nt(qh, k[:, sl]) + bias_ref[0, h]
        s_cx = _dot_nt(qh, kc[:, sl])
        m = jnp.maximum(jnp.max(s_nb, axis=-1, keepdims=True), jnp.max(s_cx, axis=-1, keepdims=True))
        p_nb = jnp.exp(s_nb - m)
        p_cx = jnp.exp(s_cx - m)
        den = jnp.sum(p_nb, axis=-1, keepdims=True) + jnp.sum(p_cx, axis=-1, keepdims=True)
        o = _dot(p_nb.astype(BF), v[:, sl]) + _dot(p_cx.astype(BF), vc[:, sl])
        outs.append(o / den)
    o_ref[...] = jnp.concatenate(outs, axis=-1).astype(BF)


def _na_bias_table(rpb, rows):
    wr = min(NA_WIN_ROWS, rows)
    n_blk = rows // NA_QROWS
    cols = jnp.arange(GRID_W)
    c0 = jnp.clip(cols - NA_WIN_COLS // 2, 0, GRID_W - NA_WIN_COLS)
    dc = cols[None, :] - cols[:, None] + (NA_WIN_COLS - 1)
    ok_c = (cols[None, :] >= c0[:, None]) & (cols[None, :] < c0[:, None] + NA_WIN_COLS)
    tabs = []
    for blk in (0, 1, n_blk - 1):
        qr = blk * NA_QROWS + jnp.arange(NA_QROWS)
        kr = (blk - 1) * NA_QROWS + jnp.arange(3 * NA_QROWS)
        r0 = jnp.clip(qr - wr // 2, 0, rows - wr)
        ok_r = ((kr[None, :] >= r0[:, None]) & (kr[None, :] < r0[:, None] + wr)
                & (kr[None, :] >= 0) & (kr[None, :] < rows))
        dr = kr[None, :] - qr[:, None] + (NA_WIN_ROWS - 1)
        dr_c = jnp.clip(dr, 0, 2 * NA_WIN_ROWS - 2)
        dc_c = jnp.clip(dc, 0, 2 * NA_WIN_COLS - 2)
        b = rpb[:, dr_c[:, None, :, None], dc_c[None, :, None, :]]
        ok = ok_r[:, None, :, None] & ok_c[None, :, None, :]
        b = jnp.where(ok[None], b.astype(F32), NEG_INF)
        tabs.append(b.reshape(NA_HEADS, NA_TOK, 3 * NA_TOK))
    return jnp.stack(tabs)


def _na(p, pc, bias_tab, *, bsz, seq_len):
    t = seq_len // NA_TOK
    qb, kb, vb = NEW_Q // WB, NEW_Q // WB + 1, NEW_Q // WB + 2
    blk = (NA_TOK, WB)
    cblk = (pc.shape[0] // bsz, WB)

    def nbr(j, col):
        return pl.BlockSpec(blk, lambda b, i: (b * t + jnp.clip(i - 1 + j, 0, t - 1), col))

    return pl.pallas_call(
        _na_kernel, grid=(bsz, t),
        in_specs=[pl.BlockSpec(blk, lambda b, i: (b * t + i, qb)),
                  nbr(0, kb), nbr(1, kb), nbr(2, kb), nbr(0, vb), nbr(1, vb), nbr(2, vb),
                  pl.BlockSpec(cblk, lambda b, i: (b, kb)),
                  pl.BlockSpec(cblk, lambda b, i: (b, vb)),
                  pl.BlockSpec((1, NA_HEADS, NA_TOK, 3 * NA_TOK),
                               lambda b, i: (jnp.where(i == 0, 0, jnp.where(i == t - 1, 2, 1)), 0, 0, 0))],
        out_specs=pl.BlockSpec(blk, lambda b, i: (b * t + i, 0)),
        out_shape=jax.ShapeDtypeStruct((bsz * seq_len, WB), BF),
        compiler_params=_cparams(2))(p, p, p, p, p, p, p, pc, pc, bias_tab)


def _ctx_attn_kernel(q_ref, k_ref, v_ref, o_ref):
    q = q_ref[...] * ATTN_SCALE
    k = k_ref[...]
    v = v_ref[...]
    outs = []
    for h in range(NA_HEADS):
        sl = slice(h * NA_HEAD_DIM, (h + 1) * NA_HEAD_DIM)
        s = _dot_nt(q[:, sl], k[:, sl])
        m = jnp.max(s, axis=-1, keepdims=True)
        p = jnp.exp(s - m)
        den = jnp.sum(p, axis=-1, keepdims=True)
        outs.append(_dot(p.astype(BF), v[:, sl]) / den)
    o_ref[...] = jnp.concatenate(outs, axis=-1).astype(BF)


def _ctx_attn(pc, *, bsz, ctx_len):
    qb, kb, vb = NEW_Q // WB, NEW_Q // WB + 1, NEW_Q // WB + 2
    blk = (ctx_len, WB)
    return pl.pallas_call(
        _ctx_attn_kernel, grid=(bsz,),
        in_specs=[pl.BlockSpec(blk, lambda b: (b, qb)), pl.BlockSpec(blk, lambda b: (b, kb)),
                  pl.BlockSpec(blk, lambda b: (b, vb))],
        out_specs=pl.BlockSpec(blk, lambda b: (b, 0)),
        out_shape=jax.ShapeDtypeStruct((bsz * ctx_len, WB), BF),
        compiler_params=_cparams(1))(pc, pc, pc)


def _filt_kernel(z_ref, w1_ref, b1_ref, w2_ref, b2_ref, fr_ref, w3_ref, dl_ref, h_ref, ss_ref, *, tm):
    i = pl.program_id(0)
    hp = lax.Precision.HIGHEST
    z = z_ref[...]
    h = jnp.sin(fr_ref[0:1, :] * (jnp.dot(z, w1_ref[...], precision=hp, preferred_element_type=F32)
                                  + b1_ref[...]))
    h = jnp.sin(fr_ref[1:2, :] * (jnp.dot(h, w2_ref[...], precision=hp, preferred_element_type=F32)
                                  + b2_ref[...]))
    h = jnp.dot(h, w3_ref[...], precision=hp, preferred_element_type=F32)
    decay = jnp.exp(-z[:, 0:1] * dl_ref[...])
    h = h * jnp.concatenate([decay] * (2 * HY_ORDER), axis=-1)
    h_ref[...] = h
    row = i * tm + lax.broadcasted_iota(jnp.int32, h.shape, 0)
    col = lax.broadcasted_iota(jnp.int32, h.shape, 1)
    is_bwd = (col // WD) % 2 == 1
    hm = jnp.where(is_bwd & (row == 0), 0.0, h)

    @pl.when(i == 0)
    def _():
        ss_ref[...] = jnp.zeros_like(ss_ref)

    ss_ref[...] += jnp.sum(hm * hm, axis=0, keepdims=True)


def _hyena_filters(length, w1, b1, w2, b2, freq, w3):
    t = jnp.arange(length, dtype=F32)
    t_norm = t / max(length - 1, 1)
    bands = jnp.linspace(1e-4, HY_PE_BANDS - 1, HY_PE_BANDS, dtype=F32)
    ang = (2.0 * math.pi / length) * t[:, None] * bands[None, :]
    z = jnp.concatenate([t_norm[:, None], jnp.cos(ang), -jnp.sin(ang)], axis=-1)
    pe_pad = 128
    z = jnp.pad(z, ((0, 0), (0, pe_pad - HY_PE_DIM)))
    w1p = jnp.pad(w1, ((0, pe_pad - HY_PE_DIM), (0, 0)))
    deltas = jnp.abs(jnp.linspace(math.log(HY_DECAY_TARGET) / HY_SLOW_DECAY,
                                  math.log(HY_DECAY_TARGET) / HY_FAST_DECAY, WD, dtype=F32))[None]
    tm = min(512, length)
    nf = HY_ORDER * 2 * WD
    h, ss = pl.pallas_call(
        functools.partial(_filt_kernel, tm=tm), grid=(length // tm,),
        in_specs=[pl.BlockSpec((tm, pe_pad), lambda i: (i, 0)),
                  _resident(w1p.shape), _resident((1, HY_FILT_HID)), _resident(w2.shape),
                  _resident((1, HY_FILT_HID)), _resident(freq.shape), _resident(w3.shape),
                  _resident(deltas.shape)],
        out_specs=[pl.BlockSpec((tm, nf), lambda i: (i, 0)), pl.BlockSpec((1, nf), lambda i: (0, 0))],
        out_shape=[jax.ShapeDtypeStruct((length, nf), F32), jax.ShapeDtypeStruct((1, nf), F32)],
        compiler_params=_cparams(1, "arbitrary"))(z, w1p, b1[None], w2, b2[None], freq, w3, deltas)
    h = h.reshape(length, HY_ORDER, 2, WD)
    ss = ss.reshape(HY_ORDER, 2, WD)
    return h, ss[:, 0] + ss[:, 1]


def _circular_kernel(h, o):
    return jnp.concatenate([h[:, o, 0], jnp.zeros((1, WD), F32), h[1:, o, 1][::-1]], axis=0)


def _cis(num, den):
    ang = (-2.0 * math.pi / den) * (num % den).astype(F32)
    return jnp.cos(ang), jnp.sin(ang)


def _stack(re, im):
    return jnp.concatenate([jnp.concatenate([re, -im], axis=1), jnp.concatenate([im, re], axis=1)], axis=0)


def _dft_tables(n1):
    n = n1 * FFT_N2
    i1 = jnp.arange(n1)
    fr, fi = _cis(i1[:, None] * i1[None, :], n1)
    half = n1 // 2
    w_fwd = _stack(fr[:, :half], fi[:, :half]).astype(BF)
    w_real = jnp.concatenate([fr, fi], axis=0).astype(BF)
    w_inv = (_stack(fr[:half, :], -fi[:half, :]) / n).astype(BF)
    i2 = jnp.arange(FFT_N2)
    num = i2[None, None, :] * (n1 * i2[None, :, None] + i1[:, None, None])
    gr, gi = _cis(num, n)
    gs = jnp.concatenate([jnp.concatenate([gr, -gi], axis=2), jnp.concatenate([gi, gr], axis=2)], axis=1)
    gs = gs.astype(BF)
    return w_fwd, w_real, w_inv, gs, jnp.swapaxes(gs, 1, 2)


def _dft_tables_direct(length):
    n = 2 * length
    i = jnp.arange(n)
    fr, fi = _cis(i[:, None] * i[None, :], n)
    w_fwd = _stack(fr[:, :length], fi[:, :length]).astype(BF)
    w_real = jnp.concatenate([fr, fi], axis=0).astype(BF)
    w_inv = (_stack(fr[:length, :], -fi[:length, :]) / n).astype(BF)
    return w_fwd, w_real, w_inv


def _stage_b_spec_kernel(a_ref, gs_ref, ss_ref, h_ref, *, kb):
    scale = lax.rsqrt(ss_ref[...] + 1e-6)
    for j in range(kb):
        a = jnp.concatenate([a_ref[0, j], a_ref[1, j]], axis=0)
        y = _dot(gs_ref[j], a) * scale
        h_ref[0, j] = y[:FFT_N2]
        h_ref[1, j] = y[FFT_N2:]


def _stage_b_conv_kernel(a_ref, gs_ref, gst_ref, h_ref, o_ref, *, kb):
    for j in range(kb):
        a = jnp.concatenate([a_ref[0, j], a_ref[1, j]], axis=0)
        y = _dot(gs_ref[j], a)
        yr, yi = y[:FFT_N2], y[FFT_N2:]
        hr, hi = h_ref[0, j], h_ref[1, j]
        z = jnp.concatenate([yr * hr - yi * hi, yr * hi + yi * hr], axis=0).astype(BF)
        b = _dot(gst_ref[j], z)
        o_ref[0, j] = b[:FFT_N2].astype(BF)
        o_ref[1, j] = b[FFT_N2:].astype(BF)


def _stage_b(a, gs, *, gst=None, spec=None, ss=None, kb=8):
    n1 = a.shape[1]
    kb = min(kb, n1)
    dblk = pl.BlockSpec((2, kb, FFT_N2, WD), lambda i: (0, i, 0, 0))
    gblk = pl.BlockSpec((kb, 2 * FFT_N2, 2 * FFT_N2), lambda i: (i, 0, 0))
    if spec is None:
        return pl.pallas_call(
            functools.partial(_stage_b_spec_kernel, kb=kb), grid=(n1 // kb,),
            in_specs=[dblk, gblk, _resident((1, WD))], out_specs=dblk,
            out_shape=jax.ShapeDtypeStruct(a.shape, F32), compiler_params=_cparams(1))(a, gs, ss)
    return pl.pallas_call(
        functools.partial(_stage_b_conv_kernel, kb=kb), grid=(n1 // kb,),
        in_specs=[dblk, gblk, gblk, dblk], out_specs=dblk,
        out_shape=jax.ShapeDtypeStruct(a.shape, BF), compiler_params=_cparams(1))(a, gs, gst, spec)


def _cmul_kernel(x_ref, h_ref, ss_ref, o_ref, *, half):
    scale = lax.rsqrt(ss_ref[...] + 1e-6)
    xr, xi = x_ref[:half, :], x_ref[half:, :]
    hr, hi = h_ref[:half, :] * scale, h_ref[half:, :] * scale
    o_ref[:half, :] = (xr * hr - xi * hi).astype(BF)
    o_ref[half:, :] = (xr * hi + xi * hr).astype(BF)


def _cmul(x, h, ss):
    return pl.pallas_call(
        functools.partial(_cmul_kernel, half=x.shape[0] // 2), grid=(1,),
        in_specs=[_resident(x.shape), _resident(h.shape), _resident(ss.shape)],
        out_specs=pl.BlockSpec(x.shape, lambda i: (0, 0)),
        out_shape=jax.ShapeDtypeStruct(x.shape, BF), compiler_params=_cparams(1))(x, h, ss)


def _hyena_long(hv, hx, filt, ss, bias, tables, *, bsz, seq_len):
    assert bsz == 2
    n1 = 2 * seq_len // FFT_N2
    rows = bsz * seq_len // FFT_N2
    wide = FFT_N2 * WD
    w_fwd, w_real, w_inv, gs, gst = tables
    u = hv.reshape(rows, wide)
    for o in range(HY_ORDER):
        kern = _circular_kernel(filt, o).reshape(n1, wide)
        spec = _stage_b(_mm(w_real, kern, bn=2048, out_dtype=BF).reshape(2, n1, FFT_N2, WD), gs,
                        ss=ss[o][None])
        a = _mm(w_fwd, u, bn=2048, out_dtype=BF).reshape(2, n1, FFT_N2, WD)
        b = _stage_b(a, gs, gst=gst, spec=spec).reshape(2 * n1, wide)
        row = jnp.tile(bias[o], FFT_N2)[None]
        u = _mm(w_inv, b, bn=2048, out_dtype=BF, epi=(hx[o].reshape(rows, wide), u, row))
    return u.reshape(bsz * seq_len, WD)


def _hyena_short_seq(hv, hx, filt, ss, bias, tables, *, bsz):
    assert bsz == 2
    w_fwd, w_real, w_inv = tables
    u = hv
    for o in range(HY_ORDER):
        spec = _mm(w_real, _circular_kernel(filt, o), bn=WD, out_dtype=F32)
        x = _mm(w_fwd, u, bn=WD, out_dtype=F32)
        z = _cmul(x, spec, ss[o][None])
        u = _mm(w_inv, z, bn=WD, out_dtype=BF, epi=(hx[o], u, bias[o][None]))
    return u


def _merge_kernel(x_ref, mod_ref, ya_ref, yb_ref, yc_ref, yd_ref, gate_ref, wp_ref, wo_ref, bo_ref,
                  g_ref, b_ref, o_ref, *, alpha):
    x = x_ref[...]
    m = jnp.zeros(x.shape, F32)
    for br, y_ref in enumerate((ya_ref, yb_ref, yc_ref, yd_ref)):
        gate = jax.nn.sigmoid(gate_ref[:, br * D_MODEL:(br + 1) * D_MODEL].astype(F32))
        m = m + gate * _dot(y_ref[...], wp_ref[br])
    out = _dot(m.astype(BF), wo_ref[...]) + bo_ref[...]
    y = alpha * x + mod_ref[0][5:6] * out
    o_ref[...] = _ln(y) * g_ref[...] + b_ref[...]


def _merge(x, mod, ya, yb, yc, yd, p, wp, wo, bo, g, b, *, tm, group_tiles, alpha):
    n = x.shape[0]
    yspec = pl.BlockSpec((tm, WD), lambda i: (i, 0))
    return pl.pallas_call(
        functools.partial(_merge_kernel, alpha=alpha), grid=(n // tm,),
        in_specs=[pl.BlockSpec((tm, D_MODEL), lambda i: (i, 0)),
                  pl.BlockSpec((1, N_MOD, D_MODEL), lambda i: (i // group_tiles, 0, 0)),
                  yspec, yspec, yspec, yspec,
                  pl.BlockSpec((tm, N_BRANCH * D_MODEL), lambda i: (i, 0)),
                  _resident(wp.shape), _resident(wo.shape), _resident((1, D_MODEL)),
                  _resident((1, D_MODEL)), _resident((1, D_MODEL))],
        out_specs=pl.BlockSpec((tm, D_MODEL), lambda i: (i, 0)),
        out_shape=jax.ShapeDtypeStruct((n, D_MODEL), F32),
        compiler_params=_cparams(1))(x, mod, ya, yb, yc, yd, p, wp, wo, bo, g, b)


def kernel(x, c, ctx, c_ctx, w_mod, b_mod, post_ln_g, post_ln_b, ffn_w_in, ffn_w_out, w_in, b_in,
           conf_dw_w, conf_dw_b, conf_ln_g, conf_ln_b, conf_w_proj, na_rpb, na_w_proj, sc_conv_w,
           sc_w_proj, hy_sconv_w, hy_sconv_b, hy_w1, hy_b1, hy_w2, hy_b2, hy_freq, hy_w3, hy_bias,
           hy_w_proj, w_out, b_out):
    bsz, n_lat, _ = x.shape
    ctx_len = ctx.shape[1]
    depth = w_mod.shape[0]
    alpha = (2 * depth) ** 0.25
    rows = n_lat // GRID_W

    tm = 512
    tmc = min(tm, ctx_len)
    tp = 256
    tpc = min(tp, ctx_len)

    xl = x.reshape(bsz * n_lat, D_MODEL)
    xc = ctx.reshape(bsz * ctx_len, D_MODEL)
    lat_tiles = n_lat // tm
    ctx_tiles = bsz * ctx_len // tmc

    tables = _dft_tables(2 * n_lat // FFT_N2)
    tables_c = _dft_tables_direct(ctx_len)

    cond = jnp.concatenate([c, c_ctx[None], jnp.zeros((8 - bsz - 1, D_MODEL), F32)], axis=0)

    for l in range(depth):
        last = l == depth - 1
        mod_all = _mm(cond, w_mod[l], bn=1024, out_dtype=F32, bias=b_mod[l][None], a_silu=True)
        mod_all = mod_all.reshape(8, N_MOD, D_MODEL)
        mod = mod_all[:bsz]
        mod_c = mod_all[bsz:bsz + 1]

        wl = w_in[l]
        w_perm = jnp.concatenate([wl[:, OFF_G:], wl[:, OFF_A:OFF_Q], wl[:, OFF_SB:OFF_G],
                                  wl[:, OFF_Q:OFF_SB]], axis=1).astype(BF)
        bl = b_in[l]
        b_perm = jnp.concatenate([bl[OFF_G:], bl[OFF_A:OFF_Q], bl[OFF_SB:OFF_G], bl[OFF_Q:OFF_SB]])[None]
        ffn_wi = ffn_w_in[l].astype(BF)
        ffn_wo = ffn_w_out[l].astype(BF)
        ln_g = post_ln_g[l][:, None, :]
        ln_b = post_ln_b[l][:, None, :]
        wp = jnp.stack([conf_w_proj[l], na_w_proj[l], sc_w_proj[l], hy_w_proj[l]]).astype(BF)
        wo = w_out[l].astype(BF)
        conv_args = (conf_dw_w[l], conf_dw_b[l][None], conf_ln_g[l][None], conf_ln_b[l][None],
                     sc_conv_w[l], hy_sconv_w[l], hy_sconv_b[l][None])
        filt_args = (hy_w1[l], hy_b1[l], hy_w2[l], hy_b2[l], hy_freq[l], hy_w3[l])

        xl = _ffn(xl, mod, ffn_wi[0], ffn_wo[0], ln_g[0], ln_b[0], m0=0, tm=tm,
                  group_tiles=lat_tiles, alpha=alpha)
        xc = _ffn(xc, mod_c, ffn_wi[0], ffn_wo[0], ln_g[0], ln_b[0], m0=0, tm=tmc,
                  group_tiles=ctx_tiles, alpha=alpha)

        p = _inproj(xl, mod, w_perm, b_perm, tm=tm, group_tiles=lat_tiles)
        pc = _inproj(xc, mod_c, w_perm, b_perm, tm=tmc, group_tiles=ctx_tiles)

        ya, yc, hv, h1, h2 = _prep(p, *conv_args, tm=tp, seq_len=n_lat)
        attn = _na(p, pc, _na_bias_table(na_rpb[l], rows), bsz=bsz, seq_len=n_lat)
        filt, ss = _hyena_filters(n_lat, *filt_args)
        yd = _hyena_long(hv, (h1, h2), filt, ss, hy_bias[l], tables, bsz=bsz, seq_len=n_lat)
        xl = _merge(xl, mod, ya, attn, yc, yd, p, wp, wo, b_out[l][None], ln_g[1], ln_b[1],
                    tm=tm, group_tiles=lat_tiles, alpha=alpha)

        if not last:
            ya, yc, hv, h1, h2 = _prep(pc, *conv_args, tm=tpc, seq_len=ctx_len)
            attn_c = _ctx_attn(pc, bsz=bsz, ctx_len=ctx_len)
            filt_c, ss_c = _hyena_filters(ctx_len, *filt_args)
            yd = _hyena_short_seq(hv, (h1, h2), filt_c, ss_c, hy_bias[l], tables_c, bsz=bsz)
            xc = _merge(xc, mod_c, ya, attn_c, yc, yd, pc, wp, wo, b_out[l][None], ln_g[1], ln_b[1],
                        tm=tmc, group_tiles=ctx_tiles, alpha=alpha)

        xl = _ffn(xl, mod, ffn_wi[1], ffn_wo[1], ln_g[2], ln_b[2], m0=6, tm=tm,
                  group_tiles=lat_tiles, alpha=alpha)
        if not last:
            xc = _ffn(xc, mod_c, ffn_wi[1], ffn_wo[1], ln_g[2], ln_b[2], m0=6, tm=tmc,
                      group_tiles=ctx_tiles, alpha=alpha)

    return xl.reshape(bsz, n_lat, D_MODEL)
```

```python
import functools
import math

import jax
import jax.numpy as jnp
import numpy as np
from jax import lax
from jax.experimental import pallas as pl
from jax.experimental.pallas import tpu as pltpu

D_MODEL = 1024
GRID_W = 64
N_BRANCH = 4
WA = 256
CONF_K = 31
NA_HEADS = 4
NA_HEAD_DIM = 64
WB = NA_HEADS * NA_HEAD_DIM
NA_WIN_ROWS = 8
NA_WIN_COLS = 16
ATTN_SCALE = NA_HEAD_DIM ** -0.5
WC = 256
SC_K = 3
WD = 256
HY_ORDER = 2
HY_SHORT_K = 3
HY_PE_BANDS = 16
HY_PE_DIM = 1 + 2 * HY_PE_BANDS
HY_FILT_HID = 64
HY_FAST_DECAY = 0.3
HY_SLOW_DECAY = 1.5
HY_DECAY_TARGET = 1e-2
D_FF = 2816
N_MOD = 9
LN_EPS = 1e-5

OFF_A = 0
OFF_Q = OFF_A + 2 * WA
OFF_K = OFF_Q + WB
OFF_V = OFF_K + WB
OFF_SB = OFF_V + WB
OFF_SC = OFF_SB + WC
OFF_SX = OFF_SC + WC
OFF_HV = OFF_SX + WC
OFF_G = OFF_HV + (1 + HY_ORDER) * WD
P_IN = OFF_G + N_BRANCH * D_MODEL

NEW_G = 0
NEW_CONV = N_BRANCH * D_MODEL
CONV_W = 2 * WA + 3 * WC + 3 * WD
NEW_Q = NEW_CONV + CONV_W

BF = jnp.bfloat16
F32 = jnp.float32

VMEM_LIMIT_BYTES = 56 * 1024 * 1024
FFN_CHUNK = 1408
INPROJ_CHUNK = 1152
HALO = 16
NA_QROWS = 4
NA_TOK = NA_QROWS * GRID_W
FFT_N2 = 128
NEG_INF = -1e30


def _cparams(n_axes, semantics="parallel"):
    return pltpu.CompilerParams(dimension_semantics=(semantics,) * n_axes,
                                vmem_limit_bytes=VMEM_LIMIT_BYTES)


def _resident(shape):
    nd = len(shape)
    return pl.BlockSpec(shape, lambda *_: (0,) * nd, pipeline_mode=pl.Buffered(1))


def _ln(x):
    mu = jnp.mean(x, axis=-1, keepdims=True)
    xc = x - mu
    var = jnp.mean(xc * xc, axis=-1, keepdims=True)
    return xc * lax.rsqrt(var + LN_EPS)


def _dot(a, b):
    return jnp.dot(a, b, preferred_element_type=F32)


def _dot_nt(a, b):
    return lax.dot_general(a, b, (((1,), (1,)), ((), ())), preferred_element_type=F32)


def _mm_kernel(*refs, a_silu, has_bias, has_epi):
    a_ref, b_ref = refs[0], refs[1]
    pos = 2
    a = a_ref[...]
    if a_silu:
        a = a.astype(F32)
        a = a * jax.nn.sigmoid(a)
    acc = _dot(a.astype(BF), b_ref[...].astype(BF))
    if has_bias:
        acc = acc + refs[pos][...]
        pos += 1
    if has_epi:
        gate_ref, u_ref, row_ref = refs[pos], refs[pos + 1], refs[pos + 2]
        pos += 3
        acc = gate_ref[...].astype(F32) * (acc + u_ref[...].astype(F32) * row_ref[...])
    o_ref = refs[pos]
    o_ref[...] = acc.astype(o_ref.dtype)


def _mm(a, b, *, bn, out_dtype, bias=None, epi=None, a_silu=False):
    m, k = a.shape
    n = b.shape[1]
    bn = min(bn, n)
    ops = [a, b]
    specs = [_resident((m, k)), pl.BlockSpec((k, bn), lambda j: (0, j))]
    if bias is not None:
        ops.append(bias)
        specs.append(pl.BlockSpec((1, bn), lambda j: (0, j)))
    if epi is not None:
        gate, u, row = epi
        ops += [gate, u, row]
        specs += [pl.BlockSpec((m, bn), lambda j: (0, j)),
                  pl.BlockSpec((m, bn), lambda j: (0, j)),
                  pl.BlockSpec((1, bn), lambda j: (0, j))]
    kern = functools.partial(_mm_kernel, a_silu=a_silu, has_bias=bias is not None,
                             has_epi=epi is not None)
    return pl.pallas_call(
        kern, grid=(n // bn,), in_specs=specs,
        out_specs=pl.BlockSpec((m, bn), lambda j: (0, j)),
        out_shape=jax.ShapeDtypeStruct((m, n), out_dtype),
        compiler_params=_cparams(1))(*ops)


def _modulated(x, mod, m0):
    shift = mod[m0:m0 + 1]
    scale = mod[m0 + 1:m0 + 2]
    return _ln(x) * (1.0 + scale) + shift


def _ffn_kernel(x_ref, mod_ref, wi_ref, wo_ref, g_ref, b_ref, o_ref, *, m0, alpha):
    x = x_ref[...]
    mod = mod_ref[0]
    h = _modulated(x, mod, m0).astype(BF)
    acc = jnp.zeros(x.shape, F32)
    for c in range(D_FF // FFN_CHUNK):
        lo = c * FFN_CHUNK
        a = _dot(h, wi_ref[:, lo:lo + FFN_CHUNK])
        g = _dot(h, wi_ref[:, D_FF + lo:D_FF + lo + FFN_CHUNK])
        u = (g * jax.nn.sigmoid(g) * a).astype(BF)
        acc = acc + _dot(u, wo_ref[lo:lo + FFN_CHUNK, :])
    y = alpha * x + (0.5 * mod[m0 + 2:m0 + 3]) * acc
    o_ref[...] = _ln(y) * g_ref[...] + b_ref[...]


def _ffn(x, mod, w_in, w_out, g, b, *, m0, tm, group_tiles, alpha):
    n = x.shape[0]
    kern = functools.partial(_ffn_kernel, m0=m0, alpha=alpha)
    return pl.pallas_call(
        kern, grid=(n // tm,),
        in_specs=[pl.BlockSpec((tm, D_MODEL), lambda i: (i, 0)),
                  pl.BlockSpec((1, N_MOD, D_MODEL), lambda i: (i // group_tiles, 0, 0)),
                  _resident(w_in.shape), _resident(w_out.shape),
                  _resident((1, D_MODEL)), _resident((1, D_MODEL))],
        out_specs=pl.BlockSpec((tm, D_MODEL), lambda i: (i, 0)),
        out_shape=jax.ShapeDtypeStruct((n, D_MODEL), F32),
        compiler_params=_cparams(1))(x, mod, w_in, w_out, g, b)


def _inproj_kernel(x_ref, mod_ref, w_ref, b_ref, o_ref, *, m0):
    h = _modulated(x_ref[...], mod_ref[0], m0).astype(BF)
    for c in range(P_IN // INPROJ_CHUNK):
        lo = c * INPROJ_CHUNK
        o_ref[:, lo:lo + INPROJ_CHUNK] = (
            _dot(h, w_ref[:, lo:lo + INPROJ_CHUNK]) + b_ref[:, lo:lo + INPROJ_CHUNK]).astype(BF)


def _inproj(x, mod, w, b, *, tm, group_tiles):
    n = x.shape[0]
    return pl.pallas_call(
        functools.partial(_inproj_kernel, m0=3), grid=(n // tm,),
        in_specs=[pl.BlockSpec((tm, D_MODEL), lambda i: (i, 0)),
                  pl.BlockSpec((1, N_MOD, D_MODEL), lambda i: (i // group_tiles, 0, 0)),
                  _resident(w.shape), _resident((1, P_IN))],
        out_specs=pl.BlockSpec((tm, P_IN), lambda i: (i, 0)),
        out_shape=jax.ShapeDtypeStruct((n, P_IN), BF),
        compiler_params=_cparams(1))(x, mod, w, b)


def _prep_kernel(x_ref, prev_ref, next_ref, cw_ref, cb_ref, lg_ref, lb_ref, sw_ref, hw_ref, hb_ref,
                 ya_ref, yc_ref, hv_ref, h1_ref, h2_ref, buf_a, buf_c, buf_h, *, tm, seq_tiles):
    i = pl.program_id(0)
    pos = i % seq_tiles
    keep_prev = jnp.where(pos == 0, 0.0, 1.0)
    keep_next = jnp.where(pos == seq_tiles - 1, 0.0, 1.0)

    def fill(t, lo, hi):
        glu = t[:, 0:WA] * jax.nn.sigmoid(t[:, WA:2 * WA])
        buf_a[lo:hi, :] = glu
        buf_c[lo:hi, :] = t[:, 3 * WA:4 * WA] * t[:, 4 * WA:5 * WA]
        buf_h[lo:hi, :] = t[:, 5 * WA:8 * WA]

    main = x_ref[...].astype(F32)
    fill(prev_ref[...].astype(F32) * keep_prev, 0, HALO)
    fill(main, HALO, HALO + tm)
    fill(next_ref[...].astype(F32) * keep_next, HALO + tm, 2 * HALO + tm)

    acc = jnp.zeros((tm, WA), F32) + cb_ref[...]
    for j in range(CONF_K):
        off = HALO - CONF_K // 2 + j
        acc = acc + cw_ref[j:j + 1, :] * buf_a[off:off + tm, :]
    u = _ln(acc) * lg_ref[...] + lb_ref[...]
    ya_ref[...] = (u * jax.nn.sigmoid(u)).astype(BF)

    acc = jnp.zeros((tm, WC), F32)
    for j in range(SC_K):
        off = HALO - SC_K // 2 + j
        acc = acc + sw_ref[j:j + 1, :] * buf_c[off:off + tm, :]
    yc_ref[...] = (main[:, 2 * WA:3 * WA] * acc).astype(BF)

    acc = jnp.zeros((tm, 3 * WD), F32) + hb_ref[...]
    for j in range(HY_SHORT_K):
        off = HALO - HY_SHORT_K // 2 + j
        acc = acc + hw_ref[j:j + 1, :] * buf_h[off:off + tm, :]
    hv_ref[...] = acc[:, 0:WD]
    h1_ref[...] = acc[:, WD:2 * WD]
    h2_ref[...] = acc[:, 2 * WD:3 * WD]


def _prep(p, cw, cb, lg, lb, sw, hw, hb, *, tm, seq_len):
    n = p.shape[0]
    seq_tiles = seq_len // tm
    hb_per_tile = tm // HALO
    n_halo_blocks = n // HALO
    cblk = NEW_CONV // CONV_W
    kern = functools.partial(_prep_kernel, tm=tm, seq_tiles=seq_tiles)
    out_bf = jax.ShapeDtypeStruct((n, WD), BF)
    out_f32 = jax.ShapeDtypeStruct((n, WD), F32)
    ospec = pl.BlockSpec((tm, WD), lambda i: (i, 0))
    return pl.pallas_call(
        kern, grid=(n // tm,),
        in_specs=[pl.BlockSpec((tm, CONV_W), lambda i: (i, cblk)),
                  pl.BlockSpec((HALO, CONV_W),
                               lambda i: (jnp.maximum(i * hb_per_tile - 1, 0), cblk)),
                  pl.BlockSpec((HALO, CONV_W),
                               lambda i: (jnp.minimum((i + 1) * hb_per_tile, n_halo_blocks - 1), cblk)),
                  _resident(cw.shape), _resident(cb.shape), _resident(lg.shape), _resident(lb.shape),
                  _resident(sw.shape), _resident(hw.shape), _resident(hb.shape)],
        out_specs=[ospec] * 5, out_shape=[out_bf, out_bf, out_f32, out_f32, out_f32],
        scratch_shapes=[pltpu.VMEM((tm + 2 * HALO, WA), F32),
                        pltpu.VMEM((tm + 2 * HALO, WC), F32),
                        pltpu.VMEM((tm + 2 * HALO, 3 * WD), F32)],
        compiler_params=_cparams(1))(p, p, p, cw, cb, lg, lb, sw, hw, hb)


def _na_kernel(q_ref, k0_ref, k1_ref, k2_ref, v0_ref, v1_ref, v2_ref, kc_ref, vc_ref, bias_ref, o_ref):
    q = q_ref[...] * ATTN_SCALE
    k = jnp.concatenate([k0_ref[...], k1_ref[...], k2_ref[...]], axis=0)
    v = jnp.concatenate([v0_ref[...], v1_ref[...], v2_ref[...]], axis=0)
    kc = kc_ref[...]
    vc = vc_ref[...]
    outs = []
    for h in range(NA_HEADS):
        sl = slice(h * NA_HEAD_DIM, (h + 1) * NA_HEAD_DIM)
        qh = q[:, sl]
        s_nb = _dot_nt(qh, k[:, sl]) + bias_ref[0, h]
        s_cx = _dot_nt(qh, kc[:, sl])
        m = jnp.maximum(jnp.max(s_nb, axis=-1, keepdims=True), jnp.max(s_cx, axis=-1, keepdims=True))
        p_nb = jnp.exp(s_nb - m)
        p_cx = jnp.exp(s_cx - m)
        den = jnp.sum(p_nb, axis=-1, keepdims=True) + jnp.sum(p_cx, axis=-1, keepdims=True)
        o = _dot(p_nb.astype(BF), v[:, sl]) + _dot(p_cx.astype(BF), vc[:, sl])
        outs.append(o / den)
    o_ref[...] = jnp.concatenate(outs, axis=-1).astype(BF)


def _rpb_expand_kernel(rpb_ref, onehot_ref, o_ref):
    o_ref[...] = jnp.dot(rpb_ref[...], onehot_ref[...], precision=lax.Precision.HIGHEST,
                         preferred_element_type=F32)


def _na_bias_table(rpb, rows):
    assert rows >= NA_WIN_ROWS and rows % NA_QROWS == 0 and rows // NA_QROWS >= 3
    n_dr, n_dc = 2 * NA_WIN_ROWS - 1, 2 * NA_WIN_COLS - 1
    cols = np.arange(GRID_W)
    c0 = np.clip(cols - NA_WIN_COLS // 2, 0, GRID_W - NA_WIN_COLS)
    dc = cols[None, :] - cols[:, None] + (NA_WIN_COLS - 1)
    ok_c = (cols[None, :] >= c0[:, None]) & (cols[None, :] < c0[:, None] + NA_WIN_COLS)
    onehot = np.zeros((128, GRID_W, GRID_W), np.float32)
    qq, kk = np.nonzero(ok_c)
    onehot[dc[qq, kk], qq, kk] = 1.0
    rpb2 = jnp.pad(rpb.reshape(NA_HEADS * n_dr, n_dc).astype(F32),
                   ((0, 64 - NA_HEADS * n_dr), (0, 128 - n_dc)))
    t = pl.pallas_call(
        _rpb_expand_kernel, grid=(1,),
        in_specs=[_resident((64, 128)), _resident((128, GRID_W * GRID_W))],
        out_specs=pl.BlockSpec((64, GRID_W * GRID_W), lambda i: (0, 0)),
        out_shape=jax.ShapeDtypeStruct((64, GRID_W * GRID_W), F32),
        compiler_params=_cparams(1))(rpb2, jnp.asarray(onehot.reshape(128, GRID_W * GRID_W)))
    t = t[:NA_HEADS * n_dr].reshape(NA_HEADS, n_dr, GRID_W, GRID_W)
    full = jnp.concatenate(
        [jnp.concatenate([t[:, s - a + NA_WIN_ROWS - 1 - NA_QROWS] for s in range(3 * NA_QROWS)], axis=-1)
         for a in range(NA_QROWS)], axis=1)
    wr = NA_WIN_ROWS
    n_blk = rows // NA_QROWS
    tabs = []
    for blk in (0, 1, n_blk - 1):
        qr = blk * NA_QROWS + np.arange(NA_QROWS)
        kr = (blk - 1) * NA_QROWS + np.arange(3 * NA_QROWS)
        r0 = np.clip(qr - wr // 2, 0, rows - wr)
        ok_r = ((kr[None, :] >= r0[:, None]) & (kr[None, :] < r0[:, None] + wr)
                & (kr[None, :] >= 0) & (kr[None, :] < rows))
        ok = (ok_r[:, None, :, None] & ok_c[None, :, None, :]).reshape(NA_TOK, 3 * NA_TOK)
        tabs.append(jnp.where(jnp.asarray(ok)[None], full, NEG_INF))
    return jnp.stack(tabs)


def _na(p, pc, bias_tab, *, bsz, seq_len):
    t = seq_len // NA_TOK
    qb, kb, vb = NEW_Q // WB, NEW_Q // WB + 1, NEW_Q // WB + 2
    blk = (NA_TOK, WB)
    cblk = (pc.shape[0] // bsz, WB)

    def nbr(j, col):
        return pl.BlockSpec(blk, lambda b, i: (b * t + jnp.clip(i - 1 + j, 0, t - 1), col))

    return pl.pallas_call(
        _na_kernel, grid=(bsz, t),
        in_specs=[pl.BlockSpec(blk, lambda b, i: (b * t + i, qb)),
                  nbr(0, kb), nbr(1, kb), nbr(2, kb), nbr(0, vb), nbr(1, vb), nbr(2, vb),
                  pl.BlockSpec(cblk, lambda b, i: (b, kb)),
                  pl.BlockSpec(cblk, lambda b, i: (b, vb)),
                  pl.BlockSpec((1, NA_HEADS, NA_TOK, 3 * NA_TOK),
                               lambda b, i: (jnp.where(i == 0, 0, jnp.where(i == t - 1, 2, 1)), 0, 0, 0))],
        out_specs=pl.BlockSpec(blk, lambda b, i: (b * t + i, 0)),
        out_shape=jax.ShapeDtypeStruct((bsz * seq_len, WB), BF),
        compiler_params=_cparams(2))(p, p, p, p, p, p, p, pc, pc, bias_tab)


def _ctx_attn_kernel(q_ref, k_ref, v_ref, o_ref):
    q = q_ref[...] * ATTN_SCALE
    k = k_ref[...]
    v = v_ref[...]
    outs = []
    for h in range(NA_HEADS):
        sl = slice(h * NA_HEAD_DIM, (h + 1) * NA_HEAD_DIM)
        s = _dot_nt(q[:, sl], k[:, sl])
        m = jnp.max(s, axis=-1, keepdims=True)
        p = jnp.exp(s - m)
        den = jnp.sum(p, axis=-1, keepdims=True)
        outs.append(_dot(p.astype(BF), v[:, sl]) / den)
    o_ref[...] = jnp.concatenate(outs, axis=-1).astype(BF)


def _ctx_attn(pc, *, bsz, ctx_len):
    qb, kb, vb = NEW_Q // WB, NEW_Q // WB + 1, NEW_Q // WB + 2
    blk = (ctx_len, WB)
    return pl.pallas_call(
        _ctx_attn_kernel, grid=(bsz,),
        in_specs=[pl.BlockSpec(blk, lambda b: (b, qb)), pl.BlockSpec(blk, lambda b: (b, kb)),
                  pl.BlockSpec(blk, lambda b: (b, vb))],
        out_specs=pl.BlockSpec(blk, lambda b: (b, 0)),
        out_shape=jax.ShapeDtypeStruct((bsz * ctx_len, WB), BF),
        compiler_params=_cparams(1))(pc, pc, pc)


def _filt_kernel(zf_ref, zb_ref, w1_ref, b1_ref, w2_ref, b2_ref, fr_ref, w3_ref, dl_ref, k_ref, ss_ref,
                 *, tm):
    i = pl.program_id(0)
    hp = lax.Precision.HIGHEST
    zf = zf_ref[...]
    zb = zb_ref[...]
    w1 = w1_ref[...]
    pre = jnp.concatenate([jnp.dot(zf, w1, precision=hp, preferred_element_type=F32),
                           jnp.dot(zb, w1, precision=hp, preferred_element_type=F32)], axis=-1)
    h = jnp.sin(fr_ref[0:1, :] * (pre + b1_ref[...]))
    h = jnp.sin(fr_ref[1:2, :] * (jnp.dot(h, w2_ref[...], precision=hp, preferred_element_type=F32)
                                  + b2_ref[...]))
    k = jnp.dot(h, w3_ref[...], precision=hp, preferred_element_type=F32)
    dec_f = jnp.exp(-zf[:, 0:1] * dl_ref[...])
    dec_b = jnp.exp(-zb[:, 0:1] * dl_ref[...])
    k = k * jnp.concatenate([dec_f] * HY_ORDER + [dec_b] * HY_ORDER, axis=-1)
    row = i * tm + lax.broadcasted_iota(jnp.int32, k.shape, 0)
    col = lax.broadcasted_iota(jnp.int32, k.shape, 1)
    k = jnp.where((col >= HY_ORDER * WD) & (row == 0), 0.0, k)
    k_ref[0] = k[:, :HY_ORDER * WD]
    k_ref[1] = k[:, HY_ORDER * WD:]

    @pl.when(i == 0)
    def _():
        ss_ref[...] = jnp.zeros_like(ss_ref)

    sq = jnp.sum(k * k, axis=0, keepdims=True)
    ss_ref[...] += sq[:, :HY_ORDER * WD] + sq[:, HY_ORDER * WD:]


def _pos_features(t, length):
    t_norm = t / max(length - 1, 1)
    bands = jnp.linspace(1e-4, HY_PE_BANDS - 1, HY_PE_BANDS, dtype=F32)
    ang = (2.0 * math.pi / length) * t[:, None] * bands[None, :]
    z = jnp.concatenate([t_norm[:, None], jnp.cos(ang), -jnp.sin(ang)], axis=-1)
    return jnp.pad(z, ((0, 0), (0, 128 - HY_PE_DIM)))


def _block_diag(a, b):
    return jnp.concatenate([jnp.concatenate([a, jnp.zeros((a.shape[0], b.shape[1]), a.dtype)], axis=1),
                            jnp.concatenate([jnp.zeros((b.shape[0], a.shape[1]), a.dtype), b], axis=1)],
                           axis=0)


def _hyena_filters(length, w1, b1, w2, b2, freq, w3):
    t = jnp.arange(length, dtype=F32)
    zf = _pos_features(t, length)
    zb = _pos_features(length - t, length)
    w1p = jnp.pad(w1, ((0, 128 - HY_PE_DIM), (0, 0)))
    w3r = w3.reshape(HY_FILT_HID, HY_ORDER, 2, WD)
    w3bd = _block_diag(w3r[:, :, 0].reshape(HY_FILT_HID, HY_ORDER * WD),
                       w3r[:, :, 1].reshape(HY_FILT_HID, HY_ORDER * WD))
    w2bd = _block_diag(w2, w2)
    b1d = jnp.concatenate([b1, b1])[None]
    b2d = jnp.concatenate([b2, b2])[None]
    frd = jnp.concatenate([freq, freq], axis=1)
    deltas = jnp.abs(jnp.linspace(math.log(HY_DECAY_TARGET) / HY_SLOW_DECAY,
                                  math.log(HY_DECAY_TARGET) / HY_FAST_DECAY, WD, dtype=F32))[None]
    tm = min(512, length)
    nf = HY_ORDER * WD
    zspec = pl.BlockSpec((tm, 128), lambda i: (i, 0))
    k, ss = pl.pallas_call(
        functools.partial(_filt_kernel, tm=tm), grid=(length // tm,),
        in_specs=[zspec, zspec, _resident(w1p.shape), _resident(b1d.shape), _resident(w2bd.shape),
                  _resident(b2d.shape), _resident(frd.shape), _resident(w3bd.shape),
                  _resident(deltas.shape)],
        out_specs=[pl.BlockSpec((2, tm, nf), lambda i: (0, i, 0)), pl.BlockSpec((1, nf), lambda i: (0, 0))],
        out_shape=[jax.ShapeDtypeStruct((2, length, nf), F32), jax.ShapeDtypeStruct((1, nf), F32)],
        compiler_params=_cparams(1, "arbitrary"))(zf, zb, w1p, b1d, w2bd, b2d, frd, w3bd, deltas)
    return k.reshape(2 * length, nf), ss


def _cis(num, den):
    ang = (-2.0 * math.pi / den) * (num % den).astype(F32)
    return jnp.cos(ang), jnp.sin(ang)


def _stack(re, im):
    return jnp.concatenate([jnp.concatenate([re, -im], axis=1), jnp.concatenate([im, re], axis=1)], axis=0)


def _dft_tables(n1):
    n = n1 * FFT_N2
    i1 = jnp.arange(n1)
    fr, fi = _cis(i1[:, None] * i1[None, :], n1)
    half = n1 // 2
    w_fwd = _stack(fr[:, :half], fi[:, :half]).astype(BF)
    w_real = jnp.concatenate([fr, fi], axis=0).astype(BF)
    w_inv = (_stack(fr[:half, :], -fi[:half, :]) / n).astype(BF)
    i2 = jnp.arange(FFT_N2)
    num = i2[None, None, :] * (n1 * i2[None, :, None] + i1[:, None, None])
    gr, gi = _cis(num, n)
    gs = jnp.concatenate([jnp.concatenate([gr, -gi], axis=2), jnp.concatenate([gi, gr], axis=2)], axis=1)
    gs = gs.astype(BF)
    return w_fwd, w_real, w_inv, gs, jnp.swapaxes(gs, 1, 2)


def _dft_tables_direct(length):
    n = 2 * length
    i = jnp.arange(n)
    fr, fi = _cis(i[:, None] * i[None, :], n)
    w_fwd = _stack(fr[:, :length], fi[:, :length]).astype(BF)
    w_real = jnp.concatenate([fr, fi], axis=0).astype(BF)
    w_inv = (_stack(fr[:length, :], -fi[:length, :]) / n).astype(BF)
    return w_fwd, w_real, w_inv


def _stage_a_kernel(*refs, nb, has_epi):
    w_ref, x_ref = refs[0], refs[1]
    m, k = w_ref.shape
    if has_epi:
        g_ref, u_ref, row_ref, o_ref, xs, os_, gs, us = refs[2:]
        gs[...] = g_ref[...].reshape(m * nb, 128)
        us[...] = u_ref[...].reshape(m * nb, 128)
        row = row_ref[...]
    else:
        o_ref, xs, os_ = refs[2:]
    xs[...] = x_ref[...].reshape(k * nb, 128)
    w = w_ref[...]
    for j in range(nb):
        acc = _dot(w, xs[pl.ds(j, k, stride=nb), :].astype(BF))
        if has_epi:
            acc = gs[pl.ds(j, m, stride=nb), :] * (acc + us[pl.ds(j, m, stride=nb), :] * row)
        os_[pl.ds(j, m, stride=nb), :] = acc
    o_ref[...] = os_[...].reshape(m, nb, 128)


def _stage_a(w, x, *, epi=None, nb=16):
    m, k = w.shape
    c = x.shape[2]
    xblk = pl.BlockSpec((k, nb, 128), lambda j, l: (0, j, l))
    oblk = pl.BlockSpec((m, nb, 128), lambda j, l: (0, j, l))
    ops, specs = [w, x], [_resident((m, k)), xblk]
    scratch = [pltpu.VMEM((k * nb, 128), F32), pltpu.VMEM((m * nb, 128), F32)]
    if epi is not None:
        ops += list(epi)
        specs += [oblk, oblk, pl.BlockSpec((1, 128), lambda j, l: (0, l))]
        scratch += [pltpu.VMEM((m * nb, 128), F32), pltpu.VMEM((m * nb, 128), F32)]
    return pl.pallas_call(
        functools.partial(_stage_a_kernel, nb=nb, has_epi=epi is not None),
        grid=(FFT_N2 // nb, c // 128), in_specs=specs, out_specs=oblk, scratch_shapes=scratch,
        out_shape=jax.ShapeDtypeStruct((m, FFT_N2, c), F32), compiler_params=_cparams(2))(*ops)


def _stage_b_spec_kernel(a_ref, gs_ref, ss_ref, h_ref, *, kb):
    scale = lax.rsqrt(ss_ref[...] + 1e-6)
    for j in range(kb):
        a = jnp.concatenate([a_ref[0, j], a_ref[1, j]], axis=0).astype(BF)
        y = _dot(gs_ref[j], a) * scale
        h_ref[0, j] = y[:FFT_N2]
        h_ref[1, j] = y[FFT_N2:]


def _stage_b_conv_kernel(a_ref, gs_ref, gst_ref, h_ref, o_ref, *, kb):
    for j in range(kb):
        a = jnp.concatenate([a_ref[0, j], a_ref[1, j]], axis=0).astype(BF)
        y = _dot(gs_ref[j], a)
        yr, yi = y[:FFT_N2], y[FFT_N2:]
        hr, hi = h_ref[0, j], h_ref[1, j]
        z = jnp.concatenate([yr * hr - yi * hi, yr * hi + yi * hr], axis=0).astype(BF)
        b = _dot(gst_ref[j], z)
        o_ref[0, j] = b[:FFT_N2]
        o_ref[1, j] = b[FFT_N2:]


def _stage_b(a, gs, *, lane_blk=0, gst=None, spec=None, ss=None, kb=8):
    n1 = a.shape[1]
    kb = min(kb, n1)
    ablk = pl.BlockSpec((2, kb, FFT_N2, WD), lambda i: (0, i, 0, lane_blk))
    dblk = pl.BlockSpec((2, kb, FFT_N2, WD), lambda i: (0, i, 0, 0))
    gblk = pl.BlockSpec((kb, 2 * FFT_N2, 2 * FFT_N2), lambda i: (i, 0, 0))
    out = jax.ShapeDtypeStruct((2, n1, FFT_N2, WD), F32)
    if spec is None:
        return pl.pallas_call(
            functools.partial(_stage_b_spec_kernel, kb=kb), grid=(n1 // kb,),
            in_specs=[ablk, gblk, pl.BlockSpec((1, WD), lambda i: (0, lane_blk))], out_specs=dblk,
            out_shape=out, compiler_params=_cparams(1))(a, gs, ss)
    return pl.pallas_call(
        functools.partial(_stage_b_conv_kernel, kb=kb), grid=(n1 // kb,),
        in_specs=[ablk, gblk, gblk, dblk], out_specs=dblk,
        out_shape=out, compiler_params=_cparams(1))(a, gs, gst, spec)


def _cmul_kernel(x_ref, h_ref, ss_ref, o_ref, *, half):
    scale = lax.rsqrt(ss_ref[...] + 1e-6)
    xr, xi = x_ref[:half, :], x_ref[half:, :]
    hr, hi = h_ref[:half, :] * scale, h_ref[half:, :] * scale
    o_ref[:half, :] = (xr * hr - xi * hi).astype(BF)
    o_ref[half:, :] = (xr * hi + xi * hr).astype(BF)


def _cmul(x, h, ss, lane_blk):
    rows = x.shape[0]
    return pl.pallas_call(
        functools.partial(_cmul_kernel, half=rows // 2), grid=(1,),
        in_specs=[_resident(x.shape), pl.BlockSpec((rows, WD), lambda i: (0, lane_blk)),
                  pl.BlockSpec((1, WD), lambda i: (0, lane_blk))],
        out_specs=pl.BlockSpec(x.shape, lambda i: (0, 0)),
        out_shape=jax.ShapeDtypeStruct(x.shape, BF), compiler_params=_cparams(1))(x, h, ss)


def _hyena_long(hv, hx, kern, ss, bias, tables, *, bsz, seq_len):
    assert bsz == 2
    n1 = 2 * seq_len // FFT_N2
    w_fwd, w_real, w_inv, gs, gst = tables
    view = (bsz * seq_len // FFT_N2, FFT_N2, WD)
    filt_a = _stage_a(w_real, kern.reshape(n1, FFT_N2, HY_ORDER * WD))
    filt_a = filt_a.reshape(2, n1, FFT_N2, HY_ORDER * WD)
    u = hv.reshape(view)
    for o in range(HY_ORDER):
        spec = _stage_b(filt_a, gs, lane_blk=o, ss=ss)
        a = _stage_a(w_fwd, u).reshape(2, n1, FFT_N2, WD)
        b = _stage_b(a, gs, gst=gst, spec=spec).reshape(2 * n1, FFT_N2, WD)
        u = _stage_a(w_inv, b, epi=(hx[o].reshape(view), u, bias[o][None]))
    return u.reshape(bsz * seq_len, WD)


def _hyena_short_seq(hv, hx, kern, ss, bias, tables, *, bsz):
    assert bsz == 2
    w_fwd, w_real, w_inv = tables
    spec = _mm(w_real, kern, bn=HY_ORDER * WD, out_dtype=F32)
    u = hv
    for o in range(HY_ORDER):
        x = _mm(w_fwd, u, bn=WD, out_dtype=F32)
        z = _cmul(x, spec, ss, o)
        u = _mm(w_inv, z, bn=WD, out_dtype=F32, epi=(hx[o], u, bias[o][None]))
    return u


def _merge_kernel(x_ref, mod_ref, ya_ref, yb_ref, yc_ref, yd_ref, gate_ref, wp_ref, wo_ref, bo_ref,
                  g_ref, b_ref, o_ref, *, alpha):
    x = x_ref[...]
    m = jnp.zeros(x.shape, F32)
    for br, y_ref in enumerate((ya_ref, yb_ref, yc_ref, yd_ref)):
        gate = jax.nn.sigmoid(gate_ref[:, br * D_MODEL:(br + 1) * D_MODEL].astype(F32))
        m = m + gate * _dot(y_ref[...].astype(BF), wp_ref[br])
    out = _dot(m.astype(BF), wo_ref[...]) + bo_ref[...]
    y = alpha * x + mod_ref[0][5:6] * out
    o_ref[...] = _ln(y) * g_ref[...] + b_ref[...]


def _merge(x, mod, ya, yb, yc, yd, p, wp, wo, bo, g, b, *, tm, group_tiles, alpha):
    n = x.shape[0]
    yspec = pl.BlockSpec((tm, WD), lambda i: (i, 0))
    return pl.pallas_call(
        functools.partial(_merge_kernel, alpha=alpha), grid=(n // tm,),
        in_specs=[pl.BlockSpec((tm, D_MODEL), lambda i: (i, 0)),
                  pl.BlockSpec((1, N_MOD, D_MODEL), lambda i: (i // group_tiles, 0, 0)),
                  yspec, yspec, yspec, yspec,
                  pl.BlockSpec((tm, N_BRANCH * D_MODEL), lambda i: (i, 0)),
                  _resident(wp.shape), _resident(wo.shape), _resident((1, D_MODEL)),
                  _resident((1, D_MODEL)), _resident((1, D_MODEL))],
        out_specs=pl.BlockSpec((tm, D_MODEL), lambda i: (i, 0)),
        out_shape=jax.ShapeDtypeStruct((n, D_MODEL), F32),
        compiler_params=_cparams(1))(x, mod, ya, yb, yc, yd, p, wp, wo, bo, g, b)


def kernel(x, c, ctx, c_ctx, w_mod, b_mod, post_ln_g, post_ln_b, ffn_w_in, ffn_w_out, w_in, b_in,
           conf_dw_w, conf_dw_b, conf_ln_g, conf_ln_b, conf_w_proj, na_rpb, na_w_proj, sc_conv_w,
           sc_w_proj, hy_sconv_w, hy_sconv_b, hy_w1, hy_b1, hy_w2, hy_b2, hy_freq, hy_w3, hy_bias,
           hy_w_proj, w_out, b_out):
    bsz, n_lat, _ = x.shape
    ctx_len = ctx.shape[1]
    depth = w_mod.shape[0]
    alpha = (2 * depth) ** 0.25
    rows = n_lat // GRID_W

    tm = 512
    tmc = min(tm, ctx_len)
    tp = 256
    tpc = min(tp, ctx_len)

    xl = x.reshape(bsz * n_lat, D_MODEL)
    xc = ctx.reshape(bsz * ctx_len, D_MODEL)
    lat_tiles = n_lat // tm
    ctx_tiles = bsz * ctx_len // tmc

    tables = _dft_tables(2 * n_lat // FFT_N2)
    tables_c = _dft_tables_direct(ctx_len)

    cond = jnp.concatenate([c, c_ctx[None], jnp.zeros((8 - bsz - 1, D_MODEL), F32)], axis=0)

    for l in range(depth):
        last = l == depth - 1
        mod_all = _mm(cond, w_mod[l], bn=1024, out_dtype=F32, bias=b_mod[l][None], a_silu=True)
        mod_all = mod_all.reshape(8, N_MOD, D_MODEL)
        mod = mod_all[:bsz]
        mod_c = mod_all[bsz:bsz + 1]

        wl = w_in[l]
        w_perm = jnp.concatenate([wl[:, OFF_G:], wl[:, OFF_A:OFF_Q], wl[:, OFF_SB:OFF_G],
                                  wl[:, OFF_Q:OFF_SB]], axis=1).astype(BF)
        bl = b_in[l]
        b_perm = jnp.concatenate([bl[OFF_G:], bl[OFF_A:OFF_Q], bl[OFF_SB:OFF_G], bl[OFF_Q:OFF_SB]])[None]
        ffn_wi = ffn_w_in[l].astype(BF)
        ffn_wo = ffn_w_out[l].astype(BF)
        ln_g = post_ln_g[l][:, None, :]
        ln_b = post_ln_b[l][:, None, :]
        wp = jnp.stack([conf_w_proj[l], na_w_proj[l], sc_w_proj[l], hy_w_proj[l]]).astype(BF)
        wo = w_out[l].astype(BF)
        conv_args = (conf_dw_w[l], conf_dw_b[l][None], conf_ln_g[l][None], conf_ln_b[l][None],
                     sc_conv_w[l], hy_sconv_w[l], hy_sconv_b[l][None])
        filt_args = (hy_w1[l], hy_b1[l], hy_w2[l], hy_b2[l], hy_freq[l], hy_w3[l])

        xl = _ffn(xl, mod, ffn_wi[0], ffn_wo[0], ln_g[0], ln_b[0], m0=0, tm=tm,
                  group_tiles=lat_tiles, alpha=alpha)
        xc = _ffn(xc, mod_c, ffn_wi[0], ffn_wo[0], ln_g[0], ln_b[0], m0=0, tm=tmc,
                  group_tiles=ctx_tiles, alpha=alpha)

        p = _inproj(xl, mod, w_perm, b_perm, tm=tm, group_tiles=lat_tiles)
        pc = _inproj(xc, mod_c, w_perm, b_perm, tm=tmc, group_tiles=ctx_tiles)

        ya, yc, hv, h1, h2 = _prep(p, *conv_args, tm=tp, seq_len=n_lat)
        attn = _na(p, pc, _na_bias_table(na_rpb[l], rows), bsz=bsz, seq_len=n_lat)
        kern, ss = _hyena_filters(n_lat, *filt_args)
        yd = _hyena_long(hv, (h1, h2), kern, ss, hy_bias[l], tables, bsz=bsz, seq_len=n_lat)
        xl = _merge(xl, mod, ya, attn, yc, yd, p, wp, wo, b_out[l][None], ln_g[1], ln_b[1],
                    tm=tm, group_tiles=lat_tiles, alpha=alpha)

        if not last:
            ya, yc, hv, h1, h2 = _prep(pc, *conv_args, tm=tpc, seq_len=ctx_len)
            attn_c = _ctx_attn(pc, bsz=bsz, ctx_len=ctx_len)
            kern_c, ss_c = _hyena_filters(ctx_len, *filt_args)
            yd = _hyena_short_seq(hv, (h1, h2), kern_c, ss_c, hy_bias[l], tables_c, bsz=bsz)
            xc = _merge(xc, mod_c, ya, attn_c, yc, yd, pc, wp, wo, b_out[l][None], ln_g[1], ln_b[1],
                        tm=tmc, group_tiles=ctx_tiles, alpha=alpha)

        xl = _ffn(xl, mod, ffn_wi[1], ffn_wo[1], ln_g[2], ln_b[2], m0=6, tm=tm,
                  group_tiles=lat_tiles, alpha=alpha)
        if not last:
            xc = _ffn(xc, mod_c, ffn_wi[1], ffn_wo[1], ln_g[2], ln_b[2], m0=6, tm=tmc,
                      group_tiles=ctx_tiles, alpha=alpha)

    return xl.reshape(bsz, n_lat, D_MODEL)
```

```python
import functools
import math

import jax
import jax.numpy as jnp
import numpy as np
from jax import lax
from jax.experimental import pallas as pl
from jax.experimental.pallas import tpu as pltpu

D_MODEL = 1024
GRID_W = 64
N_BRANCH = 4
WA = 256
CONF_K = 31
NA_HEADS = 4
NA_HEAD_DIM = 64
WB = NA_HEADS * NA_HEAD_DIM
NA_WIN_ROWS = 8
NA_WIN_COLS = 16
ATTN_SCALE = NA_HEAD_DIM ** -0.5
WC = 256
SC_K = 3
WD = 256
HY_ORDER = 2
HY_SHORT_K = 3
HY_PE_BANDS = 16
HY_PE_DIM = 1 + 2 * HY_PE_BANDS
HY_FILT_HID = 64
HY_FAST_DECAY = 0.3
HY_SLOW_DECAY = 1.5
HY_DECAY_TARGET = 1e-2
D_FF = 2816
N_MOD = 9
LN_EPS = 1e-5

OFF_A = 0
OFF_Q = OFF_A + 2 * WA
OFF_K = OFF_Q + WB
OFF_V = OFF_K + WB
OFF_SB = OFF_V + WB
OFF_SC = OFF_SB + WC
OFF_SX = OFF_SC + WC
OFF_HV = OFF_SX + WC
OFF_G = OFF_HV + (1 + HY_ORDER) * WD
P_IN = OFF_G + N_BRANCH * D_MODEL

NEW_G = 0
NEW_CONV = N_BRANCH * D_MODEL
CONV_W = 2 * WA + 3 * WC + 3 * WD
NEW_Q = NEW_CONV + CONV_W

BF = jnp.bfloat16
F32 = jnp.float32

VMEM_LIMIT_BYTES = 58 * 1024 * 1024
FFN_CHUNK = 256
INPROJ_CHUNK = 1152
HALO = 16
SHIFT_SPAN = 24
NA_QROWS = 4
NA_TOK = NA_QROWS * GRID_W
FFT_N2 = 128
NEG_INF = -1e30


def _cparams(n_axes, semantics="parallel"):
    return pltpu.CompilerParams(dimension_semantics=(semantics,) * n_axes,
                                vmem_limit_bytes=VMEM_LIMIT_BYTES)


def _resident(shape):
    nd = len(shape)
    return pl.BlockSpec(shape, lambda *_: (0,) * nd, pipeline_mode=pl.Buffered(1))


def _ln(x):
    mu = jnp.mean(x, axis=-1, keepdims=True)
    xc = x - mu
    var = jnp.mean(xc * xc, axis=-1, keepdims=True)
    return xc * lax.rsqrt(var + LN_EPS)


def _sigmoid(x):
    return 0.5 * jnp.tanh(0.5 * x) + 0.5


def _dot(a, b):
    return jnp.dot(a, b, preferred_element_type=F32)


def _dot_nt(a, b):
    return lax.dot_general(a, b, (((1,), (1,)), ((), ())), preferred_element_type=F32)


def _mm_kernel(*refs, a_silu, has_bias, has_epi):
    a_ref, b_ref = refs[0], refs[1]
    pos = 2
    a = a_ref[...]
    if a_silu:
        a = a.astype(F32)
        a = a * _sigmoid(a)
    acc = _dot(a.astype(BF), b_ref[...].astype(BF))
    if has_bias:
        acc = acc + refs[pos][...]
        pos += 1
    if has_epi:
        gate_ref, u_ref, row_ref = refs[pos], refs[pos + 1], refs[pos + 2]
        pos += 3
        acc = gate_ref[...].astype(F32) * (acc + u_ref[...].astype(F32) * row_ref[...])
    o_ref = refs[pos]
    o_ref[...] = acc.astype(o_ref.dtype)


def _mm(a, b, *, bn, out_dtype, name, bias=None, epi=None, a_silu=False):
    m, k = a.shape
    n = b.shape[1]
    bn = min(bn, n)
    ops = [a, b]
    specs = [_resident((m, k)), pl.BlockSpec((k, bn), lambda j: (0, j))]
    if bias is not None:
        ops.append(bias)
        specs.append(pl.BlockSpec((1, bn), lambda j: (0, j)))
    if epi is not None:
        gate, u, row = epi
        ops += [gate, u, row]
        specs += [pl.BlockSpec((m, bn), lambda j: (0, j)),
                  pl.BlockSpec((m, bn), lambda j: (0, j)),
                  pl.BlockSpec((1, bn), lambda j: (0, j))]
    kern = functools.partial(_mm_kernel, a_silu=a_silu, has_bias=bias is not None,
                             has_epi=epi is not None)
    return pl.pallas_call(
        kern, grid=(n // bn,), in_specs=specs,
        out_specs=pl.BlockSpec((m, bn), lambda j: (0, j)),
        out_shape=jax.ShapeDtypeStruct((m, n), out_dtype),
        compiler_params=_cparams(1), name=name)(*ops)


def _modulated(x, mod, m0):
    shift = mod[m0:m0 + 1]
    scale = mod[m0 + 1:m0 + 2]
    return _ln(x) * (1.0 + scale) + shift


def _ffn_kernel(x_ref, mod_ref, wi_ref, wo_ref, g_ref, b_ref, o_ref, u_ref, *, m0, alpha):
    x = x_ref[...]
    mod = mod_ref[0]
    h = _modulated(x, mod, m0).astype(BF)
    for c in range(D_FF // FFN_CHUNK):
        ag = _dot(h, wi_ref[:, 2 * c * FFN_CHUNK:2 * (c + 1) * FFN_CHUNK])
        a, g = ag[:, :FFN_CHUNK], ag[:, FFN_CHUNK:]
        u_ref[:, c * FFN_CHUNK:(c + 1) * FFN_CHUNK] = (g * _sigmoid(g) * a).astype(BF)
    acc = _dot(u_ref[...], wo_ref[...])
    y = alpha * x + (0.5 * mod[m0 + 2:m0 + 3]) * acc
    o_ref[...] = _ln(y) * g_ref[...] + b_ref[...]


def _ffn(x, mod, w_in, w_out, g, b, *, m0, tm, group_tiles, alpha):
    n = x.shape[0]
    kern = functools.partial(_ffn_kernel, m0=m0, alpha=alpha)
    return pl.pallas_call(
        kern, grid=(n // tm,),
        in_specs=[pl.BlockSpec((tm, D_MODEL), lambda i: (i, 0)),
                  pl.BlockSpec((1, N_MOD, D_MODEL), lambda i: (i // group_tiles, 0, 0)),
                  _resident(w_in.shape), _resident(w_out.shape),
                  _resident((1, D_MODEL)), _resident((1, D_MODEL))],
        out_specs=pl.BlockSpec((tm, D_MODEL), lambda i: (i, 0)),
        out_shape=jax.ShapeDtypeStruct((n, D_MODEL), F32),
        scratch_shapes=[pltpu.VMEM((tm, D_FF), BF)],
        compiler_params=_cparams(1), name="ffn")(x, mod, w_in, w_out, g, b)


def _inproj_kernel(x_ref, mod_ref, w_ref, b_ref, o_ref, *, m0):
    h = _modulated(x_ref[...], mod_ref[0], m0).astype(BF)
    for c in range(P_IN // INPROJ_CHUNK):
        lo = c * INPROJ_CHUNK
        o_ref[:, lo:lo + INPROJ_CHUNK] = (
            _dot(h, w_ref[:, lo:lo + INPROJ_CHUNK]) + b_ref[:, lo:lo + INPROJ_CHUNK]).astype(BF)


def _inproj(x, mod, w, b, *, tm, group_tiles):
    n = x.shape[0]
    return pl.pallas_call(
        functools.partial(_inproj_kernel, m0=3), grid=(n // tm,),
        in_specs=[pl.BlockSpec((tm, D_MODEL), lambda i: (i, 0)),
                  pl.BlockSpec((1, N_MOD, D_MODEL), lambda i: (i // group_tiles, 0, 0)),
                  _resident(w.shape), _resident((1, P_IN))],
        out_specs=pl.BlockSpec((tm, P_IN), lambda i: (i, 0)),
        out_shape=jax.ShapeDtypeStruct((n, P_IN), BF),
        compiler_params=_cparams(1), name="inproj")(x, mod, w, b)


def _prep_kernel(x_ref, prev_ref, next_ref, cw_ref, cb_ref, lg_ref, lb_ref, sw_ref, hw_ref, hb_ref,
                 ya_ref, yc_ref, hv_ref, h1_ref, h2_ref, buf_a, buf_c, buf_h, shifted, *, tm, seq_tiles):
    i = pl.program_id(0)
    pos = i % seq_tiles
    keep_prev = jnp.where(pos == 0, 0.0, 1.0)
    keep_next = jnp.where(pos == seq_tiles - 1, 0.0, 1.0)

    def fill(t, lo, hi):
        glu = t[:, 0:WA] * _sigmoid(t[:, WA:2 * WA])
        buf_a[lo:hi, :] = glu
        buf_c[lo:hi, :] = t[:, 3 * WA:4 * WA] * t[:, 4 * WA:5 * WA]
        buf_h[lo:hi, :] = t[:, 5 * WA:8 * WA]

    main = x_ref[...].astype(F32)
    fill(prev_ref[...].astype(F32) * keep_prev, 0, HALO)
    fill(main, HALO, HALO + tm)
    fill(next_ref[...].astype(F32) * keep_next, HALO + tm, 2 * HALO + tm)

    span = tm + SHIFT_SPAN
    for r in range(8):
        shifted[r] = buf_a[r:r + span, :]
    acc = jnp.zeros((tm, WA), F32) + cb_ref[...]
    for j in range(CONF_K):
        off = HALO - CONF_K // 2 + j
        base = off - off % 8
        acc = acc + cw_ref[j:j + 1, :] * shifted[off % 8, base:base + tm, :]
    u = _ln(acc) * lg_ref[...] + lb_ref[...]
    ya_ref[...] = (u * _sigmoid(u)).astype(BF)

    acc = jnp.zeros((tm, WC), F32)
    for j in range(SC_K):
        off = HALO - SC_K // 2 + j
        acc = acc + sw_ref[j:j + 1, :] * buf_c[off:off + tm, :]
    yc_ref[...] = (main[:, 2 * WA:3 * WA] * acc).astype(BF)

    acc = jnp.zeros((tm, 3 * WD), F32) + hb_ref[...]
    for j in range(HY_SHORT_K):
        off = HALO - HY_SHORT_K // 2 + j
        acc = acc + hw_ref[j:j + 1, :] * buf_h[off:off + tm, :]
    hv_ref[...] = acc[:, 0:WD].astype(BF)
    h1_ref[...] = acc[:, WD:2 * WD].astype(BF)
    h2_ref[...] = acc[:, 2 * WD:3 * WD].astype(BF)


def _prep(p, cw, cb, lg, lb, sw, hw, hb, *, tm, seq_len):
    n = p.shape[0]
    seq_tiles = seq_len // tm
    hb_per_tile = tm // HALO
    n_halo_blocks = n // HALO
    cblk = NEW_CONV // CONV_W
    kern = functools.partial(_prep_kernel, tm=tm, seq_tiles=seq_tiles)
    out_bf = jax.ShapeDtypeStruct((n, WD), BF)
    ospec = pl.BlockSpec((tm, WD), lambda i: (i, 0))
    return pl.pallas_call(
        kern, grid=(n // tm,),
        in_specs=[pl.BlockSpec((tm, CONV_W), lambda i: (i, cblk)),
                  pl.BlockSpec((HALO, CONV_W),
                               lambda i: (jnp.maximum(i * hb_per_tile - 1, 0), cblk)),
                  pl.BlockSpec((HALO, CONV_W),
                               lambda i: (jnp.minimum((i + 1) * hb_per_tile, n_halo_blocks - 1), cblk)),
                  _resident(cw.shape), _resident(cb.shape), _resident(lg.shape), _resident(lb.shape),
                  _resident(sw.shape), _resident(hw.shape), _resident(hb.shape)],
        out_specs=[ospec] * 5, out_shape=[out_bf] * 5,
        scratch_shapes=[pltpu.VMEM((tm + 2 * HALO, WA), F32),
                        pltpu.VMEM((tm + 2 * HALO, WC), F32),
                        pltpu.VMEM((tm + 2 * HALO, 3 * WD), F32),
                        pltpu.VMEM((8, tm + SHIFT_SPAN, WA), F32)],
        compiler_params=_cparams(1), name="prep")(p, p, p, cw, cb, lg, lb, sw, hw, hb)


def _na_kernel(q_ref, k0_ref, k1_ref, k2_ref, v0_ref, v1_ref, v2_ref, kc_ref, vc_ref, bias_ref, o_ref):
    q = q_ref[...] * ATTN_SCALE
    k = jnp.concatenate([k0_ref[...], k1_ref[...], k2_ref[...]], axis=0)
    v = jnp.concatenate([v0_ref[...], v1_ref[...], v2_ref[...]], axis=0)
    kc = kc_ref[...]
    vc = vc_ref[...]
    outs = []
    for h in range(NA_HEADS):
        sl = slice(h * NA_HEAD_DIM, (h + 1) * NA_HEAD_DIM)
        qh = q[:, sl]
        s_nb = _dot_nt(qh, k[:, sl]) + bias_ref[0, h]
        s_cx = _dot_nt(qh, kc[:, sl])
        m = jnp.maximum(jnp.max(s_nb, axis=-1, keepdims=True), jnp.max(s_cx, axis=-1, keepdims=True))
        p_nb = jnp.exp(s_nb - m)
        p_cx = jnp.exp(s_cx - m)
        den = jnp.sum(p_nb, axis=-1, keepdims=True) + jnp.sum(p_cx, axis=-1, keepdims=True)
        o = _dot(p_nb.astype(BF), v[:, sl]) + _dot(p_cx.astype(BF), vc[:, sl])
        outs.append(o / den)
    o_ref[...] = jnp.concatenate(outs, axis=-1).astype(BF)


def _rpb_expand_kernel(rpb_ref, onehot_ref, o_ref):
    o_ref[...] = jnp.dot(rpb_ref[...], onehot_ref[...], precision=lax.Precision.HIGHEST,
                         preferred_element_type=F32)


def _na_bias_table(rpb, rows):
    assert rows >= NA_WIN_ROWS and rows % NA_QROWS == 0 and rows // NA_QROWS >= 3
    n_dr, n_dc = 2 * NA_WIN_ROWS - 1, 2 * NA_WIN_COLS - 1
    cols = np.arange(GRID_W)
    c0 = np.clip(cols - NA_WIN_COLS // 2, 0, GRID_W - NA_WIN_COLS)
    dc = cols[None, :] - cols[:, None] + (NA_WIN_COLS - 1)
    ok_c = (cols[None, :] >= c0[:, None]) & (cols[None, :] < c0[:, None] + NA_WIN_COLS)
    onehot = np.zeros((128, GRID_W, GRID_W), np.float32)
    qq, kk = np.nonzero(ok_c)
    onehot[dc[qq, kk], qq, kk] = 1.0
    rpb2 = jnp.pad(rpb.reshape(NA_HEADS * n_dr, n_dc).astype(F32),
                   ((0, 64 - NA_HEADS * n_dr), (0, 128 - n_dc)))
    t = pl.pallas_call(
        _rpb_expand_kernel, grid=(1,),
        in_specs=[_resident((64, 128)), _resident((128, GRID_W * GRID_W))],
        out_specs=pl.BlockSpec((64, GRID_W * GRID_W), lambda i: (0, 0)),
        out_shape=jax.ShapeDtypeStruct((64, GRID_W * GRID_W), F32),
        compiler_params=_cparams(1), name="rpb_expand")(rpb2, jnp.asarray(onehot.reshape(128, GRID_W * GRID_W)))
    t = t[:NA_HEADS * n_dr].reshape(NA_HEADS, n_dr, GRID_W, GRID_W)
    full = jnp.concatenate(
        [jnp.concatenate([t[:, s - a + NA_WIN_ROWS - 1 - NA_QROWS] for s in range(3 * NA_QROWS)], axis=-1)
         for a in range(NA_QROWS)], axis=1)
    wr = NA_WIN_ROWS
    n_blk = rows // NA_QROWS
    tabs = []
    for blk in (0, 1, n_blk - 1):
        qr = blk * NA_QROWS + np.arange(NA_QROWS)
        kr = (blk - 1) * NA_QROWS + np.arange(3 * NA_QROWS)
        r0 = np.clip(qr - wr // 2, 0, rows - wr)
        ok_r = ((kr[None, :] >= r0[:, None]) & (kr[None, :] < r0[:, None] + wr)
                & (kr[None, :] >= 0) & (kr[None, :] < rows))
        ok = (ok_r[:, None, :, None] & ok_c[None, :, None, :]).reshape(NA_TOK, 3 * NA_TOK)
        tabs.append(jnp.where(jnp.asarray(ok)[None], full, NEG_INF))
    return jnp.stack(tabs)


def _na(p, pc, bias_tab, *, bsz, seq_len):
    t = seq_len // NA_TOK
    qb, kb, vb = NEW_Q // WB, NEW_Q // WB + 1, NEW_Q // WB + 2
    blk = (NA_TOK, WB)
    cblk = (pc.shape[0] // bsz, WB)

    def nbr(j, col):
        return pl.BlockSpec(blk, lambda b, i: (b * t + jnp.clip(i - 1 + j, 0, t - 1), col))

    return pl.pallas_call(
        _na_kernel, grid=(bsz, t),
        in_specs=[pl.BlockSpec(blk, lambda b, i: (b * t + i, qb)),
                  nbr(0, kb), nbr(1, kb), nbr(2, kb), nbr(0, vb), nbr(1, vb), nbr(2, vb),
                  pl.BlockSpec(cblk, lambda b, i: (b, kb)),
                  pl.BlockSpec(cblk, lambda b, i: (b, vb)),
                  pl.BlockSpec((1, NA_HEADS, NA_TOK, 3 * NA_TOK),
                               lambda b, i: (jnp.where(i == 0, 0, jnp.where(i == t - 1, 2, 1)), 0, 0, 0))],
        out_specs=pl.BlockSpec(blk, lambda b, i: (b * t + i, 0)),
        out_shape=jax.ShapeDtypeStruct((bsz * seq_len, WB), BF),
        compiler_params=_cparams(2), name="na")(p, p, p, p, p, p, p, pc, pc, bias_tab)


def _ctx_attn_kernel(q_ref, k_ref, v_ref, o_ref):
    q = q_ref[...] * ATTN_SCALE
    k = k_ref[...]
    v = v_ref[...]
    outs = []
    for h in range(NA_HEADS):
        sl = slice(h * NA_HEAD_DIM, (h + 1) * NA_HEAD_DIM)
        s = _dot_nt(q[:, sl], k[:, sl])
        m = jnp.max(s, axis=-1, keepdims=True)
        p = jnp.exp(s - m)
        den = jnp.sum(p, axis=-1, keepdims=True)
        outs.append(_dot(p.astype(BF), v[:, sl]) / den)
    o_ref[...] = jnp.concatenate(outs, axis=-1).astype(BF)


def _ctx_attn(pc, *, bsz, ctx_len):
    qb, kb, vb = NEW_Q // WB, NEW_Q // WB + 1, NEW_Q // WB + 2
    blk = (ctx_len, WB)
    return pl.pallas_call(
        _ctx_attn_kernel, grid=(bsz,),
        in_specs=[pl.BlockSpec(blk, lambda b: (b, qb)), pl.BlockSpec(blk, lambda b: (b, kb)),
                  pl.BlockSpec(blk, lambda b: (b, vb))],
        out_specs=pl.BlockSpec(blk, lambda b: (b, 0)),
        out_shape=jax.ShapeDtypeStruct((bsz * ctx_len, WB), BF),
        compiler_params=_cparams(1), name="ctx_attn")(pc, pc, pc)


def _filt_kernel(zf_ref, zb_ref, w1_ref, b1_ref, w2_ref, b2_ref, fr_ref, w3_ref, dl_ref, k_ref, ss_ref,
                 *, tm):
    i = pl.program_id(0)
    hp = lax.Precision.HIGHEST
    zf = zf_ref[...]
    zb = zb_ref[...]
    w1 = w1_ref[...]
    pre = jnp.concatenate([jnp.dot(zf, w1, precision=hp, preferred_element_type=F32),
                           jnp.dot(zb, w1, precision=hp, preferred_element_type=F32)], axis=-1)
    h = jnp.sin(fr_ref[0:1, :] * (pre + b1_ref[...]))
    h = jnp.sin(fr_ref[1:2, :] * (jnp.dot(h, w2_ref[...], precision=hp, preferred_element_type=F32)
                                  + b2_ref[...]))
    k = jnp.dot(h, w3_ref[...], precision=hp, preferred_element_type=F32)
    dec_f = jnp.exp(-zf[:, 0:1] * dl_ref[...])
    dec_b = jnp.exp(-zb[:, 0:1] * dl_ref[...])
    k = k * jnp.concatenate([dec_f] * HY_ORDER + [dec_b] * HY_ORDER, axis=-1)
    row = i * tm + lax.broadcasted_iota(jnp.int32, k.shape, 0)
    col = lax.broadcasted_iota(jnp.int32, k.shape, 1)
    k = jnp.where((col >= HY_ORDER * WD) & (row == 0), 0.0, k)
    k_ref[0] = k[:, :HY_ORDER * WD].astype(BF)
    k_ref[1] = k[:, HY_ORDER * WD:].astype(BF)

    @pl.when(i == 0)
    def _():
        ss_ref[...] = jnp.zeros_like(ss_ref)

    sq = jnp.sum(k * k, axis=0, keepdims=True)
    ss_ref[...] += sq[:, :HY_ORDER * WD] + sq[:, HY_ORDER * WD:]


def _pos_features(t, length):
    t_norm = t / max(length - 1, 1)
    bands = jnp.linspace(1e-4, HY_PE_BANDS - 1, HY_PE_BANDS, dtype=F32)
    ang = (2.0 * math.pi / length) * t[:, None] * bands[None, :]
    z = jnp.concatenate([t_norm[:, None], jnp.cos(ang), -jnp.sin(ang)], axis=-1)
    return jnp.pad(z, ((0, 0), (0, 128 - HY_PE_DIM)))


def _block_diag(a, b):
    return jnp.concatenate([jnp.concatenate([a, jnp.zeros((a.shape[0], b.shape[1]), a.dtype)], axis=1),
                            jnp.concatenate([jnp.zeros((b.shape[0], a.shape[1]), a.dtype), b], axis=1)],
                           axis=0)


def _hyena_filters(length, w1, b1, w2, b2, freq, w3):
    t = jnp.arange(length, dtype=F32)
    zf = _pos_features(t, length)
    zb = _pos_features(length - t, length)
    w1p = jnp.pad(w1, ((0, 128 - HY_PE_DIM), (0, 0)))
    w3r = w3.reshape(HY_FILT_HID, HY_ORDER, 2, WD)
    w3bd = _block_diag(w3r[:, :, 0].reshape(HY_FILT_HID, HY_ORDER * WD),
                       w3r[:, :, 1].reshape(HY_FILT_HID, HY_ORDER * WD))
    w2bd = _block_diag(w2, w2)
    b1d = jnp.concatenate([b1, b1])[None]
    b2d = jnp.concatenate([b2, b2])[None]
    frd = jnp.concatenate([freq, freq], axis=1)
    deltas = jnp.abs(jnp.linspace(math.log(HY_DECAY_TARGET) / HY_SLOW_DECAY,
                                  math.log(HY_DECAY_TARGET) / HY_FAST_DECAY, WD, dtype=F32))[None]
    tm = min(512, length)
    nf = HY_ORDER * WD
    zspec = pl.BlockSpec((tm, 128), lambda i: (i, 0))
    k, ss = pl.pallas_call(
        functools.partial(_filt_kernel, tm=tm), grid=(length // tm,),
        in_specs=[zspec, zspec, _resident(w1p.shape), _resident(b1d.shape), _resident(w2bd.shape),
                  _resident(b2d.shape), _resident(frd.shape), _resident(w3bd.shape),
                  _resident(deltas.shape)],
        out_specs=[pl.BlockSpec((2, tm, nf), lambda i: (0, i, 0)), pl.BlockSpec((1, nf), lambda i: (0, 0))],
        out_shape=[jax.ShapeDtypeStruct((2, length, nf), BF), jax.ShapeDtypeStruct((1, nf), F32)],
        compiler_params=_cparams(1, "arbitrary"), name="hyena_filter")(zf, zb, w1p, b1d, w2bd, b2d, frd, w3bd, deltas)
    return k.reshape(2 * length, nf), ss


def _cis(num, den):
    ang = (-2.0 * math.pi / den) * (num % den).astype(F32)
    return jnp.cos(ang), jnp.sin(ang)


def _stack(re, im):
    return jnp.concatenate([jnp.concatenate([re, -im], axis=1), jnp.concatenate([im, re], axis=1)], axis=0)


def _dft_tables(n1):
    n = n1 * FFT_N2
    i1 = jnp.arange(n1)
    fr, fi = _cis(i1[:, None] * i1[None, :], n1)
    half = n1 // 2
    w_fwd = _stack(fr[:, :half], fi[:, :half]).astype(BF)
    w_real = jnp.concatenate([fr, fi], axis=0).astype(BF)
    w_inv = (_stack(fr[:half, :], -fi[:half, :]) / n).astype(BF)
    i2 = jnp.arange(FFT_N2)
    num = i2[None, None, :] * (n1 * i2[None, :, None] + i1[:, None, None])
    gr, gi = _cis(num, n)
    gs = jnp.concatenate([jnp.concatenate([gr, -gi], axis=2), jnp.concatenate([gi, gr], axis=2)], axis=1)
    gs = gs.astype(BF)
    return w_fwd, w_real, w_inv, gs, jnp.swapaxes(gs, 1, 2)


def _dft_tables_direct(length):
    n = 2 * length
    i = jnp.arange(n)
    fr, fi = _cis(i[:, None] * i[None, :], n)
    w_fwd = _stack(fr[:, :length], fi[:, :length]).astype(BF)
    w_real = jnp.concatenate([fr, fi], axis=0).astype(BF)
    w_inv = (_stack(fr[:length, :], -fi[:length, :]) / n).astype(BF)
    return w_fwd, w_real, w_inv


def _stage_a_kernel(*refs, nb, n_half, has_epi):
    w_ref, x_ref = refs[0], refs[1]
    m, k = w_ref.shape
    if has_epi:
        g_ref, u_ref, row_ref, o_ref = refs[2:6]
        scratch = refs[6:]
    else:
        o_ref = refs[2]
        scratch = refs[3:]
    xs, os_ = scratch[:n_half], scratch[n_half:]
    w = w_ref[...]
    for l in range(n_half):
        lanes = slice(l * 128, (l + 1) * 128)
        xs[l][...] = x_ref[:, :, lanes].astype(F32).reshape(k * nb, 128)
        for j in range(nb):
            os_[l][pl.ds(j, m, stride=nb), :] = _dot(w, xs[l][pl.ds(j, k, stride=nb), :].astype(BF))
        acc = os_[l][...].reshape(m, nb, 128)
        if has_epi:
            acc = g_ref[:, :, lanes].astype(F32) * (
                acc + u_ref[:, :, lanes].astype(F32) * row_ref[:, lanes].reshape(1, 1, 128))
        o_ref[:, :, lanes] = acc.astype(o_ref.dtype)


def _stage_a(w, x, *, name, epi=None, nb=16, lane_blk=256):
    m, k = w.shape
    c = x.shape[2]
    n_half = lane_blk // 128
    xblk = pl.BlockSpec((k, nb, lane_blk), lambda j, l: (0, j, l))
    oblk = pl.BlockSpec((m, nb, lane_blk), lambda j, l: (0, j, l))
    ops, specs = [w, x], [_resident((m, k)), xblk]
    if epi is not None:
        ops += list(epi)
        specs += [oblk, oblk, pl.BlockSpec((1, lane_blk), lambda j, l: (0, l))]
    scratch = ([pltpu.VMEM((k * nb, 128), F32)] * n_half + [pltpu.VMEM((m * nb, 128), F32)] * n_half)
    return pl.pallas_call(
        functools.partial(_stage_a_kernel, nb=nb, n_half=n_half, has_epi=epi is not None),
        grid=(FFT_N2 // nb, c // lane_blk), in_specs=specs, out_specs=oblk, scratch_shapes=scratch,
        out_shape=jax.ShapeDtypeStruct((m, FFT_N2, c), BF), compiler_params=_cparams(2), name=name)(*ops)


def _stage_b_spec_kernel(a_ref, gs_ref, ss_ref, h_ref, *, kb):
    scale = lax.rsqrt(ss_ref[...] + 1e-6)
    for j in range(kb):
        a = jnp.concatenate([a_ref[0, j], a_ref[1, j]], axis=0)
        y = _dot(gs_ref[j], a) * scale
        h_ref[0, j] = y[:FFT_N2].astype(BF)
        h_ref[1, j] = y[FFT_N2:].astype(BF)


def _stage_b_conv_kernel(a_ref, gs_ref, gst_ref, h_ref, o_ref, *, kb):
    for j in range(kb):
        a = jnp.concatenate([a_ref[0, j], a_ref[1, j]], axis=0)
        y = _dot(gs_ref[j], a)
        yr, yi = y[:FFT_N2], y[FFT_N2:]
        hr, hi = h_ref[0, j].astype(F32), h_ref[1, j].astype(F32)
        z = jnp.concatenate([yr * hr - yi * hi, yr * hi + yi * hr], axis=0).astype(BF)
        b = _dot(gst_ref[j], z)
        o_ref[0, j] = b[:FFT_N2].astype(BF)
        o_ref[1, j] = b[FFT_N2:].astype(BF)


def _stage_b(a, gs, *, lane_blk=0, gst=None, spec=None, ss=None, kb=8):
    n1 = a.shape[1]
    kb = min(kb, n1)
    ablk = pl.BlockSpec((2, kb, FFT_N2, WD), lambda i: (0, i, 0, lane_blk))
    dblk = pl.BlockSpec((2, kb, FFT_N2, WD), lambda i: (0, i, 0, 0))
    gblk = pl.BlockSpec((kb, 2 * FFT_N2, 2 * FFT_N2), lambda i: (i, 0, 0))
    out = jax.ShapeDtypeStruct((2, n1, FFT_N2, WD), BF)
    if spec is None:
        return pl.pallas_call(
            functools.partial(_stage_b_spec_kernel, kb=kb), grid=(n1 // kb,),
            in_specs=[ablk, gblk, pl.BlockSpec((1, WD), lambda i: (0, lane_blk))], out_specs=dblk,
            out_shape=out, compiler_params=_cparams(1), name="dft_b_spectrum")(a, gs, ss)
    return pl.pallas_call(
        functools.partial(_stage_b_conv_kernel, kb=kb), grid=(n1 // kb,),
        in_specs=[ablk, gblk, gblk, dblk], out_specs=dblk,
        out_shape=out, compiler_params=_cparams(1), name="dft_b_conv")(a, gs, gst, spec)


def _cmul_kernel(x_ref, h_ref, ss_ref, o_ref, *, half):
    scale = lax.rsqrt(ss_ref[...] + 1e-6)
    xr, xi = x_ref[:half, :], x_ref[half:, :]
    hr, hi = h_ref[:half, :] * scale, h_ref[half:, :] * scale
    o_ref[:half, :] = (xr * hr - xi * hi).astype(BF)
    o_ref[half:, :] = (xr * hi + xi * hr).astype(BF)


def _cmul(x, h, ss, lane_blk):
    rows = x.shape[0]
    return pl.pallas_call(
        functools.partial(_cmul_kernel, half=rows // 2), grid=(1,),
        in_specs=[_resident(x.shape), pl.BlockSpec((rows, WD), lambda i: (0, lane_blk)),
                  pl.BlockSpec((1, WD), lambda i: (0, lane_blk))],
        out_specs=pl.BlockSpec(x.shape, lambda i: (0, 0)),
        out_shape=jax.ShapeDtypeStruct(x.shape, BF), compiler_params=_cparams(1), name="ctx_cmul")(x, h, ss)


def _hyena_long(hv, hx, kern, ss, bias, tables, *, bsz, seq_len):
    assert bsz == 2
    n1 = 2 * seq_len // FFT_N2
    w_fwd, w_real, w_inv, gs, gst = tables
    view = (bsz * seq_len // FFT_N2, FFT_N2, WD)
    filt_a = _stage_a(w_real, kern.reshape(n1, FFT_N2, HY_ORDER * WD), name="dft_a_filter")
    filt_a = filt_a.reshape(2, n1, FFT_N2, HY_ORDER * WD)
    u = hv.reshape(view)
    for o in range(HY_ORDER):
        spec = _stage_b(filt_a, gs, lane_blk=o, ss=ss)
        a = _stage_a(w_fwd, u, name="dft_a_fwd").reshape(2, n1, FFT_N2, WD)
        b = _stage_b(a, gs, gst=gst, spec=spec).reshape(2 * n1, FFT_N2, WD)
        u = _stage_a(w_inv, b, name="dft_a_inv", epi=(hx[o].reshape(view), u, bias[o][None]))
    return u.reshape(bsz * seq_len, WD)


def _hyena_short_seq(hv, hx, kern, ss, bias, tables, *, bsz):
    assert bsz == 2
    w_fwd, w_real, w_inv = tables
    spec = _mm(w_real, kern, bn=HY_ORDER * WD, out_dtype=F32, name="ctx_dft_filter")
    u = hv
    for o in range(HY_ORDER):
        x = _mm(w_fwd, u, bn=WD, out_dtype=F32, name="ctx_dft_fwd")
        z = _cmul(x, spec, ss, o)
        u = _mm(w_inv, z, bn=WD, out_dtype=BF, name="ctx_dft_inv", epi=(hx[o], u, bias[o][None]))
    return u


def _merge_kernel(x_ref, mod_ref, ya_ref, yb_ref, yc_ref, yd_ref, gate_ref, wp_ref, wo_ref, bo_ref,
                  g_ref, b_ref, o_ref, *, alpha):
    x = x_ref[...]
    m = jnp.zeros(x.shape, F32)
    for br, y_ref in enumerate((ya_ref, yb_ref, yc_ref, yd_ref)):
        gate = _sigmoid(gate_ref[:, br * D_MODEL:(br + 1) * D_MODEL])
        m = m + gate.astype(F32) * _dot(y_ref[...], wp_ref[br])
    out = _dot(m.astype(BF), wo_ref[...]) + bo_ref[...]
    y = alpha * x + mod_ref[0][5:6] * out
    o_ref[...] = _ln(y) * g_ref[...] + b_ref[...]


def _merge(x, mod, ya, yb, yc, yd, p, wp, wo, bo, g, b, *, tm, group_tiles, alpha):
    n = x.shape[0]
    yspec = pl.BlockSpec((tm, WD), lambda i: (i, 0))
    return pl.pallas_call(
        functools.partial(_merge_kernel, alpha=alpha), grid=(n // tm,),
        in_specs=[pl.BlockSpec((tm, D_MODEL), lambda i: (i, 0)),
                  pl.BlockSpec((1, N_MOD, D_MODEL), lambda i: (i // group_tiles, 0, 0)),
                  yspec, yspec, yspec, yspec,
                  pl.BlockSpec((tm, N_BRANCH * D_MODEL), lambda i: (i, 0)),
                  _resident(wp.shape), _resident(wo.shape), _resident((1, D_MODEL)),
                  _resident((1, D_MODEL)), _resident((1, D_MODEL))],
        out_specs=pl.BlockSpec((tm, D_MODEL), lambda i: (i, 0)),
        out_shape=jax.ShapeDtypeStruct((n, D_MODEL), F32),
        compiler_params=_cparams(1), name="merge")(x, mod, ya, yb, yc, yd, p, wp, wo, bo, g, b)


def kernel(x, c, ctx, c_ctx, w_mod, b_mod, post_ln_g, post_ln_b, ffn_w_in, ffn_w_out, w_in, b_in,
           conf_dw_w, conf_dw_b, conf_ln_g, conf_ln_b, conf_w_proj, na_rpb, na_w_proj, sc_conv_w,
           sc_w_proj, hy_sconv_w, hy_sconv_b, hy_w1, hy_b1, hy_w2, hy_b2, hy_freq, hy_w3, hy_bias,
           hy_w_proj, w_out, b_out):
    bsz, n_lat, _ = x.shape
    ctx_len = ctx.shape[1]
    depth = w_mod.shape[0]
    alpha = (2 * depth) ** 0.25
    rows = n_lat // GRID_W

    tm = 512
    tmc = min(tm, ctx_len)
    tf = 1024 if n_lat % 1024 == 0 else tm
    tfc = min(tf, bsz * ctx_len)
    tp = 256
    tpc = min(tp, ctx_len)

    xl = x.reshape(bsz * n_lat, D_MODEL)
    xc = ctx.reshape(bsz * ctx_len, D_MODEL)
    lat_tiles = n_lat // tm
    ctx_tiles = bsz * ctx_len // tmc

    tables = _dft_tables(2 * n_lat // FFT_N2)
    tables_c = _dft_tables_direct(ctx_len)

    cond = jnp.concatenate([c, c_ctx[None], jnp.zeros((8 - bsz - 1, D_MODEL), F32)], axis=0)

    for l in range(depth):
        last = l == depth - 1
        mod_all = _mm(cond, w_mod[l], bn=1024, out_dtype=F32, name="adaln_mod", bias=b_mod[l][None], a_silu=True)
        mod_all = mod_all.reshape(8, N_MOD, D_MODEL)
        mod = mod_all[:bsz]
        mod_c = mod_all[bsz:bsz + 1]

        wl = w_in[l]
        w_perm = jnp.concatenate([wl[:, OFF_G:], wl[:, OFF_A:OFF_Q], wl[:, OFF_SB:OFF_G],
                                  wl[:, OFF_Q:OFF_SB]], axis=1).astype(BF)
        bl = b_in[l]
        b_perm = jnp.concatenate([bl[OFF_G:], bl[OFF_A:OFF_Q], bl[OFF_SB:OFF_G], bl[OFF_Q:OFF_SB]])[None]
        n_ch = D_FF // FFN_CHUNK
        ffn_wi = ffn_w_in[l].reshape(2, D_MODEL, 2, n_ch, FFN_CHUNK).transpose(0, 1, 3, 2, 4)
        ffn_wi = ffn_wi.reshape(2, D_MODEL, 2 * D_FF).astype(BF)
        ffn_wo = ffn_w_out[l].astype(BF)
        ln_g = post_ln_g[l][:, None, :]
        ln_b = post_ln_b[l][:, None, :]
        wp = jnp.stack([conf_w_proj[l], na_w_proj[l], sc_w_proj[l], hy_w_proj[l]]).astype(BF)
        wo = w_out[l].astype(BF)
        conv_args = (conf_dw_w[l], conf_dw_b[l][None], conf_ln_g[l][None], conf_ln_b[l][None],
                     sc_conv_w[l], hy_sconv_w[l], hy_sconv_b[l][None])
        filt_args = (hy_w1[l], hy_b1[l], hy_w2[l], hy_b2[l], hy_freq[l], hy_w3[l])

        xl = _ffn(xl, mod, ffn_wi[0], ffn_wo[0], ln_g[0], ln_b[0], m0=0, tm=tf,
                  group_tiles=n_lat // tf, alpha=alpha)
        xc = _ffn(xc, mod_c, ffn_wi[0], ffn_wo[0], ln_g[0], ln_b[0], m0=0, tm=tfc,
                  group_tiles=bsz * ctx_len // tfc, alpha=alpha)

        p = _inproj(xl, mod, w_perm, b_perm, tm=tm, group_tiles=lat_tiles)
        pc = _inproj(xc, mod_c, w_perm, b_perm, tm=tmc, group_tiles=ctx_tiles)

        ya, yc, hv, h1, h2 = _prep(p, *conv_args, tm=tp, seq_len=n_lat)
        attn = _na(p, pc, _na_bias_table(na_rpb[l], rows), bsz=bsz, seq_len=n_lat)
        kern, ss = _hyena_filters(n_lat, *filt_args)
        yd = _hyena_long(hv, (h1, h2), kern, ss, hy_bias[l], tables, bsz=bsz, seq_len=n_lat)
        xl = _merge(xl, mod, ya, attn, yc, yd, p, wp, wo, b_out[l][None], ln_g[1], ln_b[1],
                    tm=tm, group_tiles=lat_tiles, alpha=alpha)

        if not last:
            ya, yc, hv, h1, h2 = _prep(pc, *conv_args, tm=tpc, seq_len=ctx_len)
            attn_c = _ctx_attn(pc, bsz=bsz, ctx_len=ctx_len)
            kern_c, ss_c = _hyena_filters(ctx_len, *filt_args)
            yd = _hyena_short_seq(hv, (h1, h2), kern_c, ss_c, hy_bias[l], tables_c, bsz=bsz)
            xc = _merge(xc, mod_c, ya, attn_c, yc, yd, pc, wp, wo, b_out[l][None], ln_g[1], ln_b[1],
                        tm=tmc, group_tiles=ctx_tiles, alpha=alpha)

        xl = _ffn(xl, mod, ffn_wi[1], ffn_wo[1], ln_g[2], ln_b[2], m0=6, tm=tf,
                  group_tiles=n_lat // tf, alpha=alpha)
        if not last:
            xc = _ffn(xc, mod_c, ffn_wi[1], ffn_wo[1], ln_g[2], ln_b[2], m0=6, tm=tfc,
                      group_tiles=bsz * ctx_len // tfc, alpha=alpha)

    return xl.reshape(bsz, n_lat, D_MODEL)
```

```python
import functools
import math

import jax
import jax.numpy as jnp
import numpy as np
from jax import lax
from jax.experimental import pallas as pl
from jax.experimental.pallas import tpu as pltpu

D_MODEL = 1024
GRID_W = 64
N_BRANCH = 4
WA = 256
CONF_K = 31
NA_HEADS = 4
NA_HEAD_DIM = 64
WB = NA_HEADS * NA_HEAD_DIM
NA_WIN_ROWS = 8
NA_WIN_COLS = 16
ATTN_SCALE = NA_HEAD_DIM ** -0.5
WC = 256
SC_K = 3
WD = 256
HY_ORDER = 2
HY_SHORT_K = 3
HY_PE_BANDS = 16
HY_PE_DIM = 1 + 2 * HY_PE_BANDS
HY_FILT_HID = 64
HY_FAST_DECAY = 0.3
HY_SLOW_DECAY = 1.5
HY_DECAY_TARGET = 1e-2
D_FF = 2816
N_MOD = 9
LN_EPS = 1e-5

OFF_A = 0
OFF_Q = OFF_A + 2 * WA
OFF_K = OFF_Q + WB
OFF_V = OFF_K + WB
OFF_SB = OFF_V + WB
OFF_SC = OFF_SB + WC
OFF_SX = OFF_SC + WC
OFF_HV = OFF_SX + WC
OFF_G = OFF_HV + (1 + HY_ORDER) * WD
P_IN = OFF_G + N_BRANCH * D_MODEL

NEW_G = 0
NEW_CONV = N_BRANCH * D_MODEL
CONV_W = 2 * WA + 3 * WC + 3 * WD
NEW_Q = NEW_CONV + CONV_W

BF = jnp.bfloat16
F32 = jnp.float32

VMEM_LIMIT_BYTES = 58 * 1024 * 1024
FFN_CHUNK = 256
INPROJ_CHUNK = 1152
HALO = 16
SHIFT_SPAN = 24
NA_QROWS = 4
NA_TOK = NA_QROWS * GRID_W
FFT_N2 = 128
NEG_INF = -1e30


def _cparams(n_axes, semantics="parallel"):
    return pltpu.CompilerParams(dimension_semantics=(semantics,) * n_axes,
                                vmem_limit_bytes=VMEM_LIMIT_BYTES)


def _resident(shape):
    nd = len(shape)
    return pl.BlockSpec(shape, lambda *_: (0,) * nd, pipeline_mode=pl.Buffered(1))


def _ln(x):
    mu = jnp.mean(x, axis=-1, keepdims=True)
    xc = x - mu
    var = jnp.mean(xc * xc, axis=-1, keepdims=True)
    return xc * lax.rsqrt(var + LN_EPS)


def _sigmoid(x):
    return 0.5 * jnp.tanh(0.5 * x) + 0.5


def _dot(a, b):
    return jnp.dot(a, b, preferred_element_type=F32)


def _dot_nt(a, b):
    return lax.dot_general(a, b, (((1,), (1,)), ((), ())), preferred_element_type=F32)


def _mm_kernel(*refs, a_silu, has_bias, has_epi):
    a_ref, b_ref = refs[0], refs[1]
    pos = 2
    a = a_ref[...]
    if a_silu:
        a = a.astype(F32)
        a = a * _sigmoid(a)
    acc = _dot(a.astype(BF), b_ref[...].astype(BF))
    if has_bias:
        acc = acc + refs[pos][...]
        pos += 1
    if has_epi:
        gate_ref, u_ref, row_ref = refs[pos], refs[pos + 1], refs[pos + 2]
        pos += 3
        acc = gate_ref[...].astype(F32) * (acc + u_ref[...].astype(F32) * row_ref[...])
    o_ref = refs[pos]
    o_ref[...] = acc.astype(o_ref.dtype)


def _mm(a, b, *, bn, out_dtype, name, bias=None, epi=None, a_silu=False, b_layer=None):
    m, k = a.shape
    n = b.shape[-1]
    bn = min(bn, n)
    ops = [a, b]
    if b_layer is None:
        bspec = pl.BlockSpec((k, bn), lambda j: (0, j))
    else:
        bspec = pl.BlockSpec((None, k, bn), lambda j: (b_layer, 0, j))
    specs = [_resident((m, k)), bspec]
    if bias is not None:
        ops.append(bias)
        specs.append(pl.BlockSpec((1, bn), lambda j: (0, j)))
    if epi is not None:
        gate, u, row = epi
        ops += [gate, u, row]
        specs += [pl.BlockSpec((m, bn), lambda j: (0, j)),
                  pl.BlockSpec((m, bn), lambda j: (0, j)),
                  pl.BlockSpec((1, bn), lambda j: (0, j))]
    kern = functools.partial(_mm_kernel, a_silu=a_silu, has_bias=bias is not None,
                             has_epi=epi is not None)
    return pl.pallas_call(
        kern, grid=(n // bn,), in_specs=specs,
        out_specs=pl.BlockSpec((m, bn), lambda j: (0, j)),
        out_shape=jax.ShapeDtypeStruct((m, n), out_dtype),
        compiler_params=_cparams(1), name=name)(*ops)


def _modulated(x, mod, m0):
    shift = mod[m0:m0 + 1]
    scale = mod[m0 + 1:m0 + 2]
    return _ln(x) * (1.0 + scale) + shift


def _ffn_kernel(x_ref, mod_ref, wi_ref, wo_ref, g_ref, b_ref, o_ref, u_ref, *, m0, alpha):
    x = x_ref[...]
    mod = mod_ref[0]
    h = _modulated(x, mod, m0).astype(BF)
    for c in range(D_FF // FFN_CHUNK):
        lo = c * FFN_CHUNK
        a = _dot(h, wi_ref[:, lo:lo + FFN_CHUNK])
        g = _dot(h, wi_ref[:, D_FF + lo:D_FF + lo + FFN_CHUNK])
        u_ref[:, lo:lo + FFN_CHUNK] = (g * _sigmoid(g) * a).astype(BF)
    acc = _dot(u_ref[...], wo_ref[...])
    y = alpha * x + (0.5 * mod[m0 + 2:m0 + 3]) * acc
    o_ref[...] = _ln(y) * g_ref[...] + b_ref[...]


def _ffn(x, mod, w_in, w_out, g, b, *, layer, half, m0, tm, group_tiles, alpha):
    n = x.shape[0]

    def stacked(w):
        return pl.BlockSpec((None, None) + w.shape[2:], lambda i: (layer, half, 0, 0),
                            pipeline_mode=pl.Buffered(1))

    kern = functools.partial(_ffn_kernel, m0=m0, alpha=alpha)
    return pl.pallas_call(
        kern, grid=(n // tm,),
        in_specs=[pl.BlockSpec((tm, D_MODEL), lambda i: (i, 0)),
                  pl.BlockSpec((1, N_MOD, D_MODEL), lambda i: (i // group_tiles, 0, 0)),
                  stacked(w_in), stacked(w_out),
                  _resident((1, D_MODEL)), _resident((1, D_MODEL))],
        out_specs=pl.BlockSpec((tm, D_MODEL), lambda i: (i, 0)),
        out_shape=jax.ShapeDtypeStruct((n, D_MODEL), F32),
        scratch_shapes=[pltpu.VMEM((tm, D_FF), BF)],
        compiler_params=_cparams(1), name="ffn")(x, mod, w_in, w_out, g, b)


def _inproj_kernel(x_ref, mod_ref, w_ref, b_ref, o_ref, *, m0):
    h = _modulated(x_ref[...], mod_ref[0], m0).astype(BF)
    for c in range(P_IN // INPROJ_CHUNK):
        lo = c * INPROJ_CHUNK
        o_ref[:, lo:lo + INPROJ_CHUNK] = (
            _dot(h, w_ref[:, lo:lo + INPROJ_CHUNK]) + b_ref[:, lo:lo + INPROJ_CHUNK]).astype(BF)


def _inproj(x, mod, w, b, *, tm, group_tiles):
    n = x.shape[0]
    return pl.pallas_call(
        functools.partial(_inproj_kernel, m0=3), grid=(n // tm,),
        in_specs=[pl.BlockSpec((tm, D_MODEL), lambda i: (i, 0)),
                  pl.BlockSpec((1, N_MOD, D_MODEL), lambda i: (i // group_tiles, 0, 0)),
                  _resident(w.shape), _resident((1, P_IN))],
        out_specs=pl.BlockSpec((tm, P_IN), lambda i: (i, 0)),
        out_shape=jax.ShapeDtypeStruct((n, P_IN), BF),
        compiler_params=_cparams(1), name="inproj")(x, mod, w, b)


def _prep_kernel(x_ref, prev_ref, next_ref, cw_ref, cb_ref, lg_ref, lb_ref, sw_ref, hw_ref, hb_ref,
                 ya_ref, yc_ref, hv_ref, h1_ref, h2_ref, buf_a, buf_c, buf_h, shifted, *, tm, seq_tiles):
    i = pl.program_id(0)
    pos = i % seq_tiles
    keep_prev = jnp.where(pos == 0, 0.0, 1.0)
    keep_next = jnp.where(pos == seq_tiles - 1, 0.0, 1.0)

    def fill(t, lo, hi):
        glu = t[:, 0:WA] * _sigmoid(t[:, WA:2 * WA])
        buf_a[lo:hi, :] = glu
        buf_c[lo:hi, :] = t[:, 3 * WA:4 * WA] * t[:, 4 * WA:5 * WA]
        buf_h[lo:hi, :] = t[:, 5 * WA:8 * WA]

    main = x_ref[...].astype(F32)
    fill(prev_ref[...].astype(F32) * keep_prev, 0, HALO)
    fill(main, HALO, HALO + tm)
    fill(next_ref[...].astype(F32) * keep_next, HALO + tm, 2 * HALO + tm)

    span = tm + SHIFT_SPAN
    for r in range(8):
        shifted[r] = buf_a[r:r + span, :]
    acc = jnp.zeros((tm, WA), F32) + cb_ref[...]
    for j in range(CONF_K):
        off = HALO - CONF_K // 2 + j
        base = off - off % 8
        acc = acc + cw_ref[j:j + 1, :] * shifted[off % 8, base:base + tm, :]
    u = _ln(acc) * lg_ref[...] + lb_ref[...]
    ya_ref[...] = (u * _sigmoid(u)).astype(BF)

    acc = jnp.zeros((tm, WC), F32)
    for j in range(SC_K):
        off = HALO - SC_K // 2 + j
        acc = acc + sw_ref[j:j + 1, :] * buf_c[off:off + tm, :]
    yc_ref[...] = (main[:, 2 * WA:3 * WA] * acc).astype(BF)

    acc = jnp.zeros((tm, 3 * WD), F32) + hb_ref[...]
    for j in range(HY_SHORT_K):
        off = HALO - HY_SHORT_K // 2 + j
        acc = acc + hw_ref[j:j + 1, :] * buf_h[off:off + tm, :]
    hv_ref[...] = acc[:, 0:WD].astype(BF)
    h1_ref[...] = acc[:, WD:2 * WD].astype(BF)
    h2_ref[...] = acc[:, 2 * WD:3 * WD].astype(BF)


def _prep(p, cw, cb, lg, lb, sw, hw, hb, *, tm, seq_len):
    n = p.shape[0]
    seq_tiles = seq_len // tm
    hb_per_tile = tm // HALO
    n_halo_blocks = n // HALO
    cblk = NEW_CONV // CONV_W
    kern = functools.partial(_prep_kernel, tm=tm, seq_tiles=seq_tiles)
    out_bf = jax.ShapeDtypeStruct((n, WD), BF)
    ospec = pl.BlockSpec((tm, WD), lambda i: (i, 0))
    return pl.pallas_call(
        kern, grid=(n // tm,),
        in_specs=[pl.BlockSpec((tm, CONV_W), lambda i: (i, cblk)),
                  pl.BlockSpec((HALO, CONV_W),
                               lambda i: (jnp.maximum(i * hb_per_tile - 1, 0), cblk)),
                  pl.BlockSpec((HALO, CONV_W),
                               lambda i: (jnp.minimum((i + 1) * hb_per_tile, n_halo_blocks - 1), cblk)),
                  _resident(cw.shape), _resident(cb.shape), _resident(lg.shape), _resident(lb.shape),
                  _resident(sw.shape), _resident(hw.shape), _resident(hb.shape)],
        out_specs=[ospec] * 5, out_shape=[out_bf] * 5,
        scratch_shapes=[pltpu.VMEM((tm + 2 * HALO, WA), F32),
                        pltpu.VMEM((tm + 2 * HALO, WC), F32),
                        pltpu.VMEM((tm + 2 * HALO, 3 * WD), F32),
                        pltpu.VMEM((8, tm + SHIFT_SPAN, WA), F32)],
        compiler_params=_cparams(1), name="prep")(p, p, p, cw, cb, lg, lb, sw, hw, hb)


def _na_kernel(q_ref, k0_ref, k1_ref, k2_ref, v0_ref, v1_ref, v2_ref, kc_ref, vc_ref, bias_ref, o_ref):
    q = q_ref[...] * ATTN_SCALE
    k = jnp.concatenate([k0_ref[...], k1_ref[...], k2_ref[...]], axis=0)
    v = jnp.concatenate([v0_ref[...], v1_ref[...], v2_ref[...]], axis=0)
    kc = kc_ref[...]
    vc = vc_ref[...]
    outs = []
    for h in range(NA_HEADS):
        sl = slice(h * NA_HEAD_DIM, (h + 1) * NA_HEAD_DIM)
        qh = q[:, sl]
        s_nb = _dot_nt(qh, k[:, sl]) + bias_ref[0, h]
        s_cx = _dot_nt(qh, kc[:, sl])
        m = jnp.maximum(jnp.max(s_nb, axis=-1, keepdims=True), jnp.max(s_cx, axis=-1, keepdims=True))
        p_nb = jnp.exp(s_nb - m)
        p_cx = jnp.exp(s_cx - m)
        den = jnp.sum(p_nb, axis=-1, keepdims=True) + jnp.sum(p_cx, axis=-1, keepdims=True)
        o = _dot(p_nb.astype(BF), v[:, sl]) + _dot(p_cx.astype(BF), vc[:, sl])
        outs.append(o / den)
    o_ref[...] = jnp.concatenate(outs, axis=-1).astype(BF)


def _rpb_expand_kernel(rpb_ref, onehot_ref, o_ref):
    o_ref[...] = jnp.dot(rpb_ref[...], onehot_ref[...], precision=lax.Precision.HIGHEST,
                         preferred_element_type=F32)


def _na_bias_table(rpb, rows):
    assert rows >= NA_WIN_ROWS and rows % NA_QROWS == 0 and rows // NA_QROWS >= 3
    n_dr, n_dc = 2 * NA_WIN_ROWS - 1, 2 * NA_WIN_COLS - 1
    cols = np.arange(GRID_W)
    c0 = np.clip(cols - NA_WIN_COLS // 2, 0, GRID_W - NA_WIN_COLS)
    dc = cols[None, :] - cols[:, None] + (NA_WIN_COLS - 1)
    ok_c = (cols[None, :] >= c0[:, None]) & (cols[None, :] < c0[:, None] + NA_WIN_COLS)
    onehot = np.zeros((128, GRID_W, GRID_W), np.float32)
    qq, kk = np.nonzero(ok_c)
    onehot[dc[qq, kk], qq, kk] = 1.0
    rpb2 = jnp.pad(rpb.reshape(NA_HEADS * n_dr, n_dc).astype(F32),
                   ((0, 64 - NA_HEADS * n_dr), (0, 128 - n_dc)))
    t = pl.pallas_call(
        _rpb_expand_kernel, grid=(1,),
        in_specs=[_resident((64, 128)), _resident((128, GRID_W * GRID_W))],
        out_specs=pl.BlockSpec((64, GRID_W * GRID_W), lambda i: (0, 0)),
        out_shape=jax.ShapeDtypeStruct((64, GRID_W * GRID_W), F32),
        compiler_params=_cparams(1), name="rpb_expand")(rpb2, jnp.asarray(onehot.reshape(128, GRID_W * GRID_W)))
    t = t[:NA_HEADS * n_dr].reshape(NA_HEADS, n_dr, GRID_W, GRID_W)
    full = jnp.concatenate(
        [jnp.concatenate([t[:, s - a + NA_WIN_ROWS - 1 - NA_QROWS] for s in range(3 * NA_QROWS)], axis=-1)
         for a in range(NA_QROWS)], axis=1)
    wr = NA_WIN_ROWS
    n_blk = rows // NA_QROWS
    tabs = []
    for blk in (0, 1, n_blk - 1):
        qr = blk * NA_QROWS + np.arange(NA_QROWS)
        kr = (blk - 1) * NA_QROWS + np.arange(3 * NA_QROWS)
        r0 = np.clip(qr - wr // 2, 0, rows - wr)
        ok_r = ((kr[None, :] >= r0[:, None]) & (kr[None, :] < r0[:, None] + wr)
                & (kr[None, :] >= 0) & (kr[None, :] < rows))
        ok = (ok_r[:, None, :, None] & ok_c[None, :, None, :]).reshape(NA_TOK, 3 * NA_TOK)
        tabs.append(jnp.where(jnp.asarray(ok)[None], full, NEG_INF))
    return jnp.stack(tabs)


def _na(p, pc, bias_tab, *, bsz, seq_len):
    t = seq_len // NA_TOK
    qb, kb, vb = NEW_Q // WB, NEW_Q // WB + 1, NEW_Q // WB + 2
    blk = (NA_TOK, WB)
    cblk = (pc.shape[0] // bsz, WB)

    def nbr(j, col):
        return pl.BlockSpec(blk, lambda b, i: (b * t + jnp.clip(i - 1 + j, 0, t - 1), col))

    return pl.pallas_call(
        _na_kernel, grid=(bsz, t),
        in_specs=[pl.BlockSpec(blk, lambda b, i: (b * t + i, qb)),
                  nbr(0, kb), nbr(1, kb), nbr(2, kb), nbr(0, vb), nbr(1, vb), nbr(2, vb),
                  pl.BlockSpec(cblk, lambda b, i: (b, kb)),
                  pl.BlockSpec(cblk, lambda b, i: (b, vb)),
                  pl.BlockSpec((1, NA_HEADS, NA_TOK, 3 * NA_TOK),
                               lambda b, i: (jnp.where(i == 0, 0, jnp.where(i == t - 1, 2, 1)), 0, 0, 0))],
        out_specs=pl.BlockSpec(blk, lambda b, i: (b * t + i, 0)),
        out_shape=jax.ShapeDtypeStruct((bsz * seq_len, WB), BF),
        compiler_params=_cparams(2), name="na")(p, p, p, p, p, p, p, pc, pc, bias_tab)


def _ctx_attn_kernel(q_ref, k_ref, v_ref, o_ref):
    q = q_ref[...] * ATTN_SCALE
    k = k_ref[...]
    v = v_ref[...]
    outs = []
    for h in range(NA_HEADS):
        sl = slice(h * NA_HEAD_DIM, (h + 1) * NA_HEAD_DIM)
        s = _dot_nt(q[:, sl], k[:, sl])
        m = jnp.max(s, axis=-1, keepdims=True)
        p = jnp.exp(s - m)
        den = jnp.sum(p, axis=-1, keepdims=True)
        outs.append(_dot(p.astype(BF), v[:, sl]) / den)
    o_ref[...] = jnp.concatenate(outs, axis=-1).astype(BF)


def _ctx_attn(pc, *, bsz, ctx_len):
    qb, kb, vb = NEW_Q // WB, NEW_Q // WB + 1, NEW_Q // WB + 2
    blk = (ctx_len, WB)
    return pl.pallas_call(
        _ctx_attn_kernel, grid=(bsz,),
        in_specs=[pl.BlockSpec(blk, lambda b: (b, qb)), pl.BlockSpec(blk, lambda b: (b, kb)),
                  pl.BlockSpec(blk, lambda b: (b, vb))],
        out_specs=pl.BlockSpec(blk, lambda b: (b, 0)),
        out_shape=jax.ShapeDtypeStruct((bsz * ctx_len, WB), BF),
        compiler_params=_cparams(1), name="ctx_attn")(pc, pc, pc)


def _filt_kernel(zf_ref, zb_ref, w1_ref, b1_ref, w2_ref, b2_ref, fr_ref, w3_ref, dl_ref, k_ref, ss_ref,
                 *, tm):
    i = pl.program_id(0)
    hp = lax.Precision.HIGHEST
    zf = zf_ref[...]
    zb = zb_ref[...]
    w1 = w1_ref[...]
    pre = jnp.concatenate([jnp.dot(zf, w1, precision=hp, preferred_element_type=F32),
                           jnp.dot(zb, w1, precision=hp, preferred_element_type=F32)], axis=-1)
    h = jnp.sin(fr_ref[0:1, :] * (pre + b1_ref[...]))
    h = jnp.sin(fr_ref[1:2, :] * (jnp.dot(h, w2_ref[...], precision=hp, preferred_element_type=F32)
                                  + b2_ref[...]))
    k = jnp.dot(h, w3_ref[...], precision=hp, preferred_element_type=F32)
    dec_f = jnp.exp(-zf[:, 0:1] * dl_ref[...])
    dec_b = jnp.exp(-zb[:, 0:1] * dl_ref[...])
    k = k * jnp.concatenate([dec_f] * HY_ORDER + [dec_b] * HY_ORDER, axis=-1)
    is_row0 = (i * tm + lax.broadcasted_iota(jnp.int32, (tm, 1), 0)) == 0
    kf = k[:, :HY_ORDER * WD]
    kb = jnp.where(is_row0, 0.0, k[:, HY_ORDER * WD:])
    k_ref[0] = kf.astype(BF)
    k_ref[1] = kb.astype(BF)

    @pl.when(i == 0)
    def _():
        ss_ref[...] = jnp.zeros_like(ss_ref)

    ss_ref[...] += jnp.sum(kf * kf + kb * kb, axis=0, keepdims=True)


def _pos_features(t, length):
    t_norm = t / max(length - 1, 1)
    bands = jnp.linspace(1e-4, HY_PE_BANDS - 1, HY_PE_BANDS, dtype=F32)
    ang = (2.0 * math.pi / length) * t[:, None] * bands[None, :]
    z = jnp.concatenate([t_norm[:, None], jnp.cos(ang), -jnp.sin(ang)], axis=-1)
    return jnp.pad(z, ((0, 0), (0, 128 - HY_PE_DIM)))


def _block_diag(a, b):
    return jnp.concatenate([jnp.concatenate([a, jnp.zeros((a.shape[0], b.shape[1]), a.dtype)], axis=1),
                            jnp.concatenate([jnp.zeros((b.shape[0], a.shape[1]), a.dtype), b], axis=1)],
                           axis=0)


def _hyena_filters(length, w1, b1, w2, b2, freq, w3):
    t = jnp.arange(length, dtype=F32)
    zf = _pos_features(t, length)
    zb = _pos_features(length - t, length)
    w1p = jnp.pad(w1, ((0, 128 - HY_PE_DIM), (0, 0)))
    w3r = w3.reshape(HY_FILT_HID, HY_ORDER, 2, WD)
    w3bd = _block_diag(w3r[:, :, 0].reshape(HY_FILT_HID, HY_ORDER * WD),
                       w3r[:, :, 1].reshape(HY_FILT_HID, HY_ORDER * WD))
    w2bd = _block_diag(w2, w2)
    b1d = jnp.concatenate([b1, b1])[None]
    b2d = jnp.concatenate([b2, b2])[None]
    frd = jnp.concatenate([freq, freq], axis=1)
    deltas = jnp.abs(jnp.linspace(math.log(HY_DECAY_TARGET) / HY_SLOW_DECAY,
                                  math.log(HY_DECAY_TARGET) / HY_FAST_DECAY, WD, dtype=F32))[None]
    tm = min(512, length)
    nf = HY_ORDER * WD
    zspec = pl.BlockSpec((tm, 128), lambda i: (i, 0))
    k, ss = pl.pallas_call(
        functools.partial(_filt_kernel, tm=tm), grid=(length // tm,),
        in_specs=[zspec, zspec, _resident(w1p.shape), _resident(b1d.shape), _resident(w2bd.shape),
                  _resident(b2d.shape), _resident(frd.shape), _resident(w3bd.shape),
                  _resident(deltas.shape)],
        out_specs=[pl.BlockSpec((2, tm, nf), lambda i: (0, i, 0)), pl.BlockSpec((1, nf), lambda i: (0, 0))],
        out_shape=[jax.ShapeDtypeStruct((2, length, nf), BF), jax.ShapeDtypeStruct((1, nf), F32)],
        compiler_params=_cparams(1, "arbitrary"), name="hyena_filter")(zf, zb, w1p, b1d, w2bd, b2d, frd, w3bd, deltas)
    return k.reshape(2 * length, nf), ss


def _cis(num, den):
    ang = (-2.0 * math.pi / den) * (num % den).astype(F32)
    return jnp.cos(ang), jnp.sin(ang)


def _stack(re, im):
    return jnp.concatenate([jnp.concatenate([re, -im], axis=1), jnp.concatenate([im, re], axis=1)], axis=0)


def _dft_tables(n1):
    n = n1 * FFT_N2
    i1 = jnp.arange(n1)
    fr, fi = _cis(i1[:, None] * i1[None, :], n1)
    half = n1 // 2
    w_fwd = _stack(fr[:, :half], fi[:, :half]).astype(BF)
    w_real = jnp.concatenate([fr, fi], axis=0).astype(BF)
    w_inv = (_stack(fr[:half, :], -fi[:half, :]) / n).astype(BF)
    i2 = jnp.arange(FFT_N2)
    num = i2[None, None, :] * (n1 * i2[None, :, None] + i1[:, None, None])
    gr, gi = _cis(num, n)
    gs = jnp.concatenate([jnp.concatenate([gr, -gi], axis=2), jnp.concatenate([gi, gr], axis=2)], axis=1)
    gs = gs.astype(BF)
    return w_fwd, w_real, w_inv, gs


def _dft_tables_direct(length):
    n = 2 * length
    i = jnp.arange(n)
    fr, fi = _cis(i[:, None] * i[None, :], n)
    w_fwd = _stack(fr[:, :length], fi[:, :length]).astype(BF)
    w_real = jnp.concatenate([fr, fi], axis=0).astype(BF)
    w_inv = (_stack(fr[:length, :], -fi[:length, :]) / n).astype(BF)
    return w_fwd, w_real, w_inv


def _stage_a_kernel(*refs, nb, n_half, has_epi):
    w_ref, x_ref = refs[0], refs[1]
    m, k = w_ref.shape
    if has_epi:
        g_ref, u_ref, row_ref, o_ref = refs[2:6]
        scratch = refs[6:]
    else:
        o_ref = refs[2]
        scratch = refs[3:]
    xs, os_ = scratch[:n_half], scratch[n_half:]
    w = w_ref[...]
    for l in range(n_half):
        lanes = slice(l * 128, (l + 1) * 128)
        xs[l][...] = x_ref[:, :, lanes].astype(F32).reshape(k * nb, 128)
        for j in range(nb):
            os_[l][pl.ds(j, m, stride=nb), :] = _dot(w, xs[l][pl.ds(j, k, stride=nb), :].astype(BF))
        acc = os_[l][...].reshape(m, nb, 128)
        if has_epi:
            acc = g_ref[:, :, lanes].astype(F32) * (
                acc + u_ref[:, :, lanes].astype(F32) * row_ref[:, lanes].reshape(1, 1, 128))
        o_ref[:, :, lanes] = acc.astype(o_ref.dtype)


def _stage_a(w, x, *, name, epi=None, nb=16, lane_blk=256):
    m, k = w.shape
    c = x.shape[2]
    n_half = lane_blk // 128
    xblk = pl.BlockSpec((k, nb, lane_blk), lambda j, l: (0, j, l))
    oblk = pl.BlockSpec((m, nb, lane_blk), lambda j, l: (0, j, l))
    ops, specs = [w, x], [_resident((m, k)), xblk]
    if epi is not None:
        ops += list(epi)
        specs += [oblk, oblk, pl.BlockSpec((1, lane_blk), lambda j, l: (0, l))]
    scratch = ([pltpu.VMEM((k * nb, 128), F32)] * n_half + [pltpu.VMEM((m * nb, 128), F32)] * n_half)
    return pl.pallas_call(
        functools.partial(_stage_a_kernel, nb=nb, n_half=n_half, has_epi=epi is not None),
        grid=(FFT_N2 // nb, c // lane_blk), in_specs=specs, out_specs=oblk, scratch_shapes=scratch,
        out_shape=jax.ShapeDtypeStruct((m, FFT_N2, c), BF), compiler_params=_cparams(2), name=name)(*ops)


def _stage_b_kernel(a_ref, f_ref, gs_ref, ss_ref, o_ref, *, kb):
    scale = lax.rsqrt(ss_ref[...] + 1e-6)
    yhs = []
    for j in range(kb):
        a = jnp.concatenate([a_ref[0, j], a_ref[1, j]], axis=0)
        f = jnp.concatenate([f_ref[0, j], f_ref[1, j]], axis=0)
        yhs.append(_dot(gs_ref[j], jnp.concatenate([a, f], axis=1)))
    zs = []
    for yh in yhs:
        y, h = yh[:, :WD], yh[:, WD:] * scale
        yr, yi = y[:FFT_N2], y[FFT_N2:]
        hr, hi = h[:FFT_N2], h[FFT_N2:]
        zs.append(jnp.concatenate([yr * hr - yi * hi, yr * hi + yi * hr], axis=0).astype(BF))
    for j in range(kb):
        b = lax.dot_general(gs_ref[j], zs[j], (((0,), (0,)), ((), ())), preferred_element_type=F32)
        o_ref[0, j] = b[:FFT_N2].astype(BF)
        o_ref[1, j] = b[FFT_N2:].astype(BF)


def _stage_b(a, filt_a, gs, ss, *, order, kb=8):
    n1 = a.shape[1]
    kb = min(kb, n1)
    dblk = pl.BlockSpec((2, kb, FFT_N2, WD), lambda i: (0, i, 0, 0))
    fblk = pl.BlockSpec((2, kb, FFT_N2, WD), lambda i: (0, i, 0, order))
    gblk = pl.BlockSpec((kb, 2 * FFT_N2, 2 * FFT_N2), lambda i: (i, 0, 0))
    return pl.pallas_call(
        functools.partial(_stage_b_kernel, kb=kb), grid=(n1 // kb,),
        in_specs=[dblk, fblk, gblk, pl.BlockSpec((1, WD), lambda i: (0, order))], out_specs=dblk,
        out_shape=jax.ShapeDtypeStruct((2, n1, FFT_N2, WD), BF),
        compiler_params=_cparams(1), name="dft_b")(a, filt_a, gs, ss)


def _cmul_kernel(x_ref, h_ref, ss_ref, o_ref, *, half):
    scale = lax.rsqrt(ss_ref[...] + 1e-6)
    xr, xi = x_ref[:half, :], x_ref[half:, :]
    hr, hi = h_ref[:half, :] * scale, h_ref[half:, :] * scale
    o_ref[:half, :] = (xr * hr - xi * hi).astype(BF)
    o_ref[half:, :] = (xr * hi + xi * hr).astype(BF)


def _cmul(x, h, ss, lane_blk):
    rows = x.shape[0]
    return pl.pallas_call(
        functools.partial(_cmul_kernel, half=rows // 2), grid=(1,),
        in_specs=[_resident(x.shape), pl.BlockSpec((rows, WD), lambda i: (0, lane_blk)),
                  pl.BlockSpec((1, WD), lambda i: (0, lane_blk))],
        out_specs=pl.BlockSpec(x.shape, lambda i: (0, 0)),
        out_shape=jax.ShapeDtypeStruct(x.shape, BF), compiler_params=_cparams(1), name="ctx_cmul")(x, h, ss)


def _hyena_long(hv, hx, kern, ss, bias, tables, *, bsz, seq_len):
    assert bsz == 2
    n1 = 2 * seq_len // FFT_N2
    w_fwd, w_real, w_inv, gs = tables
    view = (bsz * seq_len // FFT_N2, FFT_N2, WD)
    filt_a = _stage_a(w_real, kern.reshape(n1, FFT_N2, HY_ORDER * WD), name="dft_a_filter")
    filt_a = filt_a.reshape(2, n1, FFT_N2, HY_ORDER * WD)
    u = hv.reshape(view)
    for o in range(HY_ORDER):
        a = _stage_a(w_fwd, u, name="dft_a_fwd").reshape(2, n1, FFT_N2, WD)
        b = _stage_b(a, filt_a, gs, ss, order=o).reshape(2 * n1, FFT_N2, WD)
        u = _stage_a(w_inv, b, name="dft_a_inv", epi=(hx[o].reshape(view), u, bias[o][None]))
    return u.reshape(bsz * seq_len, WD)


def _hyena_short_seq(hv, hx, kern, ss, bias, tables, *, bsz):
    assert bsz == 2
    w_fwd, w_real, w_inv = tables
    spec = _mm(w_real, kern, bn=HY_ORDER * WD, out_dtype=F32, name="ctx_dft_filter")
    u = hv
    for o in range(HY_ORDER):
        x = _mm(w_fwd, u, bn=WD, out_dtype=F32, name="ctx_dft_fwd")
        z = _cmul(x, spec, ss, o)
        u = _mm(w_inv, z, bn=WD, out_dtype=BF, name="ctx_dft_inv", epi=(hx[o], u, bias[o][None]))
    return u


def _merge_kernel(x_ref, mod_ref, ya_ref, yb_ref, yc_ref, yd_ref, gate_ref, wp_ref, wo_ref, bo_ref,
                  g_ref, b_ref, o_ref, *, alpha):
    x = x_ref[...]
    m = jnp.zeros(x.shape, F32)
    for br, y_ref in enumerate((ya_ref, yb_ref, yc_ref, yd_ref)):
        gate = _sigmoid(gate_ref[:, br * D_MODEL:(br + 1) * D_MODEL])
        m = m + gate.astype(F32) * _dot(y_ref[...], wp_ref[br])
    out = _dot(m.astype(BF), wo_ref[...]) + bo_ref[...]
    y = alpha * x + mod_ref[0][5:6] * out
    o_ref[...] = _ln(y) * g_ref[...] + b_ref[...]


def _merge(x, mod, ya, yb, yc, yd, p, wp, wo, bo, g, b, *, tm, group_tiles, alpha):
    n = x.shape[0]
    yspec = pl.BlockSpec((tm, WD), lambda i: (i, 0))
    return pl.pallas_call(
        functools.partial(_merge_kernel, alpha=alpha), grid=(n // tm,),
        in_specs=[pl.BlockSpec((tm, D_MODEL), lambda i: (i, 0)),
                  pl.BlockSpec((1, N_MOD, D_MODEL), lambda i: (i // group_tiles, 0, 0)),
                  yspec, yspec, yspec, yspec,
                  pl.BlockSpec((tm, N_BRANCH * D_MODEL), lambda i: (i, 0)),
                  _resident(wp.shape), _resident(wo.shape), _resident((1, D_MODEL)),
                  _resident((1, D_MODEL)), _resident((1, D_MODEL))],
        out_specs=pl.BlockSpec((tm, D_MODEL), lambda i: (i, 0)),
        out_shape=jax.ShapeDtypeStruct((n, D_MODEL), F32),
        compiler_params=_cparams(1), name="merge")(x, mod, ya, yb, yc, yd, p, wp, wo, bo, g, b)


def kernel(x, c, ctx, c_ctx, w_mod, b_mod, post_ln_g, post_ln_b, ffn_w_in, ffn_w_out, w_in, b_in,
           conf_dw_w, conf_dw_b, conf_ln_g, conf_ln_b, conf_w_proj, na_rpb, na_w_proj, sc_conv_w,
           sc_w_proj, hy_sconv_w, hy_sconv_b, hy_w1, hy_b1, hy_w2, hy_b2, hy_freq, hy_w3, hy_bias,
           hy_w_proj, w_out, b_out):
    bsz, n_lat, _ = x.shape
    ctx_len = ctx.shape[1]
    depth = w_mod.shape[0]
    alpha = (2 * depth) ** 0.25
    rows = n_lat // GRID_W

    tm = 512
    tmc = min(tm, ctx_len)
    tf = 1024 if n_lat % 1024 == 0 else tm
    tfc = min(tf, bsz * ctx_len)
    tp = 256
    tpc = min(tp, ctx_len)

    xl = x.reshape(bsz * n_lat, D_MODEL)
    xc = ctx.reshape(bsz * ctx_len, D_MODEL)
    lat_tiles = n_lat // tm
    ctx_tiles = bsz * ctx_len // tmc

    tables = _dft_tables(2 * n_lat // FFT_N2)
    tables_c = _dft_tables_direct(ctx_len)

    cond = jnp.concatenate([c, c_ctx[None], jnp.zeros((8 - bsz - 1, D_MODEL), F32)], axis=0)

    ffn_wi = ffn_w_in.astype(BF)
    ffn_wo = ffn_w_out.astype(BF)

    for l in range(depth):
        last = l == depth - 1
        mod_all = _mm(cond, w_mod, bn=1024, out_dtype=F32, name="adaln_mod", bias=b_mod[l][None], a_silu=True,
                      b_layer=l)
        mod_all = mod_all.reshape(8, N_MOD, D_MODEL)
        mod = mod_all[:bsz]
        mod_c = mod_all[bsz:bsz + 1]

        wl = w_in[l]
        w_perm = jnp.concatenate([wl[:, OFF_G:], wl[:, OFF_A:OFF_Q], wl[:, OFF_SB:OFF_G],
                                  wl[:, OFF_Q:OFF_SB]], axis=1).astype(BF)
        bl = b_in[l]
        b_perm = jnp.concatenate([bl[OFF_G:], bl[OFF_A:OFF_Q], bl[OFF_SB:OFF_G], bl[OFF_Q:OFF_SB]])[None]
        ln_g = post_ln_g[l][:, None, :]
        ln_b = post_ln_b[l][:, None, :]
        wp = jnp.stack([conf_w_proj[l], na_w_proj[l], sc_w_proj[l], hy_w_proj[l]]).astype(BF)
        wo = w_out[l].astype(BF)
        conv_args = (conf_dw_w[l], conf_dw_b[l][None], conf_ln_g[l][None], conf_ln_b[l][None],
                     sc_conv_w[l], hy_sconv_w[l], hy_sconv_b[l][None])
        filt_args = (hy_w1[l], hy_b1[l], hy_w2[l], hy_b2[l], hy_freq[l], hy_w3[l])

        xl = _ffn(xl, mod, ffn_wi, ffn_wo, ln_g[0], ln_b[0], layer=l, half=0, m0=0, tm=tf,
                  group_tiles=n_lat // tf, alpha=alpha)
        xc = _ffn(xc, mod_c, ffn_wi, ffn_wo, ln_g[0], ln_b[0], layer=l, half=0, m0=0, tm=tfc,
                  group_tiles=bsz * ctx_len // tfc, alpha=alpha)

        p = _inproj(xl, mod, w_perm, b_perm, tm=tm, group_tiles=lat_tiles)
        pc = _inproj(xc, mod_c, w_perm, b_perm, tm=tmc, group_tiles=ctx_tiles)

        ya, yc, hv, h1, h2 = _prep(p, *conv_args, tm=tp, seq_len=n_lat)
        attn = _na(p, pc, _na_bias_table(na_rpb[l], rows), bsz=bsz, seq_len=n_lat)
        kern, ss = _hyena_filters(n_lat, *filt_args)
        yd = _hyena_long(hv, (h1, h2), kern, ss, hy_bias[l], tables, bsz=bsz, seq_len=n_lat)
        xl = _merge(xl, mod, ya, attn, yc, yd, p, wp, wo, b_out[l][None], ln_g[1], ln_b[1],
                    tm=tm, group_tiles=lat_tiles, alpha=alpha)

        if not last:
            ya, yc, hv, h1, h2 = _prep(pc, *conv_args, tm=tpc, seq_len=ctx_len)
            attn_c = _ctx_attn(pc, bsz=bsz, ctx_len=ctx_len)
            kern_c, ss_c = _hyena_filters(ctx_len, *filt_args)
            yd = _hyena_short_seq(hv, (h1, h2), kern_c, ss_c, hy_bias[l], tables_c, bsz=bsz)
            xc = _merge(xc, mod_c, ya, attn_c, yc, yd, pc, wp, wo, b_out[l][None], ln_g[1], ln_b[1],
                        tm=tmc, group_tiles=ctx_tiles, alpha=alpha)

        xl = _ffn(xl, mod, ffn_wi, ffn_wo, ln_g[2], ln_b[2], layer=l, half=1, m0=6, tm=tf,
                  group_tiles=n_lat // tf, alpha=alpha)
        if not last:
            xc = _ffn(xc, mod_c, ffn_wi, ffn_wo, ln_g[2], ln_b[2], layer=l, half=1, m0=6, tm=tfc,
                      group_tiles=bsz * ctx_len // tfc, alpha=alpha)

    return xl.reshape(bsz, n_lat, D_MODEL)
```

```python
import functools
import math

import jax
import jax.numpy as jnp
import numpy as np
from jax import lax
from jax.experimental import pallas as pl
from jax.experimental.pallas import tpu as pltpu

D_MODEL = 1024
GRID_W = 64
N_BRANCH = 4
WA = 256
CONF_K = 31
NA_HEADS = 4
NA_HEAD_DIM = 64
WB = NA_HEADS * NA_HEAD_DIM
NA_WIN_ROWS = 8
NA_WIN_COLS = 16
ATTN_SCALE = NA_HEAD_DIM ** -0.5
WC = 256
SC_K = 3
WD = 256
HY_ORDER = 2
HY_SHORT_K = 3
HY_PE_BANDS = 16
HY_PE_DIM = 1 + 2 * HY_PE_BANDS
HY_FILT_HID = 64
HY_FAST_DECAY = 0.3
HY_SLOW_DECAY = 1.5
HY_DECAY_TARGET = 1e-2
D_FF = 2816
N_MOD = 9
LN_EPS = 1e-5

OFF_A = 0
OFF_Q = OFF_A + 2 * WA
OFF_K = OFF_Q + WB
OFF_V = OFF_K + WB
OFF_SB = OFF_V + WB
OFF_SC = OFF_SB + WC
OFF_SX = OFF_SC + WC
OFF_HV = OFF_SX + WC
OFF_G = OFF_HV + (1 + HY_ORDER) * WD
P_IN = OFF_G + N_BRANCH * D_MODEL

NEW_G = 0
NEW_CONV = N_BRANCH * D_MODEL
CONV_W = 2 * WA + 3 * WC + 3 * WD
NEW_Q = NEW_CONV + CONV_W

BF = jnp.bfloat16
F32 = jnp.float32

VMEM_LIMIT_BYTES = 58 * 1024 * 1024
FFN_CHUNK = 256
INPROJ_CHUNK = 768
HALO = 16
SHIFT_SPAN = 24
NA_QROWS = 4
NA_TOK = NA_QROWS * GRID_W
FFT_N2 = 128
NEG_INF = -1e30


def _cparams(n_axes, semantics="parallel"):
    return pltpu.CompilerParams(dimension_semantics=(semantics,) * n_axes,
                                vmem_limit_bytes=VMEM_LIMIT_BYTES)


def _resident(shape):
    nd = len(shape)
    return pl.BlockSpec(shape, lambda *_: (0,) * nd, pipeline_mode=pl.Buffered(1))


def _ln(x):
    mu = jnp.mean(x, axis=-1, keepdims=True)
    xc = x - mu
    var = jnp.mean(xc * xc, axis=-1, keepdims=True)
    return xc * lax.rsqrt(var + LN_EPS)


def _sigmoid(x):
    return 0.5 * jnp.tanh(0.5 * x) + 0.5


def _dot(a, b):
    return jnp.dot(a, b, preferred_element_type=F32)


def _dot_nt(a, b):
    return lax.dot_general(a, b, (((1,), (1,)), ((), ())), preferred_element_type=F32)


def _mm_kernel(*refs, a_silu, has_bias, has_epi):
    a_ref, b_ref = refs[0], refs[1]
    pos = 2
    a = a_ref[...]
    if a_silu:
        a = a.astype(F32)
        a = a * _sigmoid(a)
    acc = _dot(a.astype(BF), b_ref[...].astype(BF))
    if has_bias:
        acc = acc + refs[pos][...]
        pos += 1
    if has_epi:
        gate_ref, u_ref, row_ref = refs[pos], refs[pos + 1], refs[pos + 2]
        pos += 3
        acc = gate_ref[...].astype(F32) * (acc + u_ref[...].astype(F32) * row_ref[...])
    o_ref = refs[pos]
    o_ref[...] = acc.astype(o_ref.dtype)


def _mm(a, b, *, bn, out_dtype, name, bias=None, epi=None, a_silu=False, b_layer=None):
    m, k = a.shape
    n = b.shape[-1]
    bn = min(bn, n)
    ops = [a, b]
    if b_layer is None:
        bspec = pl.BlockSpec((k, bn), lambda j: (0, j))
    else:
        bspec = pl.BlockSpec((None, k, bn), lambda j: (b_layer, 0, j))
    specs = [_resident((m, k)), bspec]
    if bias is not None:
        ops.append(bias)
        specs.append(pl.BlockSpec((1, bn), lambda j: (0, j)))
    if epi is not None:
        gate, u, row = epi
        ops += [gate, u, row]
        specs += [pl.BlockSpec((m, bn), lambda j: (0, j)),
                  pl.BlockSpec((m, bn), lambda j: (0, j)),
                  pl.BlockSpec((1, bn), lambda j: (0, j))]
    kern = functools.partial(_mm_kernel, a_silu=a_silu, has_bias=bias is not None,
                             has_epi=epi is not None)
    return pl.pallas_call(
        kern, grid=(n // bn,), in_specs=specs,
        out_specs=pl.BlockSpec((m, bn), lambda j: (0, j)),
        out_shape=jax.ShapeDtypeStruct((m, n), out_dtype),
        compiler_params=_cparams(1), name=name)(*ops)


def _modulated(x, mod, m0):
    shift = mod[m0:m0 + 1]
    scale = mod[m0 + 1:m0 + 2]
    return _ln(x) * (1.0 + scale) + shift


def _ffn_kernel(x_ref, mod_ref, wi_ref, wo_ref, g_ref, b_ref, o_ref, u_ref, *, m0, alpha):
    x = x_ref[...]
    mod = mod_ref[0]
    h = _modulated(x, mod, m0).astype(BF)
    for c in range(D_FF // FFN_CHUNK):
        lo = c * FFN_CHUNK
        a = _dot(h, wi_ref[:, lo:lo + FFN_CHUNK])
        g = _dot(h, wi_ref[:, D_FF + lo:D_FF + lo + FFN_CHUNK])
        u_ref[:, lo:lo + FFN_CHUNK] = (g * _sigmoid(g) * a).astype(BF)
    acc = _dot(u_ref[...], wo_ref[...])
    y = alpha * x + (0.5 * mod[m0 + 2:m0 + 3]) * acc
    o_ref[...] = _ln(y) * g_ref[...] + b_ref[...]


def _ffn(x, mod, w_in, w_out, g, b, *, layer, half, m0, tm, group_tiles, alpha):
    n = x.shape[0]

    def stacked(w):
        return pl.BlockSpec((None, None) + w.shape[2:], lambda i: (layer, half, 0, 0),
                            pipeline_mode=pl.Buffered(1))

    kern = functools.partial(_ffn_kernel, m0=m0, alpha=alpha)
    return pl.pallas_call(
        kern, grid=(n // tm,),
        in_specs=[pl.BlockSpec((tm, D_MODEL), lambda i: (i, 0)),
                  pl.BlockSpec((1, N_MOD, D_MODEL), lambda i: (i // group_tiles, 0, 0)),
                  stacked(w_in), stacked(w_out),
                  _resident((1, D_MODEL)), _resident((1, D_MODEL))],
        out_specs=pl.BlockSpec((tm, D_MODEL), lambda i: (i, 0)),
        out_shape=jax.ShapeDtypeStruct((n, D_MODEL), F32),
        scratch_shapes=[pltpu.VMEM((tm, D_FF), BF)],
        compiler_params=_cparams(1), name="ffn")(x, mod, w_in, w_out, g, b)


def _inproj_kernel(x_ref, mod_ref, w_ref, b_ref, o_ref, *, m0):
    h = _modulated(x_ref[...], mod_ref[0], m0).astype(BF)
    for c in range(P_IN // INPROJ_CHUNK):
        lo = c * INPROJ_CHUNK
        o_ref[:, lo:lo + INPROJ_CHUNK] = (
            _dot(h, w_ref[:, lo:lo + INPROJ_CHUNK]) + b_ref[:, lo:lo + INPROJ_CHUNK]).astype(BF)


def _inproj(x, mod, w, b, *, tm, group_tiles):
    n = x.shape[0]
    return pl.pallas_call(
        functools.partial(_inproj_kernel, m0=3), grid=(n // tm,),
        in_specs=[pl.BlockSpec((tm, D_MODEL), lambda i: (i, 0)),
                  pl.BlockSpec((1, N_MOD, D_MODEL), lambda i: (i // group_tiles, 0, 0)),
                  _resident(w.shape), _resident((1, P_IN))],
        out_specs=pl.BlockSpec((tm, P_IN), lambda i: (i, 0)),
        out_shape=jax.ShapeDtypeStruct((n, P_IN), BF),
        compiler_params=_cparams(1), name="inproj")(x, mod, w, b)


def _prep_kernel(x_ref, prev_ref, next_ref, cw_ref, cb_ref, lg_ref, lb_ref, sw_ref, hw_ref, hb_ref,
                 ya_ref, yc_ref, hv_ref, h1_ref, h2_ref, buf_a, buf_c, buf_h, shifted, *, tm, seq_tiles):
    i = pl.program_id(0)
    pos = i % seq_tiles
    keep_prev = jnp.where(pos == 0, 0.0, 1.0)
    keep_next = jnp.where(pos == seq_tiles - 1, 0.0, 1.0)

    def fill(t, lo, hi):
        glu = t[:, 0:WA] * _sigmoid(t[:, WA:2 * WA])
        buf_a[lo:hi, :] = glu
        buf_c[lo:hi, :] = t[:, 3 * WA:4 * WA] * t[:, 4 * WA:5 * WA]
        buf_h[lo:hi, :] = t[:, 5 * WA:8 * WA]

    main = x_ref[...].astype(F32)
    fill(prev_ref[...].astype(F32) * keep_prev, 0, HALO)
    fill(main, HALO, HALO + tm)
    fill(next_ref[...].astype(F32) * keep_next, HALO + tm, 2 * HALO + tm)

    span = tm + SHIFT_SPAN
    for r in range(8):
        shifted[r] = buf_a[r:r + span, :]
    acc = jnp.zeros((tm, WA), F32) + cb_ref[...]
    for j in range(CONF_K):
        off = HALO - CONF_K // 2 + j
        base = off - off % 8
        acc = acc + cw_ref[j:j + 1, :] * shifted[off % 8, base:base + tm, :]
    u = _ln(acc) * lg_ref[...] + lb_ref[...]
    ya_ref[...] = (u * _sigmoid(u)).astype(BF)

    acc = jnp.zeros((tm, WC), F32)
    for j in range(SC_K):
        off = HALO - SC_K // 2 + j
        acc = acc + sw_ref[j:j + 1, :] * buf_c[off:off + tm, :]
    yc_ref[...] = (main[:, 2 * WA:3 * WA] * acc).astype(BF)

    acc = jnp.zeros((tm, 3 * WD), F32) + hb_ref[...]
    for j in range(HY_SHORT_K):
        off = HALO - HY_SHORT_K // 2 + j
        acc = acc + hw_ref[j:j + 1, :] * buf_h[off:off + tm, :]
    hv_ref[...] = acc[:, 0:WD].astype(BF)
    h1_ref[...] = acc[:, WD:2 * WD].astype(BF)
    h2_ref[...] = acc[:, 2 * WD:3 * WD].astype(BF)


def _prep(p, cw, cb, lg, lb, sw, hw, hb, *, tm, seq_len):
    n = p.shape[0]
    seq_tiles = seq_len // tm
    hb_per_tile = tm // HALO
    n_halo_blocks = n // HALO
    cblk = NEW_CONV // CONV_W
    kern = functools.partial(_prep_kernel, tm=tm, seq_tiles=seq_tiles)
    out_bf = jax.ShapeDtypeStruct((n, WD), BF)
    ospec = pl.BlockSpec((tm, WD), lambda i: (i, 0))
    return pl.pallas_call(
        kern, grid=(n // tm,),
        in_specs=[pl.BlockSpec((tm, CONV_W), lambda i: (i, cblk)),
                  pl.BlockSpec((HALO, CONV_W),
                               lambda i: (jnp.maximum(i * hb_per_tile - 1, 0), cblk)),
                  pl.BlockSpec((HALO, CONV_W),
                               lambda i: (jnp.minimum((i + 1) * hb_per_tile, n_halo_blocks - 1), cblk)),
                  _resident(cw.shape), _resident(cb.shape), _resident(lg.shape), _resident(lb.shape),
                  _resident(sw.shape), _resident(hw.shape), _resident(hb.shape)],
        out_specs=[ospec] * 5, out_shape=[out_bf] * 5,
        scratch_shapes=[pltpu.VMEM((tm + 2 * HALO, WA), F32),
                        pltpu.VMEM((tm + 2 * HALO, WC), F32),
                        pltpu.VMEM((tm + 2 * HALO, 3 * WD), F32),
                        pltpu.VMEM((8, tm + SHIFT_SPAN, WA), F32)],
        compiler_params=_cparams(1), name="prep")(p, p, p, cw, cb, lg, lb, sw, hw, hb)


def _na_kernel(q_ref, k0_ref, k1_ref, k2_ref, v0_ref, v1_ref, v2_ref, kc_ref, vc_ref, bias_ref, o_ref):
    q = q_ref[...] * ATTN_SCALE
    k = jnp.concatenate([k0_ref[...], k1_ref[...], k2_ref[...]], axis=0)
    v = jnp.concatenate([v0_ref[...], v1_ref[...], v2_ref[...]], axis=0)
    kc = kc_ref[...]
    vc = vc_ref[...]
    ones_nb = jnp.ones((v.shape[0], NA_HEAD_DIM), BF)
    ones_cx = jnp.ones((vc.shape[0], NA_HEAD_DIM), BF)

    def scores(h):
        sl = slice(h * NA_HEAD_DIM, (h + 1) * NA_HEAD_DIM)
        qh = q[:, sl]
        return _dot_nt(qh, k[:, sl]) + bias_ref[0, h], _dot_nt(qh, kc[:, sl])

    def weighted(p_nb, p_cx, sl):
        o = (_dot(p_nb, jnp.concatenate([v[:, sl], ones_nb], axis=1))
             + _dot(p_cx, jnp.concatenate([vc[:, sl], ones_cx], axis=1)))
        return o[:, :NA_HEAD_DIM] / o[:, NA_HEAD_DIM:NA_HEAD_DIM + 1]

    outs = []
    sc = scores(0)
    pending = None
    for h in range(NA_HEADS):
        sl = slice(h * NA_HEAD_DIM, (h + 1) * NA_HEAD_DIM)
        s_nb, s_cx = sc
        if h + 1 < NA_HEADS:
            sc = scores(h + 1)
        m = jnp.maximum(jnp.max(s_nb, axis=-1, keepdims=True), jnp.max(s_cx, axis=-1, keepdims=True))
        p_nb = jnp.exp((s_nb - m).astype(BF))
        p_cx = jnp.exp((s_cx - m).astype(BF))
        if pending is not None:
            outs.append(pending())
        pending = functools.partial(weighted, p_nb, p_cx, sl)
    outs.append(pending())
    o_ref[...] = jnp.concatenate(outs, axis=-1).astype(BF)


def _rpb_expand_kernel(rpb_ref, onehot_ref, o_ref):
    o_ref[...] = jnp.dot(rpb_ref[...], onehot_ref[...], precision=lax.Precision.HIGHEST,
                         preferred_element_type=F32)


def _na_bias_table(rpb, rows):
    assert rows >= NA_WIN_ROWS and rows % NA_QROWS == 0 and rows // NA_QROWS >= 3
    n_dr, n_dc = 2 * NA_WIN_ROWS - 1, 2 * NA_WIN_COLS - 1
    cols = np.arange(GRID_W)
    c0 = np.clip(cols - NA_WIN_COLS // 2, 0, GRID_W - NA_WIN_COLS)
    dc = cols[None, :] - cols[:, None] + (NA_WIN_COLS - 1)
    ok_c = (cols[None, :] >= c0[:, None]) & (cols[None, :] < c0[:, None] + NA_WIN_COLS)
    onehot = np.zeros((128, GRID_W, GRID_W), np.float32)
    qq, kk = np.nonzero(ok_c)
    onehot[dc[qq, kk], qq, kk] = 1.0
    rpb2 = jnp.pad(rpb.reshape(NA_HEADS * n_dr, n_dc).astype(F32),
                   ((0, 64 - NA_HEADS * n_dr), (0, 128 - n_dc)))
    t = pl.pallas_call(
        _rpb_expand_kernel, grid=(1,),
        in_specs=[_resident((64, 128)), _resident((128, GRID_W * GRID_W))],
        out_specs=pl.BlockSpec((64, GRID_W * GRID_W), lambda i: (0, 0)),
        out_shape=jax.ShapeDtypeStruct((64, GRID_W * GRID_W), F32),
        compiler_params=_cparams(1), name="rpb_expand")(rpb2, jnp.asarray(onehot.reshape(128, GRID_W * GRID_W)))
    t = t[:NA_HEADS * n_dr].reshape(NA_HEADS, n_dr, GRID_W, GRID_W)
    full = jnp.concatenate(
        [jnp.concatenate([t[:, s - a + NA_WIN_ROWS - 1 - NA_QROWS] for s in range(3 * NA_QROWS)], axis=-1)
         for a in range(NA_QROWS)], axis=1)
    wr = NA_WIN_ROWS
    n_blk = rows // NA_QROWS
    tabs = []
    for blk in (0, 1, n_blk - 1):
        qr = blk * NA_QROWS + np.arange(NA_QROWS)
        kr = (blk - 1) * NA_QROWS + np.arange(3 * NA_QROWS)
        r0 = np.clip(qr - wr // 2, 0, rows - wr)
        ok_r = ((kr[None, :] >= r0[:, None]) & (kr[None, :] < r0[:, None] + wr)
                & (kr[None, :] >= 0) & (kr[None, :] < rows))
        ok = (ok_r[:, None, :, None] & ok_c[None, :, None, :]).reshape(NA_TOK, 3 * NA_TOK)
        tabs.append(jnp.where(jnp.asarray(ok)[None], full, NEG_INF))
    return jnp.stack(tabs)


def _na(p, pc, bias_tab, *, bsz, seq_len):
    t = seq_len // NA_TOK
    qb, kb, vb = NEW_Q // WB, NEW_Q // WB + 1, NEW_Q // WB + 2
    blk = (NA_TOK, WB)
    cblk = (pc.shape[0] // bsz, WB)

    def nbr(j, col):
        return pl.BlockSpec(blk, lambda b, i: (b * t + jnp.clip(i - 1 + j, 0, t - 1), col))

    return pl.pallas_call(
        _na_kernel, grid=(bsz, t),
        in_specs=[pl.BlockSpec(blk, lambda b, i: (b * t + i, qb)),
                  nbr(0, kb), nbr(1, kb), nbr(2, kb), nbr(0, vb), nbr(1, vb), nbr(2, vb),
                  pl.BlockSpec(cblk, lambda b, i: (b, kb)),
                  pl.BlockSpec(cblk, lambda b, i: (b, vb)),
                  pl.BlockSpec((1, NA_HEADS, NA_TOK, 3 * NA_TOK),
                               lambda b, i: (jnp.where(i == 0, 0, jnp.where(i == t - 1, 2, 1)), 0, 0, 0))],
        out_specs=pl.BlockSpec(blk, lambda b, i: (b * t + i, 0)),
        out_shape=jax.ShapeDtypeStruct((bsz * seq_len, WB), BF),
        compiler_params=_cparams(2), name="na")(p, p, p, p, p, p, p, pc, pc, bias_tab)


def _ctx_attn_kernel(q_ref, k_ref, v_ref, o_ref):
    q = q_ref[...] * ATTN_SCALE
    k = k_ref[...]
    v = v_ref[...]
    outs = []
    for h in range(NA_HEADS):
        sl = slice(h * NA_HEAD_DIM, (h + 1) * NA_HEAD_DIM)
        s = _dot_nt(q[:, sl], k[:, sl])
        m = jnp.max(s, axis=-1, keepdims=True)
        p = jnp.exp(s - m)
        den = jnp.sum(p, axis=-1, keepdims=True)
        outs.append(_dot(p.astype(BF), v[:, sl]) / den)
    o_ref[...] = jnp.concatenate(outs, axis=-1).astype(BF)


def _ctx_attn(pc, *, bsz, ctx_len):
    qb, kb, vb = NEW_Q // WB, NEW_Q // WB + 1, NEW_Q // WB + 2
    blk = (ctx_len, WB)
    return pl.pallas_call(
        _ctx_attn_kernel, grid=(bsz,),
        in_specs=[pl.BlockSpec(blk, lambda b: (b, qb)), pl.BlockSpec(blk, lambda b: (b, kb)),
                  pl.BlockSpec(blk, lambda b: (b, vb))],
        out_specs=pl.BlockSpec(blk, lambda b: (b, 0)),
        out_shape=jax.ShapeDtypeStruct((bsz * ctx_len, WB), BF),
        compiler_params=_cparams(1), name="ctx_attn")(pc, pc, pc)


def _filt_kernel(zf_ref, zb_ref, w1_ref, b1_ref, w2_ref, b2_ref, fr_ref, w3_ref, dl_ref, k_ref, ss_ref,
                 *, tm):
    i = pl.program_id(0)
    hp = lax.Precision.HIGHEST
    zf = zf_ref[...]
    zb = zb_ref[...]
    w1 = w1_ref[...]
    pre = jnp.concatenate([jnp.dot(zf, w1, precision=hp, preferred_element_type=F32),
                           jnp.dot(zb, w1, precision=hp, preferred_element_type=F32)], axis=-1)
    h = jnp.sin(fr_ref[0:1, :] * (pre + b1_ref[...]))
    h = jnp.sin(fr_ref[1:2, :] * (jnp.dot(h, w2_ref[...], precision=hp, preferred_element_type=F32)
                                  + b2_ref[...]))
    k = _dot(h.astype(BF), w3_ref[...])
    dec_f = jnp.exp(-zf[:, 0:1] * dl_ref[...])
    dec_b = jnp.exp(-zb[:, 0:1] * dl_ref[...])
    k = k * jnp.concatenate([dec_f] * HY_ORDER + [dec_b] * HY_ORDER, axis=-1)
    is_row0 = (i * tm + lax.broadcasted_iota(jnp.int32, (tm, 1), 0)) == 0
    kf = k[:, :HY_ORDER * WD]
    kb = jnp.where(is_row0, 0.0, k[:, HY_ORDER * WD:])
    k_ref[0] = kf.astype(BF)
    k_ref[1] = kb.astype(BF)

    @pl.when(i == 0)
    def _():
        ss_ref[...] = jnp.zeros_like(ss_ref)

    ss_ref[...] += jnp.sum(kf * kf + kb * kb, axis=0, keepdims=True)


def _pos_features(t, length):
    t_norm = t / max(length - 1, 1)
    bands = jnp.linspace(1e-4, HY_PE_BANDS - 1, HY_PE_BANDS, dtype=F32)
    ang = (2.0 * math.pi / length) * t[:, None] * bands[None, :]
    z = jnp.concatenate([t_norm[:, None], jnp.cos(ang), -jnp.sin(ang)], axis=-1)
    return jnp.pad(z, ((0, 0), (0, 128 - HY_PE_DIM)))


def _block_diag(a, b):
    return jnp.concatenate([jnp.concatenate([a, jnp.zeros((a.shape[0], b.shape[1]), a.dtype)], axis=1),
                            jnp.concatenate([jnp.zeros((b.shape[0], a.shape[1]), a.dtype), b], axis=1)],
                           axis=0)


def _hyena_filters(length, w1, b1, w2, b2, freq, w3):
    t = jnp.arange(length, dtype=F32)
    zf = _pos_features(t, length)
    zb = _pos_features(length - t, length)
    w1p = jnp.pad(w1, ((0, 128 - HY_PE_DIM), (0, 0)))
    w3r = w3.reshape(HY_FILT_HID, HY_ORDER, 2, WD)
    w3bd = _block_diag(w3r[:, :, 0].reshape(HY_FILT_HID, HY_ORDER * WD),
                       w3r[:, :, 1].reshape(HY_FILT_HID, HY_ORDER * WD)).astype(BF)
    w2bd = _block_diag(w2, w2)
    b1d = jnp.concatenate([b1, b1])[None]
    b2d = jnp.concatenate([b2, b2])[None]
    frd = jnp.concatenate([freq, freq], axis=1)
    deltas = jnp.abs(jnp.linspace(math.log(HY_DECAY_TARGET) / HY_SLOW_DECAY,
                                  math.log(HY_DECAY_TARGET) / HY_FAST_DECAY, WD, dtype=F32))[None]
    tm = min(512, length)
    nf = HY_ORDER * WD
    zspec = pl.BlockSpec((tm, 128), lambda i: (i, 0))
    k, ss = pl.pallas_call(
        functools.partial(_filt_kernel, tm=tm), grid=(length // tm,),
        in_specs=[zspec, zspec, _resident(w1p.shape), _resident(b1d.shape), _resident(w2bd.shape),
                  _resident(b2d.shape), _resident(frd.shape), _resident(w3bd.shape),
                  _resident(deltas.shape)],
        out_specs=[pl.BlockSpec((2, tm, nf), lambda i: (0, i, 0)), pl.BlockSpec((1, nf), lambda i: (0, 0))],
        out_shape=[jax.ShapeDtypeStruct((2, length, nf), BF), jax.ShapeDtypeStruct((1, nf), F32)],
        compiler_params=_cparams(1, "arbitrary"), name="hyena_filter")(zf, zb, w1p, b1d, w2bd, b2d, frd, w3bd, deltas)
    return k.reshape(2 * length, nf), ss


def _cis(num, den):
    ang = (-2.0 * math.pi / den) * (num % den).astype(F32)
    return jnp.cos(ang), jnp.sin(ang)


def _stack(re, im):
    return jnp.concatenate([jnp.concatenate([re, -im], axis=1), jnp.concatenate([im, re], axis=1)], axis=0)


def _dft_tables(n1):
    n = n1 * FFT_N2
    i1 = jnp.arange(n1)
    fr, fi = _cis(i1[:, None] * i1[None, :], n1)
    half = n1 // 2
    w_fwd = _stack(fr[:, :half], fi[:, :half]).astype(BF)
    w_real = jnp.concatenate([fr, fi], axis=0).astype(BF)
    w_inv = (_stack(fr[:half, :], -fi[:half, :]) / n).astype(BF)
    i2 = jnp.arange(FFT_N2)
    f2r, f2i = _cis(i2[:, None] * i2[None, :], FFT_N2)
    twr, twi = _cis(i1[:, None] * i2[None, :], n)
    gr = f2r[None] * twr[:, None, :] - f2i[None] * twi[:, None, :]
    gi = f2r[None] * twi[:, None, :] + f2i[None] * twr[:, None, :]
    gs = jnp.concatenate([jnp.concatenate([gr, -gi], axis=2), jnp.concatenate([gi, gr], axis=2)], axis=1)
    gs = gs.astype(BF)
    return w_fwd, w_real, w_inv, gs


def _dft_tables_direct(length):
    n = 2 * length
    i = jnp.arange(n)
    fr, fi = _cis(i[:, None] * i[None, :], n)
    w_fwd = _stack(fr[:, :length], fi[:, :length]).astype(BF)
    w_real = jnp.concatenate([fr, fi], axis=0).astype(BF)
    w_inv = (_stack(fr[:length, :], -fi[:length, :]) / n).astype(BF)
    return w_fwd, w_real, w_inv


def _stage_a_kernel(*refs, nb, n_half, has_epi):
    w_ref, x_ref = refs[0], refs[1]
    m, k = w_ref.shape
    if has_epi:
        g_ref, u_ref, row_ref, o_ref = refs[2:6]
        scratch = refs[6:]
    else:
        o_ref = refs[2]
        scratch = refs[3:]
    xs, os_ = scratch[:n_half], scratch[n_half:]
    w = w_ref[...]
    for l in range(n_half):
        lanes = slice(l * 128, (l + 1) * 128)
        xs[l][...] = x_ref[:, :, lanes].astype(F32).reshape(k * nb, 128)
        for j in range(nb):
            os_[l][pl.ds(j, m, stride=nb), :] = _dot(w, xs[l][pl.ds(j, k, stride=nb), :].astype(BF))
        acc = os_[l][...].reshape(m, nb, 128)
        if has_epi:
            acc = g_ref[:, :, lanes].astype(F32) * (
                acc + u_ref[:, :, lanes].astype(F32) * row_ref[:, lanes].reshape(1, 1, 128))
        o_ref[:, :, lanes] = acc.astype(o_ref.dtype)


def _stage_a(w, x, *, name, epi=None, nb=16, lane_blk=256):
    m, k = w.shape
    c = x.shape[2]
    n_half = lane_blk // 128
    xblk = pl.BlockSpec((k, nb, lane_blk), lambda j, l: (0, j, l))
    oblk = pl.BlockSpec((m, nb, lane_blk), lambda j, l: (0, j, l))
    ops, specs = [w, x], [_resident((m, k)), xblk]
    if epi is not None:
        ops += list(epi)
        specs += [oblk, oblk, pl.BlockSpec((1, lane_blk), lambda j, l: (0, l))]
    scratch = ([pltpu.VMEM((k * nb, 128), F32)] * n_half + [pltpu.VMEM((m * nb, 128), F32)] * n_half)
    return pl.pallas_call(
        functools.partial(_stage_a_kernel, nb=nb, n_half=n_half, has_epi=epi is not None),
        grid=(FFT_N2 // nb, c // lane_blk), in_specs=specs, out_specs=oblk, scratch_shapes=scratch,
        out_shape=jax.ShapeDtypeStruct((m, FFT_N2, c), BF), compiler_params=_cparams(2), name=name)(*ops)


def _stage_b_kernel(a_ref, f_ref, gs_ref, ss_ref, o_ref, *, kb):
    scale = lax.rsqrt(ss_ref[...] + 1e-6)
    yhs = []
    for j in range(kb):
        a = jnp.concatenate([a_ref[0, j], a_ref[1, j]], axis=0)
        f = jnp.concatenate([f_ref[0, j], f_ref[1, j]], axis=0)
        yhs.append(_dot(gs_ref[j], jnp.concatenate([a, f], axis=1)))
    zs = []
    for yh in yhs:
        y, h = yh[:, :WD], yh[:, WD:] * scale
        yr, yi = y[:FFT_N2], y[FFT_N2:]
        hr, hi = h[:FFT_N2], h[FFT_N2:]
        zs.append(jnp.concatenate([yr * hr - yi * hi, yr * hi + yi * hr], axis=0).astype(BF))
    for j in range(kb):
        b = lax.dot_general(gs_ref[j], zs[j], (((0,), (0,)), ((), ())), preferred_element_type=F32)
        o_ref[0, j] = b[:FFT_N2].astype(BF)
        o_ref[1, j] = b[FFT_N2:].astype(BF)


def _stage_b(a, filt_a, gs, ss, *, order, kb=8):
    n1 = a.shape[1]
    kb = min(kb, n1)
    dblk = pl.BlockSpec((2, kb, FFT_N2, WD), lambda i: (0, i, 0, 0))
    fblk = pl.BlockSpec((2, kb, FFT_N2, WD), lambda i: (0, i, 0, order))
    gblk = pl.BlockSpec((kb, 2 * FFT_N2, 2 * FFT_N2), lambda i: (i, 0, 0))
    return pl.pallas_call(
        functools.partial(_stage_b_kernel, kb=kb), grid=(n1 // kb,),
        in_specs=[dblk, fblk, gblk, pl.BlockSpec((1, WD), lambda i: (0, order))], out_specs=dblk,
        out_shape=jax.ShapeDtypeStruct((2, n1, FFT_N2, WD), BF),
        compiler_params=_cparams(1), name="dft_b")(a, filt_a, gs, ss)


def _cmul_kernel(x_ref, h_ref, ss_ref, o_ref, *, half):
    scale = lax.rsqrt(ss_ref[...] + 1e-6)
    xr, xi = x_ref[:half, :], x_ref[half:, :]
    hr, hi = h_ref[:half, :] * scale, h_ref[half:, :] * scale
    o_ref[:half, :] = (xr * hr - xi * hi).astype(BF)
    o_ref[half:, :] = (xr * hi + xi * hr).astype(BF)


def _cmul(x, h, ss, lane_blk):
    rows = x.shape[0]
    return pl.pallas_call(
        functools.partial(_cmul_kernel, half=rows // 2), grid=(1,),
        in_specs=[_resident(x.shape), pl.BlockSpec((rows, WD), lambda i: (0, lane_blk)),
                  pl.BlockSpec((1, WD), lambda i: (0, lane_blk))],
        out_specs=pl.BlockSpec(x.shape, lambda i: (0, 0)),
        out_shape=jax.ShapeDtypeStruct(x.shape, BF), compiler_params=_cparams(1), name="ctx_cmul")(x, h, ss)


def _hyena_long(hv, hx, kern, ss, bias, tables, *, bsz, seq_len):
    assert bsz == 2
    n1 = 2 * seq_len // FFT_N2
    w_fwd, w_real, w_inv, gs = tables
    view = (bsz * seq_len // FFT_N2, FFT_N2, WD)
    filt_a = _stage_a(w_real, kern.reshape(n1, FFT_N2, HY_ORDER * WD), name="dft_a_filter")
    filt_a = filt_a.reshape(2, n1, FFT_N2, HY_ORDER * WD)
    u = hv.reshape(view)
    for o in range(HY_ORDER):
        a = _stage_a(w_fwd, u, name="dft_a_fwd").reshape(2, n1, FFT_N2, WD)
        b = _stage_b(a, filt_a, gs, ss, order=o).reshape(2 * n1, FFT_N2, WD)
        u = _stage_a(w_inv, b, name="dft_a_inv", epi=(hx[o].reshape(view), u, bias[o][None]))
    return u.reshape(bsz * seq_len, WD)


def _hyena_short_seq(hv, hx, kern, ss, bias, tables, *, bsz):
    assert bsz == 2
    w_fwd, w_real, w_inv = tables
    spec = _mm(w_real, kern, bn=HY_ORDER * WD, out_dtype=F32, name="ctx_dft_filter")
    u = hv
    for o in range(HY_ORDER):
        x = _mm(w_fwd, u, bn=WD, out_dtype=F32, name="ctx_dft_fwd")
        z = _cmul(x, spec, ss, o)
        u = _mm(w_inv, z, bn=WD, out_dtype=BF, name="ctx_dft_inv", epi=(hx[o], u, bias[o][None]))
    return u


def _merge_kernel(x_ref, mod_ref, ya_ref, yb_ref, yc_ref, yd_ref, gate_ref, wp_ref, wo_ref, bo_ref,
                  g_ref, b_ref, o_ref, *, alpha):
    x = x_ref[...]
    m = jnp.zeros(x.shape, F32)
    for br, y_ref in enumerate((ya_ref, yb_ref, yc_ref, yd_ref)):
        gate = _sigmoid(gate_ref[:, br * D_MODEL:(br + 1) * D_MODEL])
        m = m + gate.astype(F32) * _dot(y_ref[...], wp_ref[br])
    out = _dot(m.astype(BF), wo_ref[...]) + bo_ref[...]
    y = alpha * x + mod_ref[0][5:6] * out
    o_ref[...] = _ln(y) * g_ref[...] + b_ref[...]


def _merge(x, mod, ya, yb, yc, yd, p, wp, wo, bo, g, b, *, tm, group_tiles, alpha):
    n = x.shape[0]
    yspec = pl.BlockSpec((tm, WD), lambda i: (i, 0))
    return pl.pallas_call(
        functools.partial(_merge_kernel, alpha=alpha), grid=(n // tm,),
        in_specs=[pl.BlockSpec((tm, D_MODEL), lambda i: (i, 0)),
                  pl.BlockSpec((1, N_MOD, D_MODEL), lambda i: (i // group_tiles, 0, 0)),
                  yspec, yspec, yspec, yspec,
                  pl.BlockSpec((tm, N_BRANCH * D_MODEL), lambda i: (i, 0)),
                  _resident(wp.shape), _resident(wo.shape), _resident((1, D_MODEL)),
                  _resident((1, D_MODEL)), _resident((1, D_MODEL))],
        out_specs=pl.BlockSpec((tm, D_MODEL), lambda i: (i, 0)),
        out_shape=jax.ShapeDtypeStruct((n, D_MODEL), F32),
        compiler_params=_cparams(1), name="merge")(x, mod, ya, yb, yc, yd, p, wp, wo, bo, g, b)


def kernel(x, c, ctx, c_ctx, w_mod, b_mod, post_ln_g, post_ln_b, ffn_w_in, ffn_w_out, w_in, b_in,
           conf_dw_w, conf_dw_b, conf_ln_g, conf_ln_b, conf_w_proj, na_rpb, na_w_proj, sc_conv_w,
           sc_w_proj, hy_sconv_w, hy_sconv_b, hy_w1, hy_b1, hy_w2, hy_b2, hy_freq, hy_w3, hy_bias,
           hy_w_proj, w_out, b_out):
    bsz, n_lat, _ = x.shape
    ctx_len = ctx.shape[1]
    depth = w_mod.shape[0]
    alpha = (2 * depth) ** 0.25
    rows = n_lat // GRID_W

    tm = 512
    tmc = min(tm, ctx_len)
    tf = 1024 if n_lat % 1024 == 0 else tm
    tfc = min(tf, bsz * ctx_len)
    tp = 256
    tpc = min(tp, ctx_len)

    xl = x.reshape(bsz * n_lat, D_MODEL)
    xc = ctx.reshape(bsz * ctx_len, D_MODEL)
    lat_tiles = n_lat // tm
    ctx_tiles = bsz * ctx_len // tmc

    tables = _dft_tables(2 * n_lat // FFT_N2)
    tables_c = _dft_tables_direct(ctx_len)

    cond = jnp.concatenate([c, c_ctx[None], jnp.zeros((8 - bsz - 1, D_MODEL), F32)], axis=0)

    ffn_wi = ffn_w_in.astype(BF)
    ffn_wo = ffn_w_out.astype(BF)

    for l in range(depth):
        last = l == depth - 1
        mod_all = _mm(cond, w_mod, bn=1024, out_dtype=F32, name="adaln_mod", bias=b_mod[l][None], a_silu=True,
                      b_layer=l)
        mod_all = mod_all.reshape(8, N_MOD, D_MODEL)
        mod = mod_all[:bsz]
        mod_c = mod_all[bsz:bsz + 1]

        wl = w_in[l]
        w_perm = jnp.concatenate([wl[:, OFF_G:], wl[:, OFF_A:OFF_Q], wl[:, OFF_SB:OFF_G],
                                  wl[:, OFF_Q:OFF_SB]], axis=1).astype(BF)
        bl = b_in[l]
        b_perm = jnp.concatenate([bl[OFF_G:], bl[OFF_A:OFF_Q], bl[OFF_SB:OFF_G], bl[OFF_Q:OFF_SB]])[None]
        ln_g = post_ln_g[l][:, None, :]
        ln_b = post_ln_b[l][:, None, :]
        wp = jnp.stack([conf_w_proj[l], na_w_proj[l], sc_w_proj[l], hy_w_proj[l]]).astype(BF)
        wo = w_out[l].astype(BF)
        conv_args = (conf_dw_w[l], conf_dw_b[l][None], conf_ln_g[l][None], conf_ln_b[l][None],
                     sc_conv_w[l], hy_sconv_w[l], hy_sconv_b[l][None])
        filt_args = (hy_w1[l], hy_b1[l], hy_w2[l], hy_b2[l], hy_freq[l], hy_w3[l])

        xl = _ffn(xl, mod, ffn_wi, ffn_wo, ln_g[0], ln_b[0], layer=l, half=0, m0=0, tm=tf,
                  group_tiles=n_lat // tf, alpha=alpha)
        xc = _ffn(xc, mod_c, ffn_wi, ffn_wo, ln_g[0], ln_b[0], layer=l, half=0, m0=0, tm=tfc,
                  group_tiles=bsz * ctx_len // tfc, alpha=alpha)

        p = _inproj(xl, mod, w_perm, b_perm, tm=tm, group_tiles=lat_tiles)
        pc = _inproj(xc, mod_c, w_perm, b_perm, tm=tmc, group_tiles=ctx_tiles)

        ya, yc, hv, h1, h2 = _prep(p, *conv_args, tm=tp, seq_len=n_lat)
        attn = _na(p, pc, _na_bias_table(na_rpb[l], rows), bsz=bsz, seq_len=n_lat)
        kern, ss = _hyena_filters(n_lat, *filt_args)
        yd = _hyena_long(hv, (h1, h2), kern, ss, hy_bias[l], tables, bsz=bsz, seq_len=n_lat)
        xl = _merge(xl, mod, ya, attn, yc, yd, p, wp, wo, b_out[l][None], ln_g[1], ln_b[1],
                    tm=tm, group_tiles=lat_tiles, alpha=alpha)

        if not last:
            ya, yc, hv, h1, h2 = _prep(pc, *conv_args, tm=tpc, seq_len=ctx_len)
            attn_c = _ctx_attn(pc, bsz=bsz, ctx_len=ctx_len)
            kern_c, ss_c = _hyena_filters(ctx_len, *filt_args)
            yd = _hyena_short_seq(hv, (h1, h2), kern_c, ss_c, hy_bias[l], tables_c, bsz=bsz)
            xc = _merge(xc, mod_c, ya, attn_c, yc, yd, pc, wp, wo, b_out[l][None], ln_g[1], ln_b[1],
                        tm=tmc, group_tiles=ctx_tiles, alpha=alpha)

        xl = _ffn(xl, mod, ffn_wi, ffn_wo, ln_g[2], ln_b[2], layer=l, half=1, m0=6, tm=tf,
                  group_tiles=n_lat // tf, alpha=alpha)
        if not last:
            xc = _ffn(xc, mod_c, ffn_wi, ffn_wo, ln_g[2], ln_b[2], layer=l, half=1, m0=6, tm=tfc,
                      group_tiles=bsz * ctx_len // tfc, alpha=alpha)

    return xl.reshape(bsz, n_lat, D_MODEL)
```

```python
import functools
import math

import jax
import jax.numpy as jnp
import numpy as np
from jax import lax
from jax.experimental import pallas as pl
from jax.experimental.pallas import tpu as pltpu

D_MODEL = 1024
GRID_W = 64
N_BRANCH = 4
WA = 256
CONF_K = 31
NA_HEADS = 4
NA_HEAD_DIM = 64
WB = NA_HEADS * NA_HEAD_DIM
NA_WIN_ROWS = 8
NA_WIN_COLS = 16
ATTN_SCALE = NA_HEAD_DIM ** -0.5
WC = 256
SC_K = 3
WD = 256
HY_ORDER = 2
HY_SHORT_K = 3
HY_PE_BANDS = 16
HY_PE_DIM = 1 + 2 * HY_PE_BANDS
HY_FILT_HID = 64
HY_FAST_DECAY = 0.3
HY_SLOW_DECAY = 1.5
HY_DECAY_TARGET = 1e-2
D_FF = 2816
N_MOD = 9
LN_EPS = 1e-5

OFF_A = 0
OFF_Q = OFF_A + 2 * WA
OFF_K = OFF_Q + WB
OFF_V = OFF_K + WB
OFF_SB = OFF_V + WB
OFF_SC = OFF_SB + WC
OFF_SX = OFF_SC + WC
OFF_HV = OFF_SX + WC
OFF_G = OFF_HV + (1 + HY_ORDER) * WD
P_IN = OFF_G + N_BRANCH * D_MODEL

NEW_G = 0
NEW_CONV = N_BRANCH * D_MODEL
CONV_W = 2 * WA + 3 * WC + 3 * WD
NEW_Q = NEW_CONV + CONV_W

BF = jnp.bfloat16
F32 = jnp.float32

VMEM_LIMIT_BYTES = 58 * 1024 * 1024
FFN_CHUNK = 256
INPROJ_CHUNKS = (1024,) * 4 + (768, 768, 768, 512)
HALO = 16
SHIFT_SPAN = 24
NA_QROWS = 4
NA_TOK = NA_QROWS * GRID_W
FFT_N2 = 128
NEG_INF = -1e30


def _cparams(n_axes, semantics="parallel"):
    return pltpu.CompilerParams(dimension_semantics=(semantics,) * n_axes,
                                vmem_limit_bytes=VMEM_LIMIT_BYTES)


def _resident(shape):
    nd = len(shape)
    return pl.BlockSpec(shape, lambda *_: (0,) * nd, pipeline_mode=pl.Buffered(1))


def _ln(x):
    mu = jnp.mean(x, axis=-1, keepdims=True)
    xc = x - mu
    var = jnp.mean(xc * xc, axis=-1, keepdims=True)
    return xc * lax.rsqrt(var + LN_EPS)


def _sigmoid(x):
    return 0.5 * jnp.tanh(0.5 * x) + 0.5


def _dot(a, b):
    return jnp.dot(a, b, preferred_element_type=F32)


def _dot_nt(a, b):
    return lax.dot_general(a, b, (((1,), (1,)), ((), ())), preferred_element_type=F32)


def _mm_kernel(*refs, a_silu, has_bias, has_epi):
    a_ref, b_ref = refs[0], refs[1]
    pos = 2
    a = a_ref[...]
    if a_silu:
        a = a.astype(F32)
        a = a * _sigmoid(a)
    acc = _dot(a.astype(BF), b_ref[...].astype(BF))
    if has_bias:
        acc = acc + refs[pos][...]
        pos += 1
    if has_epi:
        gate_ref, u_ref, row_ref = refs[pos], refs[pos + 1], refs[pos + 2]
        pos += 3
        acc = gate_ref[...].astype(F32) * (acc + u_ref[...].astype(F32) * row_ref[...])
    o_ref = refs[pos]
    o_ref[...] = acc.astype(o_ref.dtype)


def _mm(a, b, *, bn, out_dtype, name, bias=None, epi=None, a_silu=False, b_layer=None):
    m, k = a.shape
    n = b.shape[-1]
    bn = min(bn, n)
    ops = [a, b]
    if b_layer is None:
        bspec = pl.BlockSpec((k, bn), lambda j: (0, j))
    else:
        bspec = pl.BlockSpec((None, k, bn), lambda j: (b_layer, 0, j))
    specs = [_resident((m, k)), bspec]
    if bias is not None:
        ops.append(bias)
        specs.append(pl.BlockSpec((1, bn), lambda j: (0, j)))
    if epi is not None:
        gate, u, row = epi
        ops += [gate, u, row]
        specs += [pl.BlockSpec((m, bn), lambda j: (0, j)),
                  pl.BlockSpec((m, bn), lambda j: (0, j)),
                  pl.BlockSpec((1, bn), lambda j: (0, j))]
    kern = functools.partial(_mm_kernel, a_silu=a_silu, has_bias=bias is not None,
                             has_epi=epi is not None)
    return pl.pallas_call(
        kern, grid=(n // bn,), in_specs=specs,
        out_specs=pl.BlockSpec((m, bn), lambda j: (0, j)),
        out_shape=jax.ShapeDtypeStruct((m, n), out_dtype),
        compiler_params=_cparams(1), name=name)(*ops)


def _modulated(x, mod, m0):
    shift = mod[m0:m0 + 1]
    scale = mod[m0 + 1:m0 + 2]
    return _ln(x) * (1.0 + scale) + shift


def _ffn_kernel(x_ref, mod_ref, wi_ref, wo_ref, g_ref, b_ref, o_ref, u_ref, *, m0, alpha):
    x = x_ref[...]
    mod = mod_ref[0]
    h = _modulated(x, mod, m0).astype(BF)
    for c in range(D_FF // FFN_CHUNK):
        lo = c * FFN_CHUNK
        a = _dot(h, wi_ref[:, lo:lo + FFN_CHUNK])
        g = _dot(h, wi_ref[:, D_FF + lo:D_FF + lo + FFN_CHUNK])
        u_ref[:, lo:lo + FFN_CHUNK] = (g * _sigmoid(g) * a).astype(BF)
    acc = _dot(u_ref[...], wo_ref[...])
    y = alpha * x + (0.5 * mod[m0 + 2:m0 + 3]) * acc
    o_ref[...] = _ln(y) * g_ref[...] + b_ref[...]


def _ffn(x, mod, w_in, w_out, g, b, *, layer, half, m0, tm, group_tiles, alpha):
    n = x.shape[0]

    def stacked(w):
        return pl.BlockSpec((None, None) + w.shape[2:], lambda i: (layer, half, 0, 0),
                            pipeline_mode=pl.Buffered(1))

    kern = functools.partial(_ffn_kernel, m0=m0, alpha=alpha)
    return pl.pallas_call(
        kern, grid=(n // tm,),
        in_specs=[pl.BlockSpec((tm, D_MODEL), lambda i: (i, 0)),
                  pl.BlockSpec((1, N_MOD, D_MODEL), lambda i: (i // group_tiles, 0, 0)),
                  stacked(w_in), stacked(w_out),
                  _resident((1, D_MODEL)), _resident((1, D_MODEL))],
        out_specs=pl.BlockSpec((tm, D_MODEL), lambda i: (i, 0)),
        out_shape=jax.ShapeDtypeStruct((n, D_MODEL), F32),
        scratch_shapes=[pltpu.VMEM((tm, D_FF), BF)],
        compiler_params=_cparams(1), name="ffn")(x, mod, w_in, w_out, g, b)


def _inproj_kernel(x_ref, mod_ref, w_ref, b_ref, o_ref, *, m0):
    h = _modulated(x_ref[...], mod_ref[0], m0).astype(BF)
    lo = 0
    for width in INPROJ_CHUNKS:
        y = _dot(h, w_ref[:, lo:lo + width]) + b_ref[:, lo:lo + width]
        if lo < NEW_CONV:
            y = _sigmoid(y)
        o_ref[:, lo:lo + width] = y.astype(BF)
        lo += width


def _inproj(x, mod, w, b, *, tm, group_tiles):
    n = x.shape[0]
    return pl.pallas_call(
        functools.partial(_inproj_kernel, m0=3), grid=(n // tm,),
        in_specs=[pl.BlockSpec((tm, D_MODEL), lambda i: (i, 0)),
                  pl.BlockSpec((1, N_MOD, D_MODEL), lambda i: (i // group_tiles, 0, 0)),
                  _resident(w.shape), _resident((1, P_IN))],
        out_specs=pl.BlockSpec((tm, P_IN), lambda i: (i, 0)),
        out_shape=jax.ShapeDtypeStruct((n, P_IN), BF),
        compiler_params=_cparams(1), name="inproj")(x, mod, w, b)


def _prep_kernel(x_ref, prev_ref, next_ref, cw_ref, cb_ref, lg_ref, lb_ref, sw_ref, hw_ref, hb_ref,
                 ya_ref, yc_ref, hv_ref, h1_ref, h2_ref, buf_a, buf_c, buf_h, shifted, *, tm, seq_tiles):
    i = pl.program_id(0)
    pos = i % seq_tiles
    keep_prev = jnp.where(pos == 0, 0.0, 1.0)
    keep_next = jnp.where(pos == seq_tiles - 1, 0.0, 1.0)

    def fill(t, lo, hi):
        glu = t[:, 0:WA] * _sigmoid(t[:, WA:2 * WA])
        buf_a[lo:hi, :] = glu
        buf_c[lo:hi, :] = t[:, 3 * WA:4 * WA] * t[:, 4 * WA:5 * WA]
        buf_h[lo:hi, :] = t[:, 5 * WA:8 * WA]

    main = x_ref[...].astype(F32)
    fill(prev_ref[...].astype(F32) * keep_prev, 0, HALO)
    fill(main, HALO, HALO + tm)
    fill(next_ref[...].astype(F32) * keep_next, HALO + tm, 2 * HALO + tm)

    span = tm + SHIFT_SPAN
    for r in range(8):
        shifted[r] = buf_a[r:r + span, :]
    acc = jnp.zeros((tm, WA), F32) + cb_ref[...]
    for j in range(CONF_K):
        off = HALO - CONF_K // 2 + j
        base = off - off % 8
        acc = acc + cw_ref[j:j + 1, :] * shifted[off % 8, base:base + tm, :]
    u = _ln(acc) * lg_ref[...] + lb_ref[...]
    ya_ref[...] = (u * _sigmoid(u)).astype(BF)

    acc = jnp.zeros((tm, WC), F32)
    for j in range(SC_K):
        off = HALO - SC_K // 2 + j
        acc = acc + sw_ref[j:j + 1, :] * buf_c[off:off + tm, :]
    yc_ref[...] = (main[:, 2 * WA:3 * WA] * acc).astype(BF)

    acc = jnp.zeros((tm, 3 * WD), F32) + hb_ref[...]
    for j in range(HY_SHORT_K):
        off = HALO - HY_SHORT_K // 2 + j
        acc = acc + hw_ref[j:j + 1, :] * buf_h[off:off + tm, :]
    hv_ref[...] = acc[:, 0:WD].astype(BF)
    h1_ref[...] = acc[:, WD:2 * WD].astype(BF)
    h2_ref[...] = acc[:, 2 * WD:3 * WD].astype(BF)


def _prep(p, cw, cb, lg, lb, sw, hw, hb, *, tm, seq_len):
    n = p.shape[0]
    seq_tiles = seq_len // tm
    hb_per_tile = tm // HALO
    n_halo_blocks = n // HALO
    cblk = NEW_CONV // CONV_W
    kern = functools.partial(_prep_kernel, tm=tm, seq_tiles=seq_tiles)
    out_bf = jax.ShapeDtypeStruct((n, WD), BF)
    ospec = pl.BlockSpec((tm, WD), lambda i: (i, 0))
    return pl.pallas_call(
        kern, grid=(n // tm,),
        in_specs=[pl.BlockSpec((tm, CONV_W), lambda i: (i, cblk)),
                  pl.BlockSpec((HALO, CONV_W),
                               lambda i: (jnp.maximum(i * hb_per_tile - 1, 0), cblk)),
                  pl.BlockSpec((HALO, CONV_W),
                               lambda i: (jnp.minimum((i + 1) * hb_per_tile, n_halo_blocks - 1), cblk)),
                  _resident(cw.shape), _resident(cb.shape), _resident(lg.shape), _resident(lb.shape),
                  _resident(sw.shape), _resident(hw.shape), _resident(hb.shape)],
        out_specs=[ospec] * 5, out_shape=[out_bf] * 5,
        scratch_shapes=[pltpu.VMEM((tm + 2 * HALO, WA), F32),
                        pltpu.VMEM((tm + 2 * HALO, WC), F32),
                        pltpu.VMEM((tm + 2 * HALO, 3 * WD), F32),
                        pltpu.VMEM((8, tm + SHIFT_SPAN, WA), F32)],
        compiler_params=_cparams(1), name="prep")(p, p, p, cw, cb, lg, lb, sw, hw, hb)


def _na_kernel(q_ref, k0_ref, k1_ref, k2_ref, v0_ref, v1_ref, v2_ref, kc_ref, vc_ref, bias_ref, o_ref):
    q = q_ref[...] * ATTN_SCALE
    k = jnp.concatenate([k0_ref[...], k1_ref[...], k2_ref[...]], axis=0)
    v = jnp.concatenate([v0_ref[...], v1_ref[...], v2_ref[...]], axis=0)
    kc = kc_ref[...]
    vc = vc_ref[...]
    ones_nb = jnp.ones((v.shape[0], NA_HEAD_DIM), BF)
    ones_cx = jnp.ones((vc.shape[0], NA_HEAD_DIM), BF)

    def scores(h):
        sl = slice(h * NA_HEAD_DIM, (h + 1) * NA_HEAD_DIM)
        qh = q[:, sl]
        return _dot_nt(qh, k[:, sl]) + bias_ref[0, h], _dot_nt(qh, kc[:, sl])

    def weighted(p_nb, p_cx, sl):
        o = (_dot(p_nb, jnp.concatenate([v[:, sl], ones_nb], axis=1))
             + _dot(p_cx, jnp.concatenate([vc[:, sl], ones_cx], axis=1)))
        return o[:, :NA_HEAD_DIM] / o[:, NA_HEAD_DIM:NA_HEAD_DIM + 1]

    outs = []
    sc = scores(0)
    pending = None
    for h in range(NA_HEADS):
        sl = slice(h * NA_HEAD_DIM, (h + 1) * NA_HEAD_DIM)
        s_nb, s_cx = sc
        if h + 1 < NA_HEADS:
            sc = scores(h + 1)
        m = jnp.maximum(jnp.max(s_nb, axis=-1, keepdims=True), jnp.max(s_cx, axis=-1, keepdims=True))
        p_nb = jnp.exp((s_nb - m).astype(BF))
        p_cx = jnp.exp((s_cx - m).astype(BF))
        if pending is not None:
            outs.append(pending())
        pending = functools.partial(weighted, p_nb, p_cx, sl)
    outs.append(pending())
    o_ref[...] = jnp.concatenate(outs, axis=-1).astype(BF)


def _rpb_expand_kernel(rpb_ref, onehot_ref, o_ref):
    o_ref[...] = jnp.dot(rpb_ref[...], onehot_ref[...], precision=lax.Precision.HIGHEST,
                         preferred_element_type=F32)


def _na_bias_table(rpb, rows):
    assert rows >= NA_WIN_ROWS and rows % NA_QROWS == 0 and rows // NA_QROWS >= 3
    n_dr, n_dc = 2 * NA_WIN_ROWS - 1, 2 * NA_WIN_COLS - 1
    cols = np.arange(GRID_W)
    c0 = np.clip(cols - NA_WIN_COLS // 2, 0, GRID_W - NA_WIN_COLS)
    dc = cols[None, :] - cols[:, None] + (NA_WIN_COLS - 1)
    ok_c = (cols[None, :] >= c0[:, None]) & (cols[None, :] < c0[:, None] + NA_WIN_COLS)
    onehot = np.zeros((128, GRID_W, GRID_W), np.float32)
    qq, kk = np.nonzero(ok_c)
    onehot[dc[qq, kk], qq, kk] = 1.0
    rpb2 = jnp.pad(rpb.reshape(NA_HEADS * n_dr, n_dc).astype(F32),
                   ((0, 64 - NA_HEADS * n_dr), (0, 128 - n_dc)))
    t = pl.pallas_call(
        _rpb_expand_kernel, grid=(1,),
        in_specs=[_resident((64, 128)), _resident((128, GRID_W * GRID_W))],
        out_specs=pl.BlockSpec((64, GRID_W * GRID_W), lambda i: (0, 0)),
        out_shape=jax.ShapeDtypeStruct((64, GRID_W * GRID_W), F32),
        compiler_params=_cparams(1), name="rpb_expand")(rpb2, jnp.asarray(onehot.reshape(128, GRID_W * GRID_W)))
    t = t[:NA_HEADS * n_dr].reshape(NA_HEADS, n_dr, GRID_W, GRID_W)
    full = jnp.concatenate(
        [jnp.concatenate([t[:, s - a + NA_WIN_ROWS - 1 - NA_QROWS] for s in range(3 * NA_QROWS)], axis=-1)
         for a in range(NA_QROWS)], axis=1)
    wr = NA_WIN_ROWS
    n_blk = rows // NA_QROWS
    tabs = []
    for blk in (0, 1, n_blk - 1):
        qr = blk * NA_QROWS + np.arange(NA_QROWS)
        kr = (blk - 1) * NA_QROWS + np.arange(3 * NA_QROWS)
        r0 = np.clip(qr - wr // 2, 0, rows - wr)
        ok_r = ((kr[None, :] >= r0[:, None]) & (kr[None, :] < r0[:, None] + wr)
                & (kr[None, :] >= 0) & (kr[None, :] < rows))
        ok = (ok_r[:, None, :, None] & ok_c[None, :, None, :]).reshape(NA_TOK, 3 * NA_TOK)
        tabs.append(jnp.where(jnp.asarray(ok)[None], full, NEG_INF))
    return jnp.stack(tabs)


def _na(p, pc, bias_tab, *, bsz, seq_len):
    t = seq_len // NA_TOK
    qb, kb, vb = NEW_Q // WB, NEW_Q // WB + 1, NEW_Q // WB + 2
    blk = (NA_TOK, WB)
    cblk = (pc.shape[0] // bsz, WB)

    def nbr(j, col):
        return pl.BlockSpec(blk, lambda b, i: (b * t + jnp.clip(i - 1 + j, 0, t - 1), col))

    return pl.pallas_call(
        _na_kernel, grid=(bsz, t),
        in_specs=[pl.BlockSpec(blk, lambda b, i: (b * t + i, qb)),
                  nbr(0, kb), nbr(1, kb), nbr(2, kb), nbr(0, vb), nbr(1, vb), nbr(2, vb),
                  pl.BlockSpec(cblk, lambda b, i: (b, kb)),
                  pl.BlockSpec(cblk, lambda b, i: (b, vb)),
                  pl.BlockSpec((1, NA_HEADS, NA_TOK, 3 * NA_TOK),
                               lambda b, i: (jnp.where(i == 0, 0, jnp.where(i == t - 1, 2, 1)), 0, 0, 0))],
        out_specs=pl.BlockSpec(blk, lambda b, i: (b * t + i, 0)),
        out_shape=jax.ShapeDtypeStruct((bsz * seq_len, WB), BF),
        compiler_params=_cparams(2), name="na")(p, p, p, p, p, p, p, pc, pc, bias_tab)


def _ctx_attn_kernel(q_ref, k_ref, v_ref, o_ref):
    q = q_ref[...] * ATTN_SCALE
    k = k_ref[...]
    v = v_ref[...]
    outs = []
    for h in range(NA_HEADS):
        sl = slice(h * NA_HEAD_DIM, (h + 1) * NA_HEAD_DIM)
        s = _dot_nt(q[:, sl], k[:, sl])
        m = jnp.max(s, axis=-1, keepdims=True)
        p = jnp.exp(s - m)
        den = jnp.sum(p, axis=-1, keepdims=True)
        outs.append(_dot(p.astype(BF), v[:, sl]) / den)
    o_ref[...] = jnp.concatenate(outs, axis=-1).astype(BF)


def _ctx_attn(pc, *, bsz, ctx_len):
    qb, kb, vb = NEW_Q // WB, NEW_Q // WB + 1, NEW_Q // WB + 2
    blk = (ctx_len, WB)
    return pl.pallas_call(
        _ctx_attn_kernel, grid=(bsz,),
        in_specs=[pl.BlockSpec(blk, lambda b: (b, qb)), pl.BlockSpec(blk, lambda b: (b, kb)),
                  pl.BlockSpec(blk, lambda b: (b, vb))],
        out_specs=pl.BlockSpec(blk, lambda b: (b, 0)),
        out_shape=jax.ShapeDtypeStruct((bsz * ctx_len, WB), BF),
        compiler_params=_cparams(1), name="ctx_attn")(pc, pc, pc)


def _filt_kernel(zf_ref, zb_ref, w1_ref, b1_ref, w2_ref, b2_ref, fr_ref, w3_ref, dl_ref, k_ref, ss_ref,
                 *, tm):
    i = pl.program_id(0)
    hp = lax.Precision.HIGHEST
    zf = zf_ref[...]
    zb = zb_ref[...]
    w1 = w1_ref[...]
    pre = jnp.concatenate([jnp.dot(zf, w1, precision=hp, preferred_element_type=F32),
                           jnp.dot(zb, w1, precision=hp, preferred_element_type=F32)], axis=-1)
    h = jnp.sin(fr_ref[0:1, :] * (pre + b1_ref[...]))
    h = jnp.sin(fr_ref[1:2, :] * (jnp.dot(h, w2_ref[...], precision=hp, preferred_element_type=F32)
                                  + b2_ref[...]))
    k = _dot(h.astype(BF), w3_ref[...])
    dec_f = jnp.exp(-zf[:, 0:1] * dl_ref[...])
    dec_b = jnp.exp(-zb[:, 0:1] * dl_ref[...])
    k = k * jnp.concatenate([dec_f] * HY_ORDER + [dec_b] * HY_ORDER, axis=-1)
    is_row0 = (i * tm + lax.broadcasted_iota(jnp.int32, (tm, 1), 0)) == 0
    kf = k[:, :HY_ORDER * WD]
    kb = jnp.where(is_row0, 0.0, k[:, HY_ORDER * WD:])
    k_ref[0] = kf.astype(BF)
    k_ref[1] = kb.astype(BF)

    @pl.when(i == 0)
    def _():
        ss_ref[...] = jnp.zeros_like(ss_ref)

    ss_ref[...] += jnp.sum(kf * kf + kb * kb, axis=0, keepdims=True)


def _pos_features(t, length):
    t_norm = t / max(length - 1, 1)
    bands = jnp.linspace(1e-4, HY_PE_BANDS - 1, HY_PE_BANDS, dtype=F32)
    ang = (2.0 * math.pi / length) * t[:, None] * bands[None, :]
    z = jnp.concatenate([t_norm[:, None], jnp.cos(ang), -jnp.sin(ang)], axis=-1)
    return jnp.pad(z, ((0, 0), (0, 128 - HY_PE_DIM)))


def _block_diag(a, b):
    return jnp.concatenate([jnp.concatenate([a, jnp.zeros((a.shape[0], b.shape[1]), a.dtype)], axis=1),
                            jnp.concatenate([jnp.zeros((b.shape[0], a.shape[1]), a.dtype), b], axis=1)],
                           axis=0)


def _hyena_filters(length, w1, b1, w2, b2, freq, w3):
    t = jnp.arange(length, dtype=F32)
    zf = _pos_features(t, length)
    zb = _pos_features(length - t, length)
    w1p = jnp.pad(w1, ((0, 128 - HY_PE_DIM), (0, 0)))
    w3r = w3.reshape(HY_FILT_HID, HY_ORDER, 2, WD)
    w3bd = _block_diag(w3r[:, :, 0].reshape(HY_FILT_HID, HY_ORDER * WD),
                       w3r[:, :, 1].reshape(HY_FILT_HID, HY_ORDER * WD)).astype(BF)
    w2bd = _block_diag(w2, w2)
    b1d = jnp.concatenate([b1, b1])[None]
    b2d = jnp.concatenate([b2, b2])[None]
    frd = jnp.concatenate([freq, freq], axis=1)
    deltas = jnp.abs(jnp.linspace(math.log(HY_DECAY_TARGET) / HY_SLOW_DECAY,
                                  math.log(HY_DECAY_TARGET) / HY_FAST_DECAY, WD, dtype=F32))[None]
    tm = min(512, length)
    nf = HY_ORDER * WD
    zspec = pl.BlockSpec((tm, 128), lambda i: (i, 0))
    k, ss = pl.pallas_call(
        functools.partial(_filt_kernel, tm=tm), grid=(length // tm,),
        in_specs=[zspec, zspec, _resident(w1p.shape), _resident(b1d.shape), _resident(w2bd.shape),
                  _resident(b2d.shape), _resident(frd.shape), _resident(w3bd.shape),
                  _resident(deltas.shape)],
        out_specs=[pl.BlockSpec((2, tm, nf), lambda i: (0, i, 0)), pl.BlockSpec((1, nf), lambda i: (0, 0))],
        out_shape=[jax.ShapeDtypeStruct((2, length, nf), BF), jax.ShapeDtypeStruct((1, nf), F32)],
        compiler_params=_cparams(1, "arbitrary"), name="hyena_filter")(zf, zb, w1p, b1d, w2bd, b2d, frd, w3bd, deltas)
    return k.reshape(2 * length, nf), ss


def _cis(num, den):
    ang = (-2.0 * math.pi / den) * (num % den).astype(F32)
    return jnp.cos(ang), jnp.sin(ang)


def _stack(re, im):
    return jnp.concatenate([jnp.concatenate([re, -im], axis=1), jnp.concatenate([im, re], axis=1)], axis=0)


def _dft_tables(n1):
    n = n1 * FFT_N2
    i1 = jnp.arange(n1)
    fr, fi = _cis(i1[:, None] * i1[None, :], n1)
    half = n1 // 2
    w_fwd = _stack(fr[:, :half], fi[:, :half]).astype(BF)
    w_real = jnp.concatenate([fr, fi], axis=0).astype(BF)
    w_inv = (_stack(fr[:half, :], -fi[:half, :]) / n).astype(BF)
    i2 = jnp.arange(FFT_N2)
    f2r, f2i = _cis(i2[:, None] * i2[None, :], FFT_N2)
    twr, twi = _cis(i1[:, None] * i2[None, :], n)
    gr = f2r[None] * twr[:, None, :] - f2i[None] * twi[:, None, :]
    gi = f2r[None] * twi[:, None, :] + f2i[None] * twr[:, None, :]
    gs = jnp.concatenate([jnp.concatenate([gr, -gi], axis=2), jnp.concatenate([gi, gr], axis=2)], axis=1)
    gs = gs.astype(BF)
    return w_fwd, w_real, w_inv, gs


def _dft_tables_direct(length):
    n = 2 * length
    i = jnp.arange(n)
    fr, fi = _cis(i[:, None] * i[None, :], n)
    w_fwd = _stack(fr[:, :length], fi[:, :length]).astype(BF)
    w_real = jnp.concatenate([fr, fi], axis=0).astype(BF)
    w_inv = (_stack(fr[:length, :], -fi[:length, :]) / n).astype(BF)
    return w_fwd, w_real, w_inv


def _stage_a_kernel(*refs, nb, n_half, has_epi):
    w_ref, x_ref = refs[0], refs[1]
    m, k = w_ref.shape
    if has_epi:
        g_ref, u_ref, row_ref, o_ref = refs[2:6]
        scratch = refs[6:]
    else:
        o_ref = refs[2]
        scratch = refs[3:]
    xs, os_ = scratch[:n_half], scratch[n_half:]
    w = w_ref[...]
    for l in range(n_half):
        lanes = slice(l * 128, (l + 1) * 128)
        xs[l][...] = x_ref[:, :, lanes].astype(F32).reshape(k * nb, 128)
        for j in range(nb):
            os_[l][pl.ds(j, m, stride=nb), :] = _dot(w, xs[l][pl.ds(j, k, stride=nb), :].astype(BF))
        acc = os_[l][...].reshape(m, nb, 128)
        if has_epi:
            acc = g_ref[:, :, lanes].astype(F32) * (
                acc + u_ref[:, :, lanes].astype(F32) * row_ref[:, lanes].reshape(1, 1, 128))
        o_ref[:, :, lanes] = acc.astype(o_ref.dtype)


def _stage_a(w, x, *, name, epi=None, nb=16, lane_blk=256):
    m, k = w.shape
    c = x.shape[2]
    n_half = lane_blk // 128
    xblk = pl.BlockSpec((k, nb, lane_blk), lambda j, l: (0, j, l))
    oblk = pl.BlockSpec((m, nb, lane_blk), lambda j, l: (0, j, l))
    ops, specs = [w, x], [_resident((m, k)), xblk]
    if epi is not None:
        ops += list(epi)
        specs += [oblk, oblk, pl.BlockSpec((1, lane_blk), lambda j, l: (0, l))]
    scratch = ([pltpu.VMEM((k * nb, 128), F32)] * n_half + [pltpu.VMEM((m * nb, 128), F32)] * n_half)
    return pl.pallas_call(
        functools.partial(_stage_a_kernel, nb=nb, n_half=n_half, has_epi=epi is not None),
        grid=(FFT_N2 // nb, c // lane_blk), in_specs=specs, out_specs=oblk, scratch_shapes=scratch,
        out_shape=jax.ShapeDtypeStruct((m, FFT_N2, c), BF), compiler_params=_cparams(2), name=name)(*ops)


def _stage_b_kernel(a_ref, f_ref, gs_ref, ss_ref, o_ref, *, kb):
    scale = lax.rsqrt(ss_ref[...] + 1e-6)
    yhs = []
    for j in range(kb):
        a = jnp.concatenate([a_ref[0, j], a_ref[1, j]], axis=0)
        f = jnp.concatenate([f_ref[0, j], f_ref[1, j]], axis=0)
        yhs.append(_dot(gs_ref[j], jnp.concatenate([a, f], axis=1)))
    zs = []
    for yh in yhs:
        y, h = yh[:, :WD], yh[:, WD:] * scale
        yr, yi = y[:FFT_N2], y[FFT_N2:]
        hr, hi = h[:FFT_N2], h[FFT_N2:]
        zs.append(jnp.concatenate([yr * hr - yi * hi, yr * hi + yi * hr], axis=0).astype(BF))
    for j in range(kb):
        b = lax.dot_general(gs_ref[j], zs[j], (((0,), (0,)), ((), ())), preferred_element_type=F32)
        o_ref[0, j] = b[:FFT_N2].astype(BF)
        o_ref[1, j] = b[FFT_N2:].astype(BF)


def _stage_b(a, filt_a, gs, ss, *, order, kb=16):
    n1 = a.shape[1]
    kb = min(kb, n1)
    dblk = pl.BlockSpec((2, kb, FFT_N2, WD), lambda i: (0, i, 0, 0))
    fblk = pl.BlockSpec((2, kb, FFT_N2, WD), lambda i: (0, i, 0, order))
    gblk = pl.BlockSpec((kb, 2 * FFT_N2, 2 * FFT_N2), lambda i: (i, 0, 0))
    return pl.pallas_call(
        functools.partial(_stage_b_kernel, kb=kb), grid=(n1 // kb,),
        in_specs=[dblk, fblk, gblk, pl.BlockSpec((1, WD), lambda i: (0, order))], out_specs=dblk,
        out_shape=jax.ShapeDtypeStruct((2, n1, FFT_N2, WD), BF),
        compiler_params=_cparams(1), name="dft_b")(a, filt_a, gs, ss)


def _cmul_kernel(x_ref, h_ref, ss_ref, o_ref, *, half):
    scale = lax.rsqrt(ss_ref[...] + 1e-6)
    xr, xi = x_ref[:half, :], x_ref[half:, :]
    hr, hi = h_ref[:half, :] * scale, h_ref[half:, :] * scale
    o_ref[:half, :] = (xr * hr - xi * hi).astype(BF)
    o_ref[half:, :] = (xr * hi + xi * hr).astype(BF)


def _cmul(x, h, ss, lane_blk):
    rows = x.shape[0]
    return pl.pallas_call(
        functools.partial(_cmul_kernel, half=rows // 2), grid=(1,),
        in_specs=[_resident(x.shape), pl.BlockSpec((rows, WD), lambda i: (0, lane_blk)),
                  pl.BlockSpec((1, WD), lambda i: (0, lane_blk))],
        out_specs=pl.BlockSpec(x.shape, lambda i: (0, 0)),
        out_shape=jax.ShapeDtypeStruct(x.shape, BF), compiler_params=_cparams(1), name="ctx_cmul")(x, h, ss)


def _hyena_long(hv, hx, kern, ss, bias, tables, *, bsz, seq_len):
    assert bsz == 2
    n1 = 2 * seq_len // FFT_N2
    w_fwd, w_real, w_inv, gs = tables
    view = (bsz * seq_len // FFT_N2, FFT_N2, WD)
    filt_a = _stage_a(w_real, kern.reshape(n1, FFT_N2, HY_ORDER * WD), name="dft_a_filter")
    filt_a = filt_a.reshape(2, n1, FFT_N2, HY_ORDER * WD)
    u = hv.reshape(view)
    for o in range(HY_ORDER):
        a = _stage_a(w_fwd, u, name="dft_a_fwd").reshape(2, n1, FFT_N2, WD)
        b = _stage_b(a, filt_a, gs, ss, order=o).reshape(2 * n1, FFT_N2, WD)
        u = _stage_a(w_inv, b, name="dft_a_inv", epi=(hx[o].reshape(view), u, bias[o][None]))
    return u.reshape(bsz * seq_len, WD)


def _hyena_short_seq(hv, hx, kern, ss, bias, tables, *, bsz):
    assert bsz == 2
    w_fwd, w_real, w_inv = tables
    spec = _mm(w_real, kern, bn=HY_ORDER * WD, out_dtype=F32, name="ctx_dft_filter")
    u = hv
    for o in range(HY_ORDER):
        x = _mm(w_fwd, u, bn=WD, out_dtype=F32, name="ctx_dft_fwd")
        z = _cmul(x, spec, ss, o)
        u = _mm(w_inv, z, bn=WD, out_dtype=BF, name="ctx_dft_inv", epi=(hx[o], u, bias[o][None]))
    return u


def _merge_kernel(x_ref, mod_ref, ya_ref, yb_ref, yc_ref, yd_ref, gate_ref, wp_ref, wo_ref, bo_ref,
                  g_ref, b_ref, o_ref, *, alpha):
    x = x_ref[...]
    m = jnp.zeros(x.shape, F32)
    for br, y_ref in enumerate((ya_ref, yb_ref, yc_ref, yd_ref)):
        gate = gate_ref[:, br * D_MODEL:(br + 1) * D_MODEL].astype(F32)
        m = m + gate * _dot(y_ref[...], wp_ref[br])
    out = _dot(m.astype(BF), wo_ref[...]) + bo_ref[...]
    y = alpha * x + mod_ref[0][5:6] * out
    o_ref[...] = _ln(y) * g_ref[...] + b_ref[...]


def _merge(x, mod, ya, yb, yc, yd, p, wp, wo, bo, g, b, *, tm, group_tiles, alpha):
    n = x.shape[0]
    yspec = pl.BlockSpec((tm, WD), lambda i: (i, 0))
    return pl.pallas_call(
        functools.partial(_merge_kernel, alpha=alpha), grid=(n // tm,),
        in_specs=[pl.BlockSpec((tm, D_MODEL), lambda i: (i, 0)),
                  pl.BlockSpec((1, N_MOD, D_MODEL), lambda i: (i // group_tiles, 0, 0)),
                  yspec, yspec, yspec, yspec,
                  pl.BlockSpec((tm, N_BRANCH * D_MODEL), lambda i: (i, 0)),
                  _resident(wp.shape), _resident(wo.shape), _resident((1, D_MODEL)),
                  _resident((1, D_MODEL)), _resident((1, D_MODEL))],
        out_specs=pl.BlockSpec((tm, D_MODEL), lambda i: (i, 0)),
        out_shape=jax.ShapeDtypeStruct((n, D_MODEL), F32),
        compiler_params=_cparams(1), name="merge")(x, mod, ya, yb, yc, yd, p, wp, wo, bo, g, b)


def kernel(x, c, ctx, c_ctx, w_mod, b_mod, post_ln_g, post_ln_b, ffn_w_in, ffn_w_out, w_in, b_in,
           conf_dw_w, conf_dw_b, conf_ln_g, conf_ln_b, conf_w_proj, na_rpb, na_w_proj, sc_conv_w,
           sc_w_proj, hy_sconv_w, hy_sconv_b, hy_w1, hy_b1, hy_w2, hy_b2, hy_freq, hy_w3, hy_bias,
           hy_w_proj, w_out, b_out):
    bsz, n_lat, _ = x.shape
    ctx_len = ctx.shape[1]
    depth = w_mod.shape[0]
    alpha = (2 * depth) ** 0.25
    rows = n_lat // GRID_W

    tm = 512
    tmc = min(tm, ctx_len)
    tf = 1024 if n_lat % 1024 == 0 else tm
    tfc = min(tf, bsz * ctx_len)
    tp = 256
    tpc = min(tp, ctx_len)

    xl = x.reshape(bsz * n_lat, D_MODEL)
    xc = ctx.reshape(bsz * ctx_len, D_MODEL)
    lat_tiles = n_lat // tm
    ctx_tiles = bsz * ctx_len // tmc

    tables = _dft_tables(2 * n_lat // FFT_N2)
    tables_c = _dft_tables_direct(ctx_len)

    cond = jnp.concatenate([c, c_ctx[None], jnp.zeros((8 - bsz - 1, D_MODEL), F32)], axis=0)

    ffn_wi = ffn_w_in.astype(BF)
    ffn_wo = ffn_w_out.astype(BF)

    for l in range(depth):
        last = l == depth - 1
        mod_all = _mm(cond, w_mod, bn=1024, out_dtype=F32, name="adaln_mod", bias=b_mod[l][None], a_silu=True,
                      b_layer=l)
        mod_all = mod_all.reshape(8, N_MOD, D_MODEL)
        mod = mod_all[:bsz]
        mod_c = mod_all[bsz:bsz + 1]

        wl = w_in[l]
        w_perm = jnp.concatenate([wl[:, OFF_G:], wl[:, OFF_A:OFF_Q], wl[:, OFF_SB:OFF_G],
                                  wl[:, OFF_Q:OFF_SB]], axis=1).astype(BF)
        bl = b_in[l]
        b_perm = jnp.concatenate([bl[OFF_G:], bl[OFF_A:OFF_Q], bl[OFF_SB:OFF_G], bl[OFF_Q:OFF_SB]])[None]
        ln_g = post_ln_g[l][:, None, :]
        ln_b = post_ln_b[l][:, None, :]
        wp = jnp.stack([conf_w_proj[l], na_w_proj[l], sc_w_proj[l], hy_w_proj[l]]).astype(BF)
        wo = w_out[l].astype(BF)
        conv_args = (conf_dw_w[l], conf_dw_b[l][None], conf_ln_g[l][None], conf_ln_b[l][None],
                     sc_conv_w[l], hy_sconv_w[l], hy_sconv_b[l][None])
        filt_args = (hy_w1[l], hy_b1[l], hy_w2[l], hy_b2[l], hy_freq[l], hy_w3[l])

        xl = _ffn(xl, mod, ffn_wi, ffn_wo, ln_g[0], ln_b[0], layer=l, half=0, m0=0, tm=tf,
                  group_tiles=n_lat // tf, alpha=alpha)
        xc = _ffn(xc, mod_c, ffn_wi, ffn_wo, ln_g[0], ln_b[0], layer=l, half=0, m0=0, tm=tfc,
                  group_tiles=bsz * ctx_len // tfc, alpha=alpha)

        p = _inproj(xl, mod, w_perm, b_perm, tm=tm, group_tiles=lat_tiles)
        pc = _inproj(xc, mod_c, w_perm, b_perm, tm=tmc, group_tiles=ctx_tiles)

        ya, yc, hv, h1, h2 = _prep(p, *conv_args, tm=tp, seq_len=n_lat)
        attn = _na(p, pc, _na_bias_table(na_rpb[l], rows), bsz=bsz, seq_len=n_lat)
        kern, ss = _hyena_filters(n_lat, *filt_args)
        yd = _hyena_long(hv, (h1, h2), kern, ss, hy_bias[l], tables, bsz=bsz, seq_len=n_lat)
        xl = _merge(xl, mod, ya, attn, yc, yd, p, wp, wo, b_out[l][None], ln_g[1], ln_b[1],
                    tm=tm, group_tiles=lat_tiles, alpha=alpha)

        if not last:
            ya, yc, hv, h1, h2 = _prep(pc, *conv_args, tm=tpc, seq_len=ctx_len)
            attn_c = _ctx_attn(pc, bsz=bsz, ctx_len=ctx_len)
            kern_c, ss_c = _hyena_filters(ctx_len, *filt_args)
            yd = _hyena_short_seq(hv, (h1, h2), kern_c, ss_c, hy_bias[l], tables_c, bsz=bsz)
            xc = _merge(xc, mod_c, ya, attn_c, yc, yd, pc, wp, wo, b_out[l][None], ln_g[1], ln_b[1],
                        tm=tmc, group_tiles=ctx_tiles, alpha=alpha)

        xl = _ffn(xl, mod, ffn_wi, ffn_wo, ln_g[2], ln_b[2], layer=l, half=1, m0=6, tm=tf,
                  group_tiles=n_lat // tf, alpha=alpha)
        if not last:
            xc = _ffn(xc, mod_c, ffn_wi, ffn_wo, ln_g[2], ln_b[2], layer=l, half=1, m0=6, tm=tfc,
                      group_tiles=bsz * ctx_len // tfc, alpha=alpha)

    return xl.reshape(bsz, n_lat, D_MODEL)
```

```python
import functools
import math

import jax
import jax.numpy as jnp
import numpy as np
from jax import lax
from jax.experimental import pallas as pl
from jax.experimental.pallas import tpu as pltpu

D_MODEL = 1024
GRID_W = 64
N_BRANCH = 4
WA = 256
CONF_K = 31
NA_HEADS = 4
NA_HEAD_DIM = 64
WB = NA_HEADS * NA_HEAD_DIM
NA_WIN_ROWS = 8
NA_WIN_COLS = 16
ATTN_SCALE = NA_HEAD_DIM ** -0.5
WC = 256
SC_K = 3
WD = 256
HY_ORDER = 2
HY_SHORT_K = 3
HY_PE_BANDS = 16
HY_PE_DIM = 1 + 2 * HY_PE_BANDS
HY_FILT_HID = 64
HY_FAST_DECAY = 0.3
HY_SLOW_DECAY = 1.5
HY_DECAY_TARGET = 1e-2
D_FF = 2816
N_MOD = 9
LN_EPS = 1e-5

OFF_A = 0
OFF_Q = OFF_A + 2 * WA
OFF_K = OFF_Q + WB
OFF_V = OFF_K + WB
OFF_SB = OFF_V + WB
OFF_SC = OFF_SB + WC
OFF_SX = OFF_SC + WC
OFF_HV = OFF_SX + WC
OFF_G = OFF_HV + (1 + HY_ORDER) * WD
P_IN = OFF_G + N_BRANCH * D_MODEL

NEW_G = 0
NEW_CONV = N_BRANCH * D_MODEL
CONV_W = 2 * WA + 3 * WC + 3 * WD
NEW_Q = NEW_CONV + CONV_W

BF = jnp.bfloat16
F32 = jnp.float32

LANES = 128
SUBLANES = 8
VMEM_BYTES = 64 * 1024 * 1024
VMEM_LIMIT_BYTES = VMEM_BYTES - 6 * 1024 * 1024
FFN_CHUNK = 256
INPROJ_CHUNKS = (1024,) * 4 + (768, 768, 768, 512)
HALO = 16
SHIFT_SPAN = 24
NA_QROWS = 4
NA_TOK = NA_QROWS * GRID_W
FFT_N2 = 128
NEG_INF = -1e30


def _cparams(n_axes, semantics="parallel"):
    return pltpu.CompilerParams(dimension_semantics=(semantics,) * n_axes,
                                vmem_limit_bytes=VMEM_LIMIT_BYTES)


def _resident(shape):
    nd = len(shape)
    return pl.BlockSpec(shape, lambda *_: (0,) * nd, pipeline_mode=pl.Buffered(1))


def _ln(x):
    mu = jnp.mean(x, axis=-1, keepdims=True)
    xc = x - mu
    var = jnp.mean(xc * xc, axis=-1, keepdims=True)
    return xc * lax.rsqrt(var + LN_EPS)


def _sigmoid(x):
    return 0.5 * jnp.tanh(0.5 * x) + 0.5


def _dot(a, b):
    return jnp.dot(a, b, preferred_element_type=F32)


def _dot_nt(a, b):
    return lax.dot_general(a, b, (((1,), (1,)), ((), ())), preferred_element_type=F32)


def _mm_kernel(*refs, a_silu, has_bias, has_epi):
    a_ref, b_ref = refs[0], refs[1]
    pos = 2
    a = a_ref[...]
    if a_silu:
        a = a.astype(F32)
        a = a * _sigmoid(a)
    acc = _dot(a.astype(BF), b_ref[...].astype(BF))
    if has_bias:
        acc = acc + refs[pos][...]
        pos += 1
    if has_epi:
        gate_ref, u_ref, row_ref = refs[pos], refs[pos + 1], refs[pos + 2]
        pos += 3
        acc = gate_ref[...].astype(F32) * (acc + u_ref[...].astype(F32) * row_ref[...])
    o_ref = refs[pos]
    o_ref[...] = acc.astype(o_ref.dtype)


def _mm(a, b, *, bn, out_dtype, name, bias=None, epi=None, a_silu=False, b_layer=None):
    m, k = a.shape
    n = b.shape[-1]
    bn = min(bn, n)
    ops = [a, b]
    if b_layer is None:
        bspec = pl.BlockSpec((k, bn), lambda j: (0, j))
    else:
        bspec = pl.BlockSpec((None, k, bn), lambda j: (b_layer, 0, j))
    specs = [_resident((m, k)), bspec]
    if bias is not None:
        ops.append(bias)
        specs.append(pl.BlockSpec((1, bn), lambda j: (0, j)))
    if epi is not None:
        gate, u, row = epi
        ops += [gate, u, row]
        specs += [pl.BlockSpec((m, bn), lambda j: (0, j)),
                  pl.BlockSpec((m, bn), lambda j: (0, j)),
                  pl.BlockSpec((1, bn), lambda j: (0, j))]
    kern = functools.partial(_mm_kernel, a_silu=a_silu, has_bias=bias is not None,
                             has_epi=epi is not None)
    return pl.pallas_call(
        kern, grid=(n // bn,), in_specs=specs,
        out_specs=pl.BlockSpec((m, bn), lambda j: (0, j)),
        out_shape=jax.ShapeDtypeStruct((m, n), out_dtype),
        compiler_params=_cparams(1), name=name)(*ops)


def _modulated(x, mod, m0):
    shift = mod[m0:m0 + 1]
    scale = mod[m0 + 1:m0 + 2]
    return _ln(x) * (1.0 + scale) + shift


def _ffn_kernel(x_ref, mod_ref, wi_ref, wo_ref, g_ref, b_ref, o_ref, u_ref, *, m0, alpha):
    x = x_ref[...]
    mod = mod_ref[0]
    h = _modulated(x, mod, m0).astype(BF)
    for c in range(D_FF // FFN_CHUNK):
        lo = c * FFN_CHUNK
        a = _dot(h, wi_ref[:, lo:lo + FFN_CHUNK])
        g = _dot(h, wi_ref[:, D_FF + lo:D_FF + lo + FFN_CHUNK])
        u_ref[:, lo:lo + FFN_CHUNK] = (g * _sigmoid(g) * a).astype(BF)
    acc = _dot(u_ref[...], wo_ref[...])
    y = alpha * x + (0.5 * mod[m0 + 2:m0 + 3]) * acc
    o_ref[...] = _ln(y) * g_ref[...] + b_ref[...]


def _ffn(x, mod, w_in, w_out, g, b, *, layer, half, m0, tm, group_tiles, alpha):
    n = x.shape[0]

    def stacked(w):
        return pl.BlockSpec((None, None) + w.shape[2:], lambda i: (layer, half, 0, 0),
                            pipeline_mode=pl.Buffered(1))

    kern = functools.partial(_ffn_kernel, m0=m0, alpha=alpha)
    return pl.pallas_call(
        kern, grid=(n // tm,),
        in_specs=[pl.BlockSpec((tm, D_MODEL), lambda i: (i, 0)),
                  pl.BlockSpec((1, N_MOD, D_MODEL), lambda i: (i // group_tiles, 0, 0)),
                  stacked(w_in), stacked(w_out),
                  _resident((1, D_MODEL)), _resident((1, D_MODEL))],
        out_specs=pl.BlockSpec((tm, D_MODEL), lambda i: (i, 0)),
        out_shape=jax.ShapeDtypeStruct((n, D_MODEL), F32),
        scratch_shapes=[pltpu.VMEM((tm, D_FF), BF)],
        compiler_params=_cparams(1), name="ffn")(x, mod, w_in, w_out, g, b)


def _inproj_kernel(x_ref, mod_ref, w_ref, b_ref, o_ref, *, m0):
    h = _modulated(x_ref[...], mod_ref[0], m0).astype(BF)
    lo = 0
    for width in INPROJ_CHUNKS:
        y = _dot(h, w_ref[:, lo:lo + width]) + b_ref[:, lo:lo + width]
        if lo < NEW_CONV:
            y = _sigmoid(y)
        o_ref[:, lo:lo + width] = y.astype(BF)
        lo += width


def _inproj(x, mod, w, b, *, tm, group_tiles):
    n = x.shape[0]
    return pl.pallas_call(
        functools.partial(_inproj_kernel, m0=3), grid=(n // tm,),
        in_specs=[pl.BlockSpec((tm, D_MODEL), lambda i: (i, 0)),
                  pl.BlockSpec((1, N_MOD, D_MODEL), lambda i: (i // group_tiles, 0, 0)),
                  _resident(w.shape), _resident((1, P_IN))],
        out_specs=pl.BlockSpec((tm, P_IN), lambda i: (i, 0)),
        out_shape=jax.ShapeDtypeStruct((n, P_IN), BF),
        compiler_params=_cparams(1), name="inproj")(x, mod, w, b)


def _prep_kernel(x_ref, prev_ref, next_ref, cw_ref, cb_ref, lg_ref, lb_ref, sw_ref, hw_ref, hb_ref,
                 ya_ref, yc_ref, hv_ref, h1_ref, h2_ref, buf_a, buf_c, buf_h, shifted, *, tm, seq_tiles):
    i = pl.program_id(0)
    pos = i % seq_tiles
    keep_prev = jnp.where(pos == 0, 0.0, 1.0)
    keep_next = jnp.where(pos == seq_tiles - 1, 0.0, 1.0)

    def fill(t, lo, hi):
        glu = t[:, 0:WA] * _sigmoid(t[:, WA:2 * WA])
        buf_a[lo:hi, :] = glu
        buf_c[lo:hi, :] = t[:, 3 * WA:4 * WA] * t[:, 4 * WA:5 * WA]
        buf_h[lo:hi, :] = t[:, 5 * WA:8 * WA]

    main = x_ref[...].astype(F32)
    fill(prev_ref[...].astype(F32) * keep_prev, 0, HALO)
    fill(main, HALO, HALO + tm)
    fill(next_ref[...].astype(F32) * keep_next, HALO + tm, 2 * HALO + tm)

    span = tm + SHIFT_SPAN
    for r in range(8):
        shifted[r] = buf_a[r:r + span, :]
    acc = jnp.zeros((tm, WA), F32) + cb_ref[...]
    for j in range(CONF_K):
        off = HALO - CONF_K // 2 + j
        base = off - off % 8
        acc = acc + cw_ref[j:j + 1, :] * shifted[off % 8, base:base + tm, :]
    u = _ln(acc) * lg_ref[...] + lb_ref[...]
    ya_ref[...] = (u * _sigmoid(u)).astype(BF)

    acc = jnp.zeros((tm, WC), F32)
    for j in range(SC_K):
        off = HALO - SC_K // 2 + j
        acc = acc + sw_ref[j:j + 1, :] * buf_c[off:off + tm, :]
    yc_ref[...] = (main[:, 2 * WA:3 * WA] * acc).astype(BF)

    acc = jnp.zeros((tm, 3 * WD), F32) + hb_ref[...]
    for j in range(HY_SHORT_K):
        off = HALO - HY_SHORT_K // 2 + j
        acc = acc + hw_ref[j:j + 1, :] * buf_h[off:off + tm, :]
    hv_ref[...] = acc[:, 0:WD].astype(BF)
    h1_ref[...] = acc[:, WD:2 * WD].astype(BF)
    h2_ref[...] = acc[:, 2 * WD:3 * WD].astype(BF)


def _prep(p, cw, cb, lg, lb, sw, hw, hb, *, tm, seq_len):
    n = p.shape[0]
    seq_tiles = seq_len // tm
    hb_per_tile = tm // HALO
    n_halo_blocks = n // HALO
    cblk = NEW_CONV // CONV_W
    kern = functools.partial(_prep_kernel, tm=tm, seq_tiles=seq_tiles)
    out_bf = jax.ShapeDtypeStruct((n, WD), BF)
    ospec = pl.BlockSpec((tm, WD), lambda i: (i, 0))
    return pl.pallas_call(
        kern, grid=(n // tm,),
        in_specs=[pl.BlockSpec((tm, CONV_W), lambda i: (i, cblk)),
                  pl.BlockSpec((HALO, CONV_W),
                               lambda i: (jnp.maximum(i * hb_per_tile - 1, 0), cblk)),
                  pl.BlockSpec((HALO, CONV_W),
                               lambda i: (jnp.minimum((i + 1) * hb_per_tile, n_halo_blocks - 1), cblk)),
                  _resident(cw.shape), _resident(cb.shape), _resident(lg.shape), _resident(lb.shape),
                  _resident(sw.shape), _resident(hw.shape), _resident(hb.shape)],
        out_specs=[ospec] * 5, out_shape=[out_bf] * 5,
        scratch_shapes=[pltpu.VMEM((tm + 2 * HALO, WA), F32),
                        pltpu.VMEM((tm + 2 * HALO, WC), F32),
                        pltpu.VMEM((tm + 2 * HALO, 3 * WD), F32),
                        pltpu.VMEM((8, tm + SHIFT_SPAN, WA), F32)],
        compiler_params=_cparams(1), name="prep")(p, p, p, cw, cb, lg, lb, sw, hw, hb)


def _na_kernel(q_ref, k0_ref, k1_ref, k2_ref, v0_ref, v1_ref, v2_ref, kc_ref, vc_ref, bias_ref, o_ref):
    q = q_ref[...] * ATTN_SCALE
    k = jnp.concatenate([k0_ref[...], k1_ref[...], k2_ref[...]], axis=0)
    v = jnp.concatenate([v0_ref[...], v1_ref[...], v2_ref[...]], axis=0)
    kc = kc_ref[...]
    vc = vc_ref[...]
    ones_nb = jnp.ones((v.shape[0], NA_HEAD_DIM), BF)
    ones_cx = jnp.ones((vc.shape[0], NA_HEAD_DIM), BF)

    def scores(h):
        sl = slice(h * NA_HEAD_DIM, (h + 1) * NA_HEAD_DIM)
        qh = q[:, sl]
        return _dot_nt(qh, k[:, sl]) + bias_ref[0, h], _dot_nt(qh, kc[:, sl])

    def weighted(p_nb, p_cx, sl):
        o = (_dot(p_nb, jnp.concatenate([v[:, sl], ones_nb], axis=1))
             + _dot(p_cx, jnp.concatenate([vc[:, sl], ones_cx], axis=1)))
        return o[:, :NA_HEAD_DIM] / o[:, NA_HEAD_DIM:NA_HEAD_DIM + 1]

    outs = []
    sc = {0: scores(0), 1: scores(1)}
    pending = None
    for h in range(NA_HEADS):
        sl = slice(h * NA_HEAD_DIM, (h + 1) * NA_HEAD_DIM)
        s_nb, s_cx = sc.pop(h)
        if h + 2 < NA_HEADS:
            sc[h + 2] = scores(h + 2)
        m = jnp.maximum(jnp.max(s_nb, axis=-1, keepdims=True), jnp.max(s_cx, axis=-1, keepdims=True))
        p_nb = jnp.exp((s_nb - m).astype(BF))
        p_cx = jnp.exp((s_cx - m).astype(BF))
        if pending is not None:
            outs.append(pending())
        pending = functools.partial(weighted, p_nb, p_cx, sl)
    outs.append(pending())
    o_ref[...] = jnp.concatenate(outs, axis=-1).astype(BF)


def _rpb_expand_kernel(rpb_ref, onehot_ref, o_ref):
    o_ref[...] = jnp.dot(rpb_ref[...], onehot_ref[...], precision=lax.Precision.HIGHEST,
                         preferred_element_type=F32)


def _na_bias_table(rpb, rows):
    assert rows >= NA_WIN_ROWS and rows % NA_QROWS == 0 and rows // NA_QROWS >= 3
    n_dr, n_dc = 2 * NA_WIN_ROWS - 1, 2 * NA_WIN_COLS - 1
    cols = np.arange(GRID_W)
    c0 = np.clip(cols - NA_WIN_COLS // 2, 0, GRID_W - NA_WIN_COLS)
    dc = cols[None, :] - cols[:, None] + (NA_WIN_COLS - 1)
    ok_c = (cols[None, :] >= c0[:, None]) & (cols[None, :] < c0[:, None] + NA_WIN_COLS)
    onehot = np.zeros((LANES, GRID_W, GRID_W), np.float32)
    qq, kk = np.nonzero(ok_c)
    onehot[dc[qq, kk], qq, kk] = 1.0
    rpb2 = jnp.pad(rpb.reshape(NA_HEADS * n_dr, n_dc).astype(F32),
                   ((0, 64 - NA_HEADS * n_dr), (0, LANES - n_dc)))
    t = pl.pallas_call(
        _rpb_expand_kernel, grid=(1,),
        in_specs=[_resident((64, LANES)), _resident((LANES, GRID_W * GRID_W))],
        out_specs=pl.BlockSpec((64, GRID_W * GRID_W), lambda i: (0, 0)),
        out_shape=jax.ShapeDtypeStruct((64, GRID_W * GRID_W), F32),
        compiler_params=_cparams(1), name="rpb_expand")(rpb2, jnp.asarray(onehot.reshape(LANES, GRID_W * GRID_W)))
    t = t[:NA_HEADS * n_dr].reshape(NA_HEADS, n_dr, GRID_W, GRID_W)
    full = jnp.concatenate(
        [jnp.concatenate([t[:, s - a + NA_WIN_ROWS - 1 - NA_QROWS] for s in range(3 * NA_QROWS)], axis=-1)
         for a in range(NA_QROWS)], axis=1)
    wr = NA_WIN_ROWS
    n_blk = rows // NA_QROWS
    tabs = []
    for blk in (0, 1, n_blk - 1):
        qr = blk * NA_QROWS + np.arange(NA_QROWS)
        kr = (blk - 1) * NA_QROWS + np.arange(3 * NA_QROWS)
        r0 = np.clip(qr - wr // 2, 0, rows - wr)
        ok_r = ((kr[None, :] >= r0[:, None]) & (kr[None, :] < r0[:, None] + wr)
                & (kr[None, :] >= 0) & (kr[None, :] < rows))
        ok = (ok_r[:, None, :, None] & ok_c[None, :, None, :]).reshape(NA_TOK, 3 * NA_TOK)
        tabs.append(jnp.where(jnp.asarray(ok)[None], full, NEG_INF))
    return jnp.stack(tabs)


def _na(p, pc, bias_tab, *, bsz, seq_len):
    t = seq_len // NA_TOK
    qb, kb, vb = NEW_Q // WB, NEW_Q // WB + 1, NEW_Q // WB + 2
    blk = (NA_TOK, WB)
    cblk = (pc.shape[0] // bsz, WB)

    def nbr(j, col):
        return pl.BlockSpec(blk, lambda b, i: (b * t + jnp.clip(i - 1 + j, 0, t - 1), col))

    return pl.pallas_call(
        _na_kernel, grid=(bsz, t),
        in_specs=[pl.BlockSpec(blk, lambda b, i: (b * t + i, qb)),
                  nbr(0, kb), nbr(1, kb), nbr(2, kb), nbr(0, vb), nbr(1, vb), nbr(2, vb),
                  pl.BlockSpec(cblk, lambda b, i: (b, kb)),
                  pl.BlockSpec(cblk, lambda b, i: (b, vb)),
                  pl.BlockSpec((1, NA_HEADS, NA_TOK, 3 * NA_TOK),
                               lambda b, i: (jnp.where(i == 0, 0, jnp.where(i == t - 1, 2, 1)), 0, 0, 0))],
        out_specs=pl.BlockSpec(blk, lambda b, i: (b * t + i, 0)),
        out_shape=jax.ShapeDtypeStruct((bsz * seq_len, WB), BF),
        compiler_params=_cparams(2), name="na")(p, p, p, p, p, p, p, pc, pc, bias_tab)


def _ctx_attn_kernel(q_ref, k_ref, v_ref, o_ref):
    q = q_ref[...] * ATTN_SCALE
    k = k_ref[...]
    v = v_ref[...]
    outs = []
    for h in range(NA_HEADS):
        sl = slice(h * NA_HEAD_DIM, (h + 1) * NA_HEAD_DIM)
        s = _dot_nt(q[:, sl], k[:, sl])
        m = jnp.max(s, axis=-1, keepdims=True)
        p = jnp.exp(s - m)
        den = jnp.sum(p, axis=-1, keepdims=True)
        outs.append(_dot(p.astype(BF), v[:, sl]) / den)
    o_ref[...] = jnp.concatenate(outs, axis=-1).astype(BF)


def _ctx_attn(pc, *, bsz, ctx_len):
    qb, kb, vb = NEW_Q // WB, NEW_Q // WB + 1, NEW_Q // WB + 2
    blk = (ctx_len, WB)
    return pl.pallas_call(
        _ctx_attn_kernel, grid=(bsz,),
        in_specs=[pl.BlockSpec(blk, lambda b: (b, qb)), pl.BlockSpec(blk, lambda b: (b, kb)),
                  pl.BlockSpec(blk, lambda b: (b, vb))],
        out_specs=pl.BlockSpec(blk, lambda b: (b, 0)),
        out_shape=jax.ShapeDtypeStruct((bsz * ctx_len, WB), BF),
        compiler_params=_cparams(1), name="ctx_attn")(pc, pc, pc)


def _filt_kernel(zf_ref, zb_ref, w1_ref, b1_ref, w2_ref, b2_ref, fr_ref, w3_ref, dl_ref, k_ref, ss_ref,
                 *, tm):
    i = pl.program_id(0)
    hp = lax.Precision.HIGHEST
    zf = zf_ref[...]
    zb = zb_ref[...]
    w1 = w1_ref[...]
    pre = jnp.concatenate([jnp.dot(zf, w1, precision=hp, preferred_element_type=F32),
                           jnp.dot(zb, w1, precision=hp, preferred_element_type=F32)], axis=-1)
    h = jnp.sin(fr_ref[0:1, :] * (pre + b1_ref[...]))
    h = jnp.sin(fr_ref[1:2, :] * (jnp.dot(h, w2_ref[...], precision=hp, preferred_element_type=F32)
                                  + b2_ref[...]))
    k = _dot(h.astype(BF), w3_ref[...])
    dec_f = jnp.exp(-zf[:, 0:1] * dl_ref[...])
    dec_b = jnp.exp(-zb[:, 0:1] * dl_ref[...])
    k = k * jnp.concatenate([dec_f] * HY_ORDER + [dec_b] * HY_ORDER, axis=-1)
    is_row0 = (i * tm + lax.broadcasted_iota(jnp.int32, (tm, 1), 0)) == 0
    kf = k[:, :HY_ORDER * WD]
    kb = jnp.where(is_row0, 0.0, k[:, HY_ORDER * WD:])
    k_ref[0] = kf.astype(BF)
    k_ref[1] = kb.astype(BF)

    @pl.when(i == 0)
    def _():
        ss_ref[...] = jnp.zeros_like(ss_ref)

    ss_ref[...] += jnp.sum(kf * kf + kb * kb, axis=0, keepdims=True)


def _pos_features(t, length):
    t_norm = t / max(length - 1, 1)
    bands = jnp.linspace(1e-4, HY_PE_BANDS - 1, HY_PE_BANDS, dtype=F32)
    ang = (2.0 * math.pi / length) * t[:, None] * bands[None, :]
    z = jnp.concatenate([t_norm[:, None], jnp.cos(ang), -jnp.sin(ang)], axis=-1)
    return jnp.pad(z, ((0, 0), (0, LANES - HY_PE_DIM)))


def _block_diag(a, b):
    return jnp.concatenate([jnp.concatenate([a, jnp.zeros((a.shape[0], b.shape[1]), a.dtype)], axis=1),
                            jnp.concatenate([jnp.zeros((b.shape[0], a.shape[1]), a.dtype), b], axis=1)],
                           axis=0)


def _hyena_filters(length, w1, b1, w2, b2, freq, w3):
    t = jnp.arange(length, dtype=F32)
    zf = _pos_features(t, length)
    zb = _pos_features(length - t, length)
    w1p = jnp.pad(w1, ((0, LANES - HY_PE_DIM), (0, 0)))
    w3r = w3.reshape(HY_FILT_HID, HY_ORDER, 2, WD)
    w3bd = _block_diag(w3r[:, :, 0].reshape(HY_FILT_HID, HY_ORDER * WD),
                       w3r[:, :, 1].reshape(HY_FILT_HID, HY_ORDER * WD)).astype(BF)
    w2bd = _block_diag(w2, w2)
    b1d = jnp.concatenate([b1, b1])[None]
    b2d = jnp.concatenate([b2, b2])[None]
    frd = jnp.concatenate([freq, freq], axis=1)
    deltas = jnp.abs(jnp.linspace(math.log(HY_DECAY_TARGET) / HY_SLOW_DECAY,
                                  math.log(HY_DECAY_TARGET) / HY_FAST_DECAY, WD, dtype=F32))[None]
    tm = min(512, length)
    nf = HY_ORDER * WD
    zspec = pl.BlockSpec((tm, LANES), lambda i: (i, 0))
    k, ss = pl.pallas_call(
        functools.partial(_filt_kernel, tm=tm), grid=(length // tm,),
        in_specs=[zspec, zspec, _resident(w1p.shape), _resident(b1d.shape), _resident(w2bd.shape),
                  _resident(b2d.shape), _resident(frd.shape), _resident(w3bd.shape),
                  _resident(deltas.shape)],
        out_specs=[pl.BlockSpec((2, tm, nf), lambda i: (0, i, 0)), pl.BlockSpec((1, nf), lambda i: (0, 0))],
        out_shape=[jax.ShapeDtypeStruct((2, length, nf), BF), jax.ShapeDtypeStruct((1, nf), F32)],
        compiler_params=_cparams(1, "arbitrary"), name="hyena_filter")(zf, zb, w1p, b1d, w2bd, b2d, frd, w3bd, deltas)
    return k.reshape(2 * length, nf), ss


def _cis(num, den):
    ang = (-2.0 * math.pi / den) * (num % den).astype(F32)
    return jnp.cos(ang), jnp.sin(ang)


def _stack(re, im):
    return jnp.concatenate([jnp.concatenate([re, -im], axis=1), jnp.concatenate([im, re], axis=1)], axis=0)


def _dft_tables(n1):
    n = n1 * FFT_N2
    i1 = jnp.arange(n1)
    fr, fi = _cis(i1[:, None] * i1[None, :], n1)
    half = n1 // 2
    w_fwd = _stack(fr[:, :half], fi[:, :half]).astype(BF)
    w_real = jnp.concatenate([fr, fi], axis=0).astype(BF)
    w_inv = (_stack(fr[:half, :], -fi[:half, :]) / n).astype(BF)
    i2 = jnp.arange(FFT_N2)
    f2r, f2i = _cis(i2[:, None] * i2[None, :], FFT_N2)
    twr, twi = _cis(i1[:, None] * i2[None, :], n)
    gr = f2r[None] * twr[:, None, :] - f2i[None] * twi[:, None, :]
    gi = f2r[None] * twi[:, None, :] + f2i[None] * twr[:, None, :]
    gs = jnp.concatenate([jnp.concatenate([gr, -gi], axis=2), jnp.concatenate([gi, gr], axis=2)], axis=1)
    gs = gs.astype(BF)
    return w_fwd, w_real, w_inv, gs


def _dft_tables_direct(length):
    n = 2 * length
    i = jnp.arange(n)
    fr, fi = _cis(i[:, None] * i[None, :], n)
    w_fwd = _stack(fr[:, :length], fi[:, :length]).astype(BF)
    w_real = jnp.concatenate([fr, fi], axis=0).astype(BF)
    w_inv = (_stack(fr[:length, :], -fi[:length, :]) / n).astype(BF)
    return w_fwd, w_real, w_inv


def _stage_a_kernel(*refs, nb, n_half, has_epi):
    w_ref, x_ref = refs[0], refs[1]
    m, k = w_ref.shape
    if has_epi:
        g_ref, u_ref, row_ref, o_ref = refs[2:6]
        scratch = refs[6:]
    else:
        o_ref = refs[2]
        scratch = refs[3:]
    xs, os_ = scratch[:n_half], scratch[n_half:]
    w = w_ref[...]
    for l in range(n_half):
        lanes = slice(l * LANES, (l + 1) * LANES)
        xs[l][...] = x_ref[:, :, lanes].astype(F32).reshape(k * nb, LANES)
        for j in range(nb):
            os_[l][pl.ds(j, m, stride=nb), :] = _dot(w, xs[l][pl.ds(j, k, stride=nb), :].astype(BF))
        acc = os_[l][...].reshape(m, nb, LANES)
        if has_epi:
            acc = g_ref[:, :, lanes].astype(F32) * (
                acc + u_ref[:, :, lanes].astype(F32) * row_ref[:, lanes].reshape(1, 1, LANES))
        o_ref[:, :, lanes] = acc.astype(o_ref.dtype)


def _stage_a(w, x, *, name, epi=None, nb=16, lane_blk=256):
    m, k = w.shape
    c = x.shape[2]
    n_half = lane_blk // LANES
    xblk = pl.BlockSpec((k, nb, lane_blk), lambda j, l: (0, j, l))
    oblk = pl.BlockSpec((m, nb, lane_blk), lambda j, l: (0, j, l))
    ops, specs = [w, x], [_resident((m, k)), xblk]
    if epi is not None:
        ops += list(epi)
        specs += [oblk, oblk, pl.BlockSpec((1, lane_blk), lambda j, l: (0, l))]
    scratch = ([pltpu.VMEM((k * nb, LANES), F32)] * n_half + [pltpu.VMEM((m * nb, LANES), F32)] * n_half)
    return pl.pallas_call(
        functools.partial(_stage_a_kernel, nb=nb, n_half=n_half, has_epi=epi is not None),
        grid=(FFT_N2 // nb, c // lane_blk), in_specs=specs, out_specs=oblk, scratch_shapes=scratch,
        out_shape=jax.ShapeDtypeStruct((m, FFT_N2, c), BF), compiler_params=_cparams(2), name=name)(*ops)


def _stage_b_kernel(a_ref, f_ref, gs_ref, ss_ref, o_ref, *, kb):
    scale = lax.rsqrt(ss_ref[...] + 1e-6)
    yhs = []
    for j in range(kb):
        a = jnp.concatenate([a_ref[0, j], a_ref[1, j]], axis=0)
        f = jnp.concatenate([f_ref[0, j], f_ref[1, j]], axis=0)
        yhs.append(_dot(gs_ref[j], jnp.concatenate([a, f], axis=1)))
    zs = []
    for yh in yhs:
        y, h = yh[:, :WD], yh[:, WD:] * scale
        yr, yi = y[:FFT_N2], y[FFT_N2:]
        hr, hi = h[:FFT_N2], h[FFT_N2:]
        zs.append(jnp.concatenate([yr * hr - yi * hi, yr * hi + yi * hr], axis=0).astype(BF))
    for j in range(kb):
        b = lax.dot_general(gs_ref[j], zs[j], (((0,), (0,)), ((), ())), preferred_element_type=F32)
        o_ref[0, j] = b[:FFT_N2].astype(BF)
        o_ref[1, j] = b[FFT_N2:].astype(BF)


def _stage_b(a, filt_a, gs, ss, *, order, kb=16):
    n1 = a.shape[1]
    kb = min(kb, n1)
    dblk = pl.BlockSpec((2, kb, FFT_N2, WD), lambda i: (0, i, 0, 0))
    fblk = pl.BlockSpec((2, kb, FFT_N2, WD), lambda i: (0, i, 0, order))
    gblk = pl.BlockSpec((kb, 2 * FFT_N2, 2 * FFT_N2), lambda i: (i, 0, 0))
    return pl.pallas_call(
        functools.partial(_stage_b_kernel, kb=kb), grid=(n1 // kb,),
        in_specs=[dblk, fblk, gblk, pl.BlockSpec((1, WD), lambda i: (0, order))], out_specs=dblk,
        out_shape=jax.ShapeDtypeStruct((2, n1, FFT_N2, WD), BF),
        compiler_params=_cparams(1), name="dft_b")(a, filt_a, gs, ss)


def _cmul_kernel(x_ref, h_ref, ss_ref, o_ref, *, half):
    scale = lax.rsqrt(ss_ref[...] + 1e-6)
    xr, xi = x_ref[:half, :], x_ref[half:, :]
    hr, hi = h_ref[:half, :] * scale, h_ref[half:, :] * scale
    o_ref[:half, :] = (xr * hr - xi * hi).astype(BF)
    o_ref[half:, :] = (xr * hi + xi * hr).astype(BF)


def _cmul(x, h, ss, lane_blk):
    rows = x.shape[0]
    return pl.pallas_call(
        functools.partial(_cmul_kernel, half=rows // 2), grid=(1,),
        in_specs=[_resident(x.shape), pl.BlockSpec((rows, WD), lambda i: (0, lane_blk)),
                  pl.BlockSpec((1, WD), lambda i: (0, lane_blk))],
        out_specs=pl.BlockSpec(x.shape, lambda i: (0, 0)),
        out_shape=jax.ShapeDtypeStruct(x.shape, BF), compiler_params=_cparams(1), name="ctx_cmul")(x, h, ss)


def _hyena_long(hv, hx, kern, ss, bias, tables, *, bsz, seq_len):
    assert bsz == 2
    n1 = 2 * seq_len // FFT_N2
    w_fwd, w_real, w_inv, gs = tables
    view = (bsz * seq_len // FFT_N2, FFT_N2, WD)
    filt_a = _stage_a(w_real, kern.reshape(n1, FFT_N2, HY_ORDER * WD), name="dft_a_filter")
    filt_a = filt_a.reshape(2, n1, FFT_N2, HY_ORDER * WD)
    u = hv.reshape(view)
    for o in range(HY_ORDER):
        a = _stage_a(w_fwd, u, name="dft_a_fwd").reshape(2, n1, FFT_N2, WD)
        b = _stage_b(a, filt_a, gs, ss, order=o).reshape(2 * n1, FFT_N2, WD)
        u = _stage_a(w_inv, b, name="dft_a_inv", epi=(hx[o].reshape(view), u, bias[o][None]))
    return u.reshape(bsz * seq_len, WD)


def _hyena_short_seq(hv, hx, kern, ss, bias, tables, *, bsz):
    assert bsz == 2
    w_fwd, w_real, w_inv = tables
    spec = _mm(w_real, kern, bn=HY_ORDER * WD, out_dtype=F32, name="ctx_dft_filter")
    u = hv
    for o in range(HY_ORDER):
        x = _mm(w_fwd, u, bn=WD, out_dtype=F32, name="ctx_dft_fwd")
        z = _cmul(x, spec, ss, o)
        u = _mm(w_inv, z, bn=WD, out_dtype=BF, name="ctx_dft_inv", epi=(hx[o], u, bias[o][None]))
    return u


def _merge_kernel(x_ref, mod_ref, ya_ref, yb_ref, yc_ref, yd_ref, gate_ref, wp_ref, wo_ref, bo_ref,
                  g_ref, b_ref, o_ref, *, alpha):
    x = x_ref[...]
    m = jnp.zeros(x.shape, F32)
    for br, y_ref in enumerate((ya_ref, yb_ref, yc_ref, yd_ref)):
        gate = gate_ref[:, br * D_MODEL:(br + 1) * D_MODEL].astype(F32)
        m = m + gate * _dot(y_ref[...], wp_ref[br])
    out = _dot(m.astype(BF), wo_ref[...]) + bo_ref[...]
    y = alpha * x + mod_ref[0][5:6] * out
    o_ref[...] = _ln(y) * g_ref[...] + b_ref[...]


def _merge(x, mod, ya, yb, yc, yd, p, wp, wo, bo, g, b, *, tm, group_tiles, alpha):
    n = x.shape[0]
    yspec = pl.BlockSpec((tm, WD), lambda i: (i, 0))
    return pl.pallas_call(
        functools.partial(_merge_kernel, alpha=alpha), grid=(n // tm,),
        in_specs=[pl.BlockSpec((tm, D_MODEL), lambda i: (i, 0)),
                  pl.BlockSpec((1, N_MOD, D_MODEL), lambda i: (i // group_tiles, 0, 0)),
                  yspec, yspec, yspec, yspec,
                  pl.BlockSpec((tm, N_BRANCH * D_MODEL), lambda i: (i, 0)),
                  _resident(wp.shape), _resident(wo.shape), _resident((1, D_MODEL)),
                  _resident((1, D_MODEL)), _resident((1, D_MODEL))],
        out_specs=pl.BlockSpec((tm, D_MODEL), lambda i: (i, 0)),
        out_shape=jax.ShapeDtypeStruct((n, D_MODEL), F32),
        compiler_params=_cparams(1), name="merge")(x, mod, ya, yb, yc, yd, p, wp, wo, bo, g, b)


def kernel(x, c, ctx, c_ctx, w_mod, b_mod, post_ln_g, post_ln_b, ffn_w_in, ffn_w_out, w_in, b_in,
           conf_dw_w, conf_dw_b, conf_ln_g, conf_ln_b, conf_w_proj, na_rpb, na_w_proj, sc_conv_w,
           sc_w_proj, hy_sconv_w, hy_sconv_b, hy_w1, hy_b1, hy_w2, hy_b2, hy_freq, hy_w3, hy_bias,
           hy_w_proj, w_out, b_out):
    bsz, n_lat, _ = x.shape
    ctx_len = ctx.shape[1]
    depth = w_mod.shape[0]
    alpha = (2 * depth) ** 0.25
    rows = n_lat // GRID_W

    tm = 512
    tmc = min(tm, ctx_len)
    tf = 1024 if n_lat % 1024 == 0 else tm
    tfc = min(tf, bsz * ctx_len)
    tp = 512
    tpc = min(tp, ctx_len)

    xl = x.reshape(bsz * n_lat, D_MODEL)
    xc = ctx.reshape(bsz * ctx_len, D_MODEL)
    lat_tiles = n_lat // tm
    ctx_tiles = bsz * ctx_len // tmc

    tables = _dft_tables(2 * n_lat // FFT_N2)
    tables_c = _dft_tables_direct(ctx_len)

    cond = jnp.concatenate([c, c_ctx[None], jnp.zeros((SUBLANES - bsz - 1, D_MODEL), F32)], axis=0)

    ffn_wi = ffn_w_in.astype(BF)
    ffn_wo = ffn_w_out.astype(BF)

    for l in range(depth):
        last = l == depth - 1
        mod_all = _mm(cond, w_mod, bn=1024, out_dtype=F32, name="adaln_mod", bias=b_mod[l][None], a_silu=True,
                      b_layer=l)
        mod_all = mod_all.reshape(SUBLANES, N_MOD, D_MODEL)
        mod = mod_all[:bsz]
        mod_c = mod_all[bsz:bsz + 1]

        wl = w_in[l]
        w_perm = jnp.concatenate([wl[:, OFF_G:], wl[:, OFF_A:OFF_Q], wl[:, OFF_SB:OFF_G],
                                  wl[:, OFF_Q:OFF_SB]], axis=1).astype(BF)
        bl = b_in[l]
        b_perm = jnp.concatenate([bl[OFF_G:], bl[OFF_A:OFF_Q], bl[OFF_SB:OFF_G], bl[OFF_Q:OFF_SB]])[None]
        ln_g = post_ln_g[l][:, None, :]
        ln_b = post_ln_b[l][:, None, :]
        wp = jnp.stack([conf_w_proj[l], na_w_proj[l], sc_w_proj[l], hy_w_proj[l]]).astype(BF)
        wo = w_out[l].astype(BF)
        conv_args = (conf_dw_w[l], conf_dw_b[l][None], conf_ln_g[l][None], conf_ln_b[l][None],
                     sc_conv_w[l], hy_sconv_w[l], hy_sconv_b[l][None])
        filt_args = (hy_w1[l], hy_b1[l], hy_w2[l], hy_b2[l], hy_freq[l], hy_w3[l])

        xl = _ffn(xl, mod, ffn_wi, ffn_wo, ln_g[0], ln_b[0], layer=l, half=0, m0=0, tm=tf,
                  group_tiles=n_lat // tf, alpha=alpha)
        xc = _ffn(xc, mod_c, ffn_wi, ffn_wo, ln_g[0], ln_b[0], layer=l, half=0, m0=0, tm=tfc,
                  group_tiles=bsz * ctx_len // tfc, alpha=alpha)

        p = _inproj(xl, mod, w_perm, b_perm, tm=tm, group_tiles=lat_tiles)
        pc = _inproj(xc, mod_c, w_perm, b_perm, tm=tmc, group_tiles=ctx_tiles)

        ya, yc, hv, h1, h2 = _prep(p, *conv_args, tm=tp, seq_len=n_lat)
        attn = _na(p, pc, _na_bias_table(na_rpb[l], rows), bsz=bsz, seq_len=n_lat)
        kern, ss = _hyena_filters(n_lat, *filt_args)
        yd = _hyena_long(hv, (h1, h2), kern, ss, hy_bias[l], tables, bsz=bsz, seq_len=n_lat)
        xl = _merge(xl, mod, ya, attn, yc, yd, p, wp, wo, b_out[l][None], ln_g[1], ln_b[1],
                    tm=tm, group_tiles=lat_tiles, alpha=alpha)

        if not last:
            ya, yc, hv, h1, h2 = _prep(pc, *conv_args, tm=tpc, seq_len=ctx_len)
            attn_c = _ctx_attn(pc, bsz=bsz, ctx_len=ctx_len)
            kern_c, ss_c = _hyena_filters(ctx_len, *filt_args)
            yd = _hyena_short_seq(hv, (h1, h2), kern_c, ss_c, hy_bias[l], tables_c, bsz=bsz)
            xc = _merge(xc, mod_c, ya, attn_c, yc, yd, pc, wp, wo, b_out[l][None], ln_g[1], ln_b[1],
                        tm=tmc, group_tiles=ctx_tiles, alpha=alpha)

        xl = _ffn(xl, mod, ffn_wi, ffn_wo, ln_g[2], ln_b[2], layer=l, half=1, m0=6, tm=tf,
                  group_tiles=n_lat // tf, alpha=alpha)
        if not last:
            xc = _ffn(xc, mod_c, ffn_wi, ffn_wo, ln_g[2], ln_b[2], layer=l, half=1, m0=6, tm=tfc,
                      group_tiles=bsz * ctx_len // tfc, alpha=alpha)

    return xl.reshape(bsz, n_lat, D_MODEL)
```

```python
import functools
import math

import jax
import jax.numpy as jnp
import numpy as np
from jax import lax
from jax.experimental import pallas as pl
from jax.experimental.pallas import tpu as pltpu

D_MODEL = 1024
GRID_W = 64
N_BRANCH = 4
WA = 256
CONF_K = 31
NA_HEADS = 4
NA_HEAD_DIM = 64
WB = NA_HEADS * NA_HEAD_DIM
NA_WIN_ROWS = 8
NA_WIN_COLS = 16
ATTN_SCALE = NA_HEAD_DIM ** -0.5
WC = 256
SC_K = 3
WD = 256
HY_ORDER = 2
HY_SHORT_K = 3
HY_PE_BANDS = 16
HY_PE_DIM = 1 + 2 * HY_PE_BANDS
HY_FILT_HID = 64
HY_FAST_DECAY = 0.3
HY_SLOW_DECAY = 1.5
HY_DECAY_TARGET = 1e-2
D_FF = 2816
N_MOD = 9
LN_EPS = 1e-5

OFF_A = 0
OFF_Q = OFF_A + 2 * WA
OFF_K = OFF_Q + WB
OFF_V = OFF_K + WB
OFF_SB = OFF_V + WB
OFF_SC = OFF_SB + WC
OFF_SX = OFF_SC + WC
OFF_HV = OFF_SX + WC
OFF_G = OFF_HV + (1 + HY_ORDER) * WD
P_IN = OFF_G + N_BRANCH * D_MODEL

NEW_G = 0
NEW_CONV = N_BRANCH * D_MODEL
CONV_W = 2 * WA + 3 * WC + 3 * WD
NEW_Q = NEW_CONV + CONV_W

BF = jnp.bfloat16
F32 = jnp.float32

LANES = 128
SUBLANES = 8
VMEM_BYTES = 64 * 1024 * 1024
VMEM_LIMIT_BYTES = VMEM_BYTES - 6 * 1024 * 1024
FFN_CHUNK = 256
INPROJ_CHUNKS = (1024,) * 4 + (768, 768, 768, 512)
HALO = 16
SHIFT_SPAN = 24
NA_QROWS = 4
NA_TOK = NA_QROWS * GRID_W
FFT_N2 = 128
NEG_INF = -1e30


def _cparams(n_axes, semantics="parallel"):
    return pltpu.CompilerParams(dimension_semantics=(semantics,) * n_axes,
                                vmem_limit_bytes=VMEM_LIMIT_BYTES)


def _resident(shape):
    nd = len(shape)
    return pl.BlockSpec(shape, lambda *_: (0,) * nd, pipeline_mode=pl.Buffered(1))


def _ln(x):
    mu = jnp.mean(x, axis=-1, keepdims=True)
    xc = x - mu
    var = jnp.mean(xc * xc, axis=-1, keepdims=True)
    return xc * lax.rsqrt(var + LN_EPS)


def _sigmoid(x):
    return 0.5 * jnp.tanh(0.5 * x) + 0.5


def _dot(a, b):
    return jnp.dot(a, b, preferred_element_type=F32)


def _dot_nt(a, b):
    return lax.dot_general(a, b, (((1,), (1,)), ((), ())), preferred_element_type=F32)


def _mm_kernel(*refs, a_silu, has_bias, has_epi, precise):
    a_ref, b_ref = refs[0], refs[1]
    pos = 2
    a = a_ref[...]
    if a_silu:
        a = a.astype(F32)
        a = a * _sigmoid(a)
    if precise:
        acc = jnp.dot(a.astype(F32), b_ref[...].astype(F32), precision=lax.Precision.HIGHEST,
                      preferred_element_type=F32)
    else:
        acc = _dot(a.astype(BF), b_ref[...].astype(BF))
    if has_bias:
        acc = acc + refs[pos][...]
        pos += 1
    if has_epi:
        gate_ref, u_ref, row_ref = refs[pos], refs[pos + 1], refs[pos + 2]
        pos += 3
        acc = gate_ref[...].astype(F32) * (acc + u_ref[...].astype(F32) * row_ref[...])
    o_ref = refs[pos]
    o_ref[...] = acc.astype(o_ref.dtype)


def _mm(a, b, *, bn, out_dtype, name, bias=None, epi=None, a_silu=False, b_layer=None, precise=False):
    m, k = a.shape
    n = b.shape[-1]
    bn = min(bn, n)
    ops = [a, b]
    if b_layer is None:
        bspec = pl.BlockSpec((k, bn), lambda j: (0, j))
    else:
        bspec = pl.BlockSpec((None, k, bn), lambda j: (b_layer, 0, j))
    specs = [_resident((m, k)), bspec]
    if bias is not None:
        ops.append(bias)
        specs.append(pl.BlockSpec((1, bn), lambda j: (0, j)))
    if epi is not None:
        gate, u, row = epi
        ops += [gate, u, row]
        specs += [pl.BlockSpec((m, bn), lambda j: (0, j)),
                  pl.BlockSpec((m, bn), lambda j: (0, j)),
                  pl.BlockSpec((1, bn), lambda j: (0, j))]
    kern = functools.partial(_mm_kernel, a_silu=a_silu, has_bias=bias is not None,
                             has_epi=epi is not None, precise=precise)
    return pl.pallas_call(
        kern, grid=(n // bn,), in_specs=specs,
        out_specs=pl.BlockSpec((m, bn), lambda j: (0, j)),
        out_shape=jax.ShapeDtypeStruct((m, n), out_dtype),
        compiler_params=_cparams(1), name=name)(*ops)


def _modulated(x, mod, m0):
    shift = mod[m0:m0 + 1]
    scale = mod[m0 + 1:m0 + 2]
    return _ln(x) * (1.0 + scale) + shift


def _ffn_kernel(x_ref, mod_ref, wi_ref, wo_ref, g_ref, b_ref, o_ref, u_ref, *, m0, alpha):
    x = x_ref[...]
    mod = mod_ref[0]
    h = _modulated(x, mod, m0).astype(BF)
    for c in range(D_FF // FFN_CHUNK):
        lo = c * FFN_CHUNK
        a = _dot(h, wi_ref[:, lo:lo + FFN_CHUNK])
        g = _dot(h, wi_ref[:, D_FF + lo:D_FF + lo + FFN_CHUNK])
        u_ref[:, lo:lo + FFN_CHUNK] = (g * _sigmoid(g) * a).astype(BF)
    acc = _dot(u_ref[...], wo_ref[...])
    y = alpha * x + (0.5 * mod[m0 + 2:m0 + 3]) * acc
    o_ref[...] = _ln(y) * g_ref[...] + b_ref[...]


def _ffn(x, mod, w_in, w_out, g, b, *, layer, half, m0, tm, group_tiles, alpha):
    n = x.shape[0]

    def stacked(w):
        return pl.BlockSpec((None, None) + w.shape[2:], lambda i: (layer, half, 0, 0),
                            pipeline_mode=pl.Buffered(1))

    kern = functools.partial(_ffn_kernel, m0=m0, alpha=alpha)
    return pl.pallas_call(
        kern, grid=(n // tm,),
        in_specs=[pl.BlockSpec((tm, D_MODEL), lambda i: (i, 0)),
                  pl.BlockSpec((1, N_MOD, D_MODEL), lambda i: (i // group_tiles, 0, 0)),
                  stacked(w_in), stacked(w_out),
                  _resident((1, D_MODEL)), _resident((1, D_MODEL))],
        out_specs=pl.BlockSpec((tm, D_MODEL), lambda i: (i, 0)),
        out_shape=jax.ShapeDtypeStruct((n, D_MODEL), F32),
        scratch_shapes=[pltpu.VMEM((tm, D_FF), BF)],
        compiler_params=_cparams(1), name="ffn")(x, mod, w_in, w_out, g, b)


def _inproj_kernel(x_ref, mod_ref, w_ref, b_ref, o_ref, *, m0):
    h = _modulated(x_ref[...], mod_ref[0], m0).astype(BF)
    lo = 0
    for width in INPROJ_CHUNKS:
        y = _dot(h, w_ref[:, lo:lo + width]) + b_ref[:, lo:lo + width]
        if lo < NEW_CONV:
            y = _sigmoid(y)
        o_ref[:, lo:lo + width] = y.astype(BF)
        lo += width


def _inproj(x, mod, w, b, *, tm, group_tiles):
    n = x.shape[0]
    return pl.pallas_call(
        functools.partial(_inproj_kernel, m0=3), grid=(n // tm,),
        in_specs=[pl.BlockSpec((tm, D_MODEL), lambda i: (i, 0)),
                  pl.BlockSpec((1, N_MOD, D_MODEL), lambda i: (i // group_tiles, 0, 0)),
                  _resident(w.shape), _resident((1, P_IN))],
        out_specs=pl.BlockSpec((tm, P_IN), lambda i: (i, 0)),
        out_shape=jax.ShapeDtypeStruct((n, P_IN), BF),
        compiler_params=_cparams(1), name="inproj")(x, mod, w, b)


def _prep_kernel(x_ref, prev_ref, next_ref, cw_ref, cb_ref, lg_ref, lb_ref, sw_ref, hw_ref, hb_ref,
                 ya_ref, yc_ref, hv_ref, h1_ref, h2_ref, buf_a, buf_c, buf_h, shifted, *, tm, seq_tiles):
    i = pl.program_id(0)
    pos = i % seq_tiles
    keep_prev = jnp.where(pos == 0, 0.0, 1.0)
    keep_next = jnp.where(pos == seq_tiles - 1, 0.0, 1.0)

    def fill(t, lo, hi):
        glu = t[:, 0:WA] * _sigmoid(t[:, WA:2 * WA])
        buf_a[lo:hi, :] = glu
        buf_c[lo:hi, :] = t[:, 3 * WA:4 * WA] * t[:, 4 * WA:5 * WA]
        buf_h[lo:hi, :] = t[:, 5 * WA:8 * WA]

    main = x_ref[...].astype(F32)
    fill(prev_ref[...].astype(F32) * keep_prev, 0, HALO)
    fill(main, HALO, HALO + tm)
    fill(next_ref[...].astype(F32) * keep_next, HALO + tm, 2 * HALO + tm)

    span = tm + SHIFT_SPAN
    for r in range(8):
        shifted[r] = buf_a[r:r + span, :]
    acc = jnp.zeros((tm, WA), F32) + cb_ref[...]
    for j in range(CONF_K):
        off = HALO - CONF_K // 2 + j
        base = off - off % 8
        acc = acc + cw_ref[j:j + 1, :] * shifted[off % 8, base:base + tm, :]
    u = _ln(acc) * lg_ref[...] + lb_ref[...]
    ya_ref[...] = (u * _sigmoid(u)).astype(BF)

    acc = jnp.zeros((tm, WC), F32)
    for j in range(SC_K):
        off = HALO - SC_K // 2 + j
        acc = acc + sw_ref[j:j + 1, :] * buf_c[off:off + tm, :]
    yc_ref[...] = (main[:, 2 * WA:3 * WA] * acc).astype(BF)

    acc = jnp.zeros((tm, 3 * WD), F32) + hb_ref[...]
    for j in range(HY_SHORT_K):
        off = HALO - HY_SHORT_K // 2 + j
        acc = acc + hw_ref[j:j + 1, :] * buf_h[off:off + tm, :]
    hv_ref[...] = acc[:, 0:WD].astype(BF)
    h1_ref[...] = acc[:, WD:2 * WD].astype(BF)
    h2_ref[...] = acc[:, 2 * WD:3 * WD].astype(BF)


def _prep(p, cw, cb, lg, lb, sw, hw, hb, *, tm, seq_len):
    n = p.shape[0]
    seq_tiles = seq_len // tm
    hb_per_tile = tm // HALO
    n_halo_blocks = n // HALO
    cblk = NEW_CONV // CONV_W
    kern = functools.partial(_prep_kernel, tm=tm, seq_tiles=seq_tiles)
    out_bf = jax.ShapeDtypeStruct((n, WD), BF)
    ospec = pl.BlockSpec((tm, WD), lambda i: (i, 0))
    return pl.pallas_call(
        kern, grid=(n // tm,),
        in_specs=[pl.BlockSpec((tm, CONV_W), lambda i: (i, cblk)),
                  pl.BlockSpec((HALO, CONV_W),
                               lambda i: (jnp.maximum(i * hb_per_tile - 1, 0), cblk)),
                  pl.BlockSpec((HALO, CONV_W),
                               lambda i: (jnp.minimum((i + 1) * hb_per_tile, n_halo_blocks - 1), cblk)),
                  _resident(cw.shape), _resident(cb.shape), _resident(lg.shape), _resident(lb.shape),
                  _resident(sw.shape), _resident(hw.shape), _resident(hb.shape)],
        out_specs=[ospec] * 5, out_shape=[out_bf] * 5,
        scratch_shapes=[pltpu.VMEM((tm + 2 * HALO, WA), F32),
                        pltpu.VMEM((tm + 2 * HALO, WC), F32),
                        pltpu.VMEM((tm + 2 * HALO, 3 * WD), F32),
                        pltpu.VMEM((8, tm + SHIFT_SPAN, WA), F32)],
        compiler_params=_cparams(1), name="prep")(p, p, p, cw, cb, lg, lb, sw, hw, hb)


def _na_kernel(q_ref, k0_ref, k1_ref, k2_ref, v0_ref, v1_ref, v2_ref, kc_ref, vc_ref, bias_ref, o_ref):
    q = q_ref[...] * ATTN_SCALE
    k = jnp.concatenate([k0_ref[...], k1_ref[...], k2_ref[...]], axis=0)
    v = jnp.concatenate([v0_ref[...], v1_ref[...], v2_ref[...]], axis=0)
    kc = kc_ref[...]
    vc = vc_ref[...]
    ones_nb = jnp.ones((v.shape[0], NA_HEAD_DIM), BF)
    ones_cx = jnp.ones((vc.shape[0], NA_HEAD_DIM), BF)

    def scores(h):
        sl = slice(h * NA_HEAD_DIM, (h + 1) * NA_HEAD_DIM)
        qh = q[:, sl]
        return _dot_nt(qh, k[:, sl]) + bias_ref[0, h], _dot_nt(qh, kc[:, sl])

    def weighted(p_nb, p_cx, sl):
        o = (_dot(p_nb, jnp.concatenate([v[:, sl], ones_nb], axis=1))
             + _dot(p_cx, jnp.concatenate([vc[:, sl], ones_cx], axis=1)))
        return o[:, :NA_HEAD_DIM] / o[:, NA_HEAD_DIM:NA_HEAD_DIM + 1]

    outs = []
    sc = {0: scores(0), 1: scores(1)}
    pending = None
    for h in range(NA_HEADS):
        sl = slice(h * NA_HEAD_DIM, (h + 1) * NA_HEAD_DIM)
        s_nb, s_cx = sc.pop(h)
        if h + 2 < NA_HEADS:
            sc[h + 2] = scores(h + 2)
        m = jnp.maximum(jnp.max(s_nb, axis=-1, keepdims=True), jnp.max(s_cx, axis=-1, keepdims=True))
        p_nb = jnp.exp((s_nb - m).astype(BF))
        p_cx = jnp.exp((s_cx - m).astype(BF))
        if pending is not None:
            outs.append(pending())
        pending = functools.partial(weighted, p_nb, p_cx, sl)
    outs.append(pending())
    o_ref[...] = jnp.concatenate(outs, axis=-1).astype(BF)


def _rpb_expand_kernel(rpb_ref, onehot_ref, o_ref):
    o_ref[...] = jnp.dot(rpb_ref[...], onehot_ref[...], precision=lax.Precision.HIGHEST,
                         preferred_element_type=F32)


def _na_bias_table(rpb, rows):
    assert rows >= NA_WIN_ROWS and rows % NA_QROWS == 0 and rows // NA_QROWS >= 3
    n_dr, n_dc = 2 * NA_WIN_ROWS - 1, 2 * NA_WIN_COLS - 1
    cols = np.arange(GRID_W)
    c0 = np.clip(cols - NA_WIN_COLS // 2, 0, GRID_W - NA_WIN_COLS)
    dc = cols[None, :] - cols[:, None] + (NA_WIN_COLS - 1)
    ok_c = (cols[None, :] >= c0[:, None]) & (cols[None, :] < c0[:, None] + NA_WIN_COLS)
    onehot = np.zeros((LANES, GRID_W, GRID_W), np.float32)
    qq, kk = np.nonzero(ok_c)
    onehot[dc[qq, kk], qq, kk] = 1.0
    rpb2 = jnp.pad(rpb.reshape(NA_HEADS * n_dr, n_dc).astype(F32),
                   ((0, 64 - NA_HEADS * n_dr), (0, LANES - n_dc)))
    t = pl.pallas_call(
        _rpb_expand_kernel, grid=(1,),
        in_specs=[_resident((64, LANES)), _resident((LANES, GRID_W * GRID_W))],
        out_specs=pl.BlockSpec((64, GRID_W * GRID_W), lambda i: (0, 0)),
        out_shape=jax.ShapeDtypeStruct((64, GRID_W * GRID_W), F32),
        compiler_params=_cparams(1), name="rpb_expand")(rpb2, jnp.asarray(onehot.reshape(LANES, GRID_W * GRID_W)))
    t = t[:NA_HEADS * n_dr].reshape(NA_HEADS, n_dr, GRID_W, GRID_W)
    full = jnp.concatenate(
        [jnp.concatenate([t[:, s - a + NA_WIN_ROWS - 1 - NA_QROWS] for s in range(3 * NA_QROWS)], axis=-1)
         for a in range(NA_QROWS)], axis=1)
    wr = NA_WIN_ROWS
    n_blk = rows // NA_QROWS
    tabs = []
    for blk in (0, 1, n_blk - 1):
        qr = blk * NA_QROWS + np.arange(NA_QROWS)
        kr = (blk - 1) * NA_QROWS + np.arange(3 * NA_QROWS)
        r0 = np.clip(qr - wr // 2, 0, rows - wr)
        ok_r = ((kr[None, :] >= r0[:, None]) & (kr[None, :] < r0[:, None] + wr)
                & (kr[None, :] >= 0) & (kr[None, :] < rows))
        ok = (ok_r[:, None, :, None] & ok_c[None, :, None, :]).reshape(NA_TOK, 3 * NA_TOK)
        tabs.append(jnp.where(jnp.asarray(ok)[None], full, NEG_INF))
    return jnp.stack(tabs)


def _na(p, pc, bias_tab, *, bsz, seq_len):
    t = seq_len // NA_TOK
    qb, kb, vb = NEW_Q // WB, NEW_Q // WB + 1, NEW_Q // WB + 2
    blk = (NA_TOK, WB)
    cblk = (pc.shape[0] // bsz, WB)

    def nbr(j, col):
        return pl.BlockSpec(blk, lambda b, i: (b * t + jnp.clip(i - 1 + j, 0, t - 1), col))

    return pl.pallas_call(
        _na_kernel, grid=(bsz, t),
        in_specs=[pl.BlockSpec(blk, lambda b, i: (b * t + i, qb)),
                  nbr(0, kb), nbr(1, kb), nbr(2, kb), nbr(0, vb), nbr(1, vb), nbr(2, vb),
                  pl.BlockSpec(cblk, lambda b, i: (b, kb)),
                  pl.BlockSpec(cblk, lambda b, i: (b, vb)),
                  pl.BlockSpec((1, NA_HEADS, NA_TOK, 3 * NA_TOK),
                               lambda b, i: (jnp.where(i == 0, 0, jnp.where(i == t - 1, 2, 1)), 0, 0, 0))],
        out_specs=pl.BlockSpec(blk, lambda b, i: (b * t + i, 0)),
        out_shape=jax.ShapeDtypeStruct((bsz * seq_len, WB), BF),
        compiler_params=_cparams(2), name="na")(p, p, p, p, p, p, p, pc, pc, bias_tab)


def _ctx_attn_kernel(q_ref, k_ref, v_ref, o_ref):
    q = q_ref[...] * ATTN_SCALE
    k = k_ref[...]
    v = v_ref[...]
    outs = []
    for h in range(NA_HEADS):
        sl = slice(h * NA_HEAD_DIM, (h + 1) * NA_HEAD_DIM)
        s = _dot_nt(q[:, sl], k[:, sl])
        m = jnp.max(s, axis=-1, keepdims=True)
        p = jnp.exp(s - m)
        den = jnp.sum(p, axis=-1, keepdims=True)
        outs.append(_dot(p.astype(BF), v[:, sl]) / den)
    o_ref[...] = jnp.concatenate(outs, axis=-1).astype(BF)


def _ctx_attn(pc, *, bsz, ctx_len):
    qb, kb, vb = NEW_Q // WB, NEW_Q // WB + 1, NEW_Q // WB + 2
    blk = (ctx_len, WB)
    return pl.pallas_call(
        _ctx_attn_kernel, grid=(bsz,),
        in_specs=[pl.BlockSpec(blk, lambda b: (b, qb)), pl.BlockSpec(blk, lambda b: (b, kb)),
                  pl.BlockSpec(blk, lambda b: (b, vb))],
        out_specs=pl.BlockSpec(blk, lambda b: (b, 0)),
        out_shape=jax.ShapeDtypeStruct((bsz * ctx_len, WB), BF),
        compiler_params=_cparams(1), name="ctx_attn")(pc, pc, pc)


def _filt_kernel(zf_ref, zb_ref, w1_ref, b1_ref, w2_ref, b2_ref, fr_ref, w3_ref, dl_ref, k_ref, ss_ref,
                 *, tm):
    i = pl.program_id(0)
    hp = lax.Precision.HIGHEST
    zf = zf_ref[...]
    zb = zb_ref[...]
    w1 = w1_ref[...]
    pre = jnp.concatenate([jnp.dot(zf, w1, precision=hp, preferred_element_type=F32),
                           jnp.dot(zb, w1, precision=hp, preferred_element_type=F32)], axis=-1)
    h = jnp.sin(fr_ref[0:1, :] * (pre + b1_ref[...]))
    h = jnp.sin(fr_ref[1:2, :] * (jnp.dot(h, w2_ref[...], precision=hp, preferred_element_type=F32)
                                  + b2_ref[...]))
    k = _dot(h.astype(BF), w3_ref[...])
    dec_f = jnp.exp(-zf[:, 0:1] * dl_ref[...])
    dec_b = jnp.exp(-zb[:, 0:1] * dl_ref[...])
    k = k * jnp.concatenate([dec_f] * HY_ORDER + [dec_b] * HY_ORDER, axis=-1)
    is_row0 = (i * tm + lax.broadcasted_iota(jnp.int32, (tm, 1), 0)) == 0
    kf = k[:, :HY_ORDER * WD]
    kb = jnp.where(is_row0, 0.0, k[:, HY_ORDER * WD:])
    k_ref[0] = kf.astype(BF)
    k_ref[1] = kb.astype(BF)

    @pl.when(i == 0)
    def _():
        ss_ref[...] = jnp.zeros_like(ss_ref)

    ss_ref[...] += jnp.sum(kf * kf + kb * kb, axis=0, keepdims=True)


def _pos_features(t, length):
    t_norm = t / max(length - 1, 1)
    bands = jnp.linspace(1e-4, HY_PE_BANDS - 1, HY_PE_BANDS, dtype=F32)
    ang = (2.0 * math.pi / length) * t[:, None] * bands[None, :]
    z = jnp.concatenate([t_norm[:, None], jnp.cos(ang), -jnp.sin(ang)], axis=-1)
    return jnp.pad(z, ((0, 0), (0, LANES - HY_PE_DIM)))


def _block_diag(a, b):
    return jnp.concatenate([jnp.concatenate([a, jnp.zeros((a.shape[0], b.shape[1]), a.dtype)], axis=1),
                            jnp.concatenate([jnp.zeros((b.shape[0], a.shape[1]), a.dtype), b], axis=1)],
                           axis=0)


def _hyena_filters(length, w1, b1, w2, b2, freq, w3):
    t = jnp.arange(length, dtype=F32)
    zf = _pos_features(t, length)
    zb = _pos_features(length - t, length)
    w1p = jnp.pad(w1, ((0, LANES - HY_PE_DIM), (0, 0)))
    w3r = w3.reshape(HY_FILT_HID, HY_ORDER, 2, WD)
    w3bd = _block_diag(w3r[:, :, 0].reshape(HY_FILT_HID, HY_ORDER * WD),
                       w3r[:, :, 1].reshape(HY_FILT_HID, HY_ORDER * WD)).astype(BF)
    w2bd = _block_diag(w2, w2)
    b1d = jnp.concatenate([b1, b1])[None]
    b2d = jnp.concatenate([b2, b2])[None]
    frd = jnp.concatenate([freq, freq], axis=1)
    deltas = jnp.abs(jnp.linspace(math.log(HY_DECAY_TARGET) / HY_SLOW_DECAY,
                                  math.log(HY_DECAY_TARGET) / HY_FAST_DECAY, WD, dtype=F32))[None]
    tm = min(512, length)
    nf = HY_ORDER * WD
    zspec = pl.BlockSpec((tm, LANES), lambda i: (i, 0))
    k, ss = pl.pallas_call(
        functools.partial(_filt_kernel, tm=tm), grid=(length // tm,),
        in_specs=[zspec, zspec, _resident(w1p.shape), _resident(b1d.shape), _resident(w2bd.shape),
                  _resident(b2d.shape), _resident(frd.shape), _resident(w3bd.shape),
                  _resident(deltas.shape)],
        out_specs=[pl.BlockSpec((2, tm, nf), lambda i: (0, i, 0)), pl.BlockSpec((1, nf), lambda i: (0, 0))],
        out_shape=[jax.ShapeDtypeStruct((2, length, nf), BF), jax.ShapeDtypeStruct((1, nf), F32)],
        compiler_params=_cparams(1, "arbitrary"), name="hyena_filter")(zf, zb, w1p, b1d, w2bd, b2d, frd, w3bd, deltas)
    return k.reshape(2 * length, nf), ss


def _cis(num, den):
    ang = (-2.0 * math.pi / den) * (num % den).astype(F32)
    return jnp.cos(ang), jnp.sin(ang)


def _stack(re, im):
    return jnp.concatenate([jnp.concatenate([re, -im], axis=1), jnp.concatenate([im, re], axis=1)], axis=0)


def _dft_tables(n1):
    n = n1 * FFT_N2
    i1 = jnp.arange(n1)
    fr, fi = _cis(i1[:, None] * i1[None, :], n1)
    half = n1 // 2
    w_fwd = _stack(fr[:, :half], fi[:, :half]).astype(BF)
    w_real = jnp.concatenate([fr, fi], axis=0).astype(BF)
    w_inv = (_stack(fr[:half, :], -fi[:half, :]) / n).astype(BF)
    i2 = jnp.arange(FFT_N2)
    f2r, f2i = _cis(i2[:, None] * i2[None, :], FFT_N2)
    twr, twi = _cis(i1[:, None] * i2[None, :], n)
    gr = f2r[None] * twr[:, None, :] - f2i[None] * twi[:, None, :]
    gi = f2r[None] * twi[:, None, :] + f2i[None] * twr[:, None, :]
    gs = jnp.concatenate([jnp.concatenate([gr, -gi], axis=2), jnp.concatenate([gi, gr], axis=2)], axis=1)
    gs = gs.astype(BF)
    return w_fwd, w_real, w_inv, gs


def _dft_tables_direct(length):
    n = 2 * length
    i = jnp.arange(n)
    fr, fi = _cis(i[:, None] * i[None, :], n)
    w_fwd = _stack(fr[:, :length], fi[:, :length]).astype(BF)
    w_real = jnp.concatenate([fr, fi], axis=0).astype(BF)
    w_inv = (_stack(fr[:length, :], -fi[:length, :]) / n).astype(BF)
    return w_fwd, w_real, w_inv


def _stage_a_kernel(*refs, nb, n_half, has_epi):
    w_ref, x_ref = refs[0], refs[1]
    m, k = w_ref.shape
    if has_epi:
        g_ref, u_ref, row_ref, o_ref = refs[2:6]
        scratch = refs[6:]
    else:
        o_ref = refs[2]
        scratch = refs[3:]
    xs, os_ = scratch[:n_half], scratch[n_half:]
    w = w_ref[...]
    for l in range(n_half):
        lanes = slice(l * LANES, (l + 1) * LANES)
        xs[l][...] = x_ref[:, :, lanes].astype(F32).reshape(k * nb, LANES)
        for j in range(nb):
            os_[l][pl.ds(j, m, stride=nb), :] = _dot(w, xs[l][pl.ds(j, k, stride=nb), :].astype(BF))
        acc = os_[l][...].reshape(m, nb, LANES)
        if has_epi:
            acc = g_ref[:, :, lanes].astype(F32) * (
                acc + u_ref[:, :, lanes].astype(F32) * row_ref[:, lanes].reshape(1, 1, LANES))
        o_ref[:, :, lanes] = acc.astype(o_ref.dtype)


def _stage_a(w, x, *, name, epi=None, nb=16, lane_blk=256):
    m, k = w.shape
    c = x.shape[2]
    n_half = lane_blk // LANES
    xblk = pl.BlockSpec((k, nb, lane_blk), lambda j, l: (0, j, l))
    oblk = pl.BlockSpec((m, nb, lane_blk), lambda j, l: (0, j, l))
    ops, specs = [w, x], [_resident((m, k)), xblk]
    if epi is not None:
        ops += list(epi)
        specs += [oblk, oblk, pl.BlockSpec((1, lane_blk), lambda j, l: (0, l))]
    scratch = ([pltpu.VMEM((k * nb, LANES), F32)] * n_half + [pltpu.VMEM((m * nb, LANES), F32)] * n_half)
    return pl.pallas_call(
        functools.partial(_stage_a_kernel, nb=nb, n_half=n_half, has_epi=epi is not None),
        grid=(FFT_N2 // nb, c // lane_blk), in_specs=specs, out_specs=oblk, scratch_shapes=scratch,
        out_shape=jax.ShapeDtypeStruct((m, FFT_N2, c), BF), compiler_params=_cparams(2), name=name)(*ops)


def _stage_b_kernel(a_ref, f_ref, gs_ref, ss_ref, o_ref, *, kb):
    scale = lax.rsqrt(ss_ref[...] + 1e-6)
    yhs = []
    for j in range(kb):
        a = jnp.concatenate([a_ref[0, j], a_ref[1, j]], axis=0)
        f = jnp.concatenate([f_ref[0, j], f_ref[1, j]], axis=0)
        yhs.append(_dot(gs_ref[j], jnp.concatenate([a, f], axis=1)))
    zs = []
    for yh in yhs:
        y, h = yh[:, :WD], yh[:, WD:] * scale
        yr, yi = y[:FFT_N2], y[FFT_N2:]
        hr, hi = h[:FFT_N2], h[FFT_N2:]
        zs.append(jnp.concatenate([yr * hr - yi * hi, yr * hi + yi * hr], axis=0).astype(BF))
    for j in range(kb):
        b = lax.dot_general(gs_ref[j], zs[j], (((0,), (0,)), ((), ())), preferred_element_type=F32)
        o_ref[0, j] = b[:FFT_N2].astype(BF)
        o_ref[1, j] = b[FFT_N2:].astype(BF)


def _stage_b(a, filt_a, gs, ss, *, order, kb=16):
    n1 = a.shape[1]
    kb = min(kb, n1)
    dblk = pl.BlockSpec((2, kb, FFT_N2, WD), lambda i: (0, i, 0, 0))
    fblk = pl.BlockSpec((2, kb, FFT_N2, WD), lambda i: (0, i, 0, order))
    gblk = pl.BlockSpec((kb, 2 * FFT_N2, 2 * FFT_N2), lambda i: (i, 0, 0))
    return pl.pallas_call(
        functools.partial(_stage_b_kernel, kb=kb), grid=(n1 // kb,),
        in_specs=[dblk, fblk, gblk, pl.BlockSpec((1, WD), lambda i: (0, order))], out_specs=dblk,
        out_shape=jax.ShapeDtypeStruct((2, n1, FFT_N2, WD), BF),
        compiler_params=_cparams(1), name="dft_b")(a, filt_a, gs, ss)


def _cmul_kernel(x_ref, h_ref, ss_ref, o_ref, *, half):
    scale = lax.rsqrt(ss_ref[...] + 1e-6)
    xr, xi = x_ref[:half, :], x_ref[half:, :]
    hr, hi = h_ref[:half, :] * scale, h_ref[half:, :] * scale
    o_ref[:half, :] = (xr * hr - xi * hi).astype(BF)
    o_ref[half:, :] = (xr * hi + xi * hr).astype(BF)


def _cmul(x, h, ss, lane_blk):
    rows = x.shape[0]
    return pl.pallas_call(
        functools.partial(_cmul_kernel, half=rows // 2), grid=(1,),
        in_specs=[_resident(x.shape), pl.BlockSpec((rows, WD), lambda i: (0, lane_blk)),
                  pl.BlockSpec((1, WD), lambda i: (0, lane_blk))],
        out_specs=pl.BlockSpec(x.shape, lambda i: (0, 0)),
        out_shape=jax.ShapeDtypeStruct(x.shape, BF), compiler_params=_cparams(1), name="ctx_cmul")(x, h, ss)


def _hyena_long(hv, hx, kern, ss, bias, tables, *, bsz, seq_len):
    assert bsz == 2
    n1 = 2 * seq_len // FFT_N2
    w_fwd, w_real, w_inv, gs = tables
    view = (bsz * seq_len // FFT_N2, FFT_N2, WD)
    filt_a = _stage_a(w_real, kern.reshape(n1, FFT_N2, HY_ORDER * WD), name="dft_a_filter")
    filt_a = filt_a.reshape(2, n1, FFT_N2, HY_ORDER * WD)
    u = hv.reshape(view)
    for o in range(HY_ORDER):
        a = _stage_a(w_fwd, u, name="dft_a_fwd").reshape(2, n1, FFT_N2, WD)
        b = _stage_b(a, filt_a, gs, ss, order=o).reshape(2 * n1, FFT_N2, WD)
        u = _stage_a(w_inv, b, name="dft_a_inv", epi=(hx[o].reshape(view), u, bias[o][None]))
    return u.reshape(bsz * seq_len, WD)


def _hyena_short_seq(hv, hx, kern, ss, bias, tables, *, bsz):
    assert bsz == 2
    w_fwd, w_real, w_inv = tables
    spec = _mm(w_real, kern, bn=HY_ORDER * WD, out_dtype=F32, name="ctx_dft_filter")
    u = hv
    for o in range(HY_ORDER):
        x = _mm(w_fwd, u, bn=WD, out_dtype=F32, name="ctx_dft_fwd")
        z = _cmul(x, spec, ss, o)
        u = _mm(w_inv, z, bn=WD, out_dtype=BF, name="ctx_dft_inv", epi=(hx[o], u, bias[o][None]))
    return u


def _merge_kernel(x_ref, mod_ref, ya_ref, yb_ref, yc_ref, yd_ref, gate_ref, wp_ref, wo_ref, bo_ref,
                  g_ref, b_ref, o_ref, *, alpha):
    x = x_ref[...]
    m = jnp.zeros(x.shape, F32)
    for br, y_ref in enumerate((ya_ref, yb_ref, yc_ref, yd_ref)):
        gate = gate_ref[:, br * D_MODEL:(br + 1) * D_MODEL].astype(F32)
        m = m + gate * _dot(y_ref[...], wp_ref[br])
    out = _dot(m.astype(BF), wo_ref[...]) + bo_ref[...]
    y = alpha * x + mod_ref[0][5:6] * out
    o_ref[...] = _ln(y) * g_ref[...] + b_ref[...]


def _merge(x, mod, ya, yb, yc, yd, p, wp, wo, bo, g, b, *, tm, group_tiles, alpha):
    n = x.shape[0]
    yspec = pl.BlockSpec((tm, WD), lambda i: (i, 0))
    return pl.pallas_call(
        functools.partial(_merge_kernel, alpha=alpha), grid=(n // tm,),
        in_specs=[pl.BlockSpec((tm, D_MODEL), lambda i: (i, 0)),
                  pl.BlockSpec((1, N_MOD, D_MODEL), lambda i: (i // group_tiles, 0, 0)),
                  yspec, yspec, yspec, yspec,
                  pl.BlockSpec((tm, N_BRANCH * D_MODEL), lambda i: (i, 0)),
                  _resident(wp.shape), _resident(wo.shape), _resident((1, D_MODEL)),
                  _resident((1, D_MODEL)), _resident((1, D_MODEL))],
        out_specs=pl.BlockSpec((tm, D_MODEL), lambda i: (i, 0)),
        out_shape=jax.ShapeDtypeStruct((n, D_MODEL), F32),
        compiler_params=_cparams(1), name="merge")(x, mod, ya, yb, yc, yd, p, wp, wo, bo, g, b)


def kernel(x, c, ctx, c_ctx, w_mod, b_mod, post_ln_g, post_ln_b, ffn_w_in, ffn_w_out, w_in, b_in,
           conf_dw_w, conf_dw_b, conf_ln_g, conf_ln_b, conf_w_proj, na_rpb, na_w_proj, sc_conv_w,
           sc_w_proj, hy_sconv_w, hy_sconv_b, hy_w1, hy_b1, hy_w2, hy_b2, hy_freq, hy_w3, hy_bias,
           hy_w_proj, w_out, b_out):
    bsz, n_lat, _ = x.shape
    ctx_len = ctx.shape[1]
    depth = w_mod.shape[0]
    alpha = (2 * depth) ** 0.25
    rows = n_lat // GRID_W

    tm = 512
    tmc = min(tm, ctx_len)
    tf = 1024 if n_lat % 1024 == 0 else tm
    tfc = min(tf, bsz * ctx_len)
    tp = 512
    tpc = min(tp, ctx_len)

    xl = x.reshape(bsz * n_lat, D_MODEL)
    xc = ctx.reshape(bsz * ctx_len, D_MODEL)
    lat_tiles = n_lat // tm
    ctx_tiles = bsz * ctx_len // tmc

    tables = _dft_tables(2 * n_lat // FFT_N2)
    tables_c = _dft_tables_direct(ctx_len)

    cond = jnp.concatenate([c, c_ctx[None], jnp.zeros((SUBLANES - bsz - 1, D_MODEL), F32)], axis=0)

    ffn_wi = ffn_w_in.astype(BF)
    ffn_wo = ffn_w_out.astype(BF)

    for l in range(depth):
        last = l == depth - 1
        mod_all = _mm(cond, w_mod, bn=1024, out_dtype=F32, name="adaln_mod", bias=b_mod[l][None], a_silu=True,
                      b_layer=l, precise=True)
        mod_all = mod_all.reshape(SUBLANES, N_MOD, D_MODEL)
        mod = mod_all[:bsz]
        mod_c = mod_all[bsz:bsz + 1]

        wl = w_in[l]
        w_perm = jnp.concatenate([wl[:, OFF_G:], wl[:, OFF_A:OFF_Q], wl[:, OFF_SB:OFF_G],
                                  wl[:, OFF_Q:OFF_SB]], axis=1).astype(BF)
        bl = b_in[l]
        b_perm = jnp.concatenate([bl[OFF_G:], bl[OFF_A:OFF_Q], bl[OFF_SB:OFF_G], bl[OFF_Q:OFF_SB]])[None]
        ln_g = post_ln_g[l][:, None, :]
        ln_b = post_ln_b[l][:, None, :]
        wp = jnp.stack([conf_w_proj[l], na_w_proj[l], sc_w_proj[l], hy_w_proj[l]]).astype(BF)
        wo = w_out[l].astype(BF)
        conv_args = (conf_dw_w[l], conf_dw_b[l][None], conf_ln_g[l][None], conf_ln_b[l][None],
                     sc_conv_w[l], hy_sconv_w[l], hy_sconv_b[l][None])
        filt_args = (hy_w1[l], hy_b1[l], hy_w2[l], hy_b2[l], hy_freq[l], hy_w3[l])

        xl = _ffn(xl, mod, ffn_wi, ffn_wo, ln_g[0], ln_b[0], layer=l, half=0, m0=0, tm=tf,
                  group_tiles=n_lat // tf, alpha=alpha)
        xc = _ffn(xc, mod_c, ffn_wi, ffn_wo, ln_g[0], ln_b[0], layer=l, half=0, m0=0, tm=tfc,
                  group_tiles=bsz * ctx_len // tfc, alpha=alpha)

        p = _inproj(xl, mod, w_perm, b_perm, tm=tm, group_tiles=lat_tiles)
        pc = _inproj(xc, mod_c, w_perm, b_perm, tm=tmc, group_tiles=ctx_tiles)

        ya, yc, hv, h1, h2 = _prep(p, *conv_args, tm=tp, seq_len=n_lat)
        attn = _na(p, pc, _na_bias_table(na_rpb[l], rows), bsz=bsz, seq_len=n_lat)
        kern, ss = _hyena_filters(n_lat, *filt_args)
        yd = _hyena_long(hv, (h1, h2), kern, ss, hy_bias[l], tables, bsz=bsz, seq_len=n_lat)
        xl = _merge(xl, mod, ya, attn, yc, yd, p, wp, wo, b_out[l][None], ln_g[1], ln_b[1],
                    tm=tm, group_tiles=lat_tiles, alpha=alpha)

        if not last:
            ya, yc, hv, h1, h2 = _prep(pc, *conv_args, tm=tpc, seq_len=ctx_len)
            attn_c = _ctx_attn(pc, bsz=bsz, ctx_len=ctx_len)
            kern_c, ss_c = _hyena_filters(ctx_len, *filt_args)
            yd = _hyena_short_seq(hv, (h1, h2), kern_c, ss_c, hy_bias[l], tables_c, bsz=bsz)
            xc = _merge(xc, mod_c, ya, attn_c, yc, yd, pc, wp, wo, b_out[l][None], ln_g[1], ln_b[1],
                        tm=tmc, group_tiles=ctx_tiles, alpha=alpha)

        xl = _ffn(xl, mod, ffn_wi, ffn_wo, ln_g[2], ln_b[2], layer=l, half=1, m0=6, tm=tf,
                  group_tiles=n_lat // tf, alpha=alpha)
        if not last:
            xc = _ffn(xc, mod_c, ffn_wi, ffn_wo, ln_g[2], ln_b[2], layer=l, half=1, m0=6, tm=tfc,
                      group_tiles=bsz * ctx_len // tfc, alpha=alpha)

    return xl.reshape(bsz, n_lat, D_MODEL)
```

```python
import functools
import math

import jax
import jax.numpy as jnp
import numpy as np
from jax import lax
from jax.experimental import pallas as pl
from jax.experimental.pallas import tpu as pltpu

D_MODEL = 1024
GRID_W = 64
N_BRANCH = 4
WA = 256
CONF_K = 31
NA_HEADS = 4
NA_HEAD_DIM = 64
WB = NA_HEADS * NA_HEAD_DIM
NA_WIN_ROWS = 8
NA_WIN_COLS = 16
ATTN_SCALE = NA_HEAD_DIM ** -0.5
WC = 256
SC_K = 3
WD = 256
HY_ORDER = 2
HY_SHORT_K = 3
HY_PE_BANDS = 16
HY_PE_DIM = 1 + 2 * HY_PE_BANDS
HY_FILT_HID = 64
HY_FAST_DECAY = 0.3
HY_SLOW_DECAY = 1.5
HY_DECAY_TARGET = 1e-2
D_FF = 2816
N_MOD = 9
LN_EPS = 1e-5

OFF_A = 0
OFF_Q = OFF_A + 2 * WA
OFF_K = OFF_Q + WB
OFF_V = OFF_K + WB
OFF_SB = OFF_V + WB
OFF_SC = OFF_SB + WC
OFF_SX = OFF_SC + WC
OFF_HV = OFF_SX + WC
OFF_G = OFF_HV + (1 + HY_ORDER) * WD
P_IN = OFF_G + N_BRANCH * D_MODEL

NEW_G = 0
NEW_CONV = N_BRANCH * D_MODEL
CONV_W = 2 * WA + 3 * WC + 3 * WD
NEW_Q = NEW_CONV + CONV_W

BF = jnp.bfloat16
F32 = jnp.float32

LANES = 128
SUBLANES = 8
VMEM_BYTES = 64 * 1024 * 1024
VMEM_LIMIT_BYTES = VMEM_BYTES - 6 * 1024 * 1024
FFN_CHUNK = 256
INPROJ_CHUNKS = (1024,) * 4 + (768, 768, 768, 512)
HALO = 16
SHIFT_SPAN = 24
NA_QROWS = 4
NA_TOK = NA_QROWS * GRID_W
FFT_N2 = 128
DFT_B_KB = 16
NEG_INF = -1e30


def _cparams(n_axes, semantics="parallel"):
    return pltpu.CompilerParams(dimension_semantics=(semantics,) * n_axes,
                                vmem_limit_bytes=VMEM_LIMIT_BYTES)


def _resident(shape):
    nd = len(shape)
    return pl.BlockSpec(shape, lambda *_: (0,) * nd, pipeline_mode=pl.Buffered(1))


def _ln(x):
    mu = jnp.mean(x, axis=-1, keepdims=True)
    xc = x - mu
    var = jnp.mean(xc * xc, axis=-1, keepdims=True)
    return xc * lax.rsqrt(var + LN_EPS)


def _sigmoid(x):
    return 0.5 * jnp.tanh(0.5 * x) + 0.5


def _dot(a, b):
    return jnp.dot(a, b, preferred_element_type=F32)


def _dot_nt(a, b):
    return lax.dot_general(a, b, (((1,), (1,)), ((), ())), preferred_element_type=F32)


def _mm_kernel(*refs, a_silu, has_bias, has_epi, precise):
    a_ref, b_ref = refs[0], refs[1]
    pos = 2
    a = a_ref[...]
    if a_silu:
        a = a.astype(F32)
        a = a * _sigmoid(a)
    if precise:
        acc = jnp.dot(a.astype(F32), b_ref[...].astype(F32), precision=lax.Precision.HIGHEST,
                      preferred_element_type=F32)
    else:
        acc = _dot(a.astype(BF), b_ref[...].astype(BF))
    if has_bias:
        acc = acc + refs[pos][...]
        pos += 1
    if has_epi:
        gate_ref, u_ref, row_ref = refs[pos], refs[pos + 1], refs[pos + 2]
        pos += 3
        acc = gate_ref[...].astype(F32) * (acc + u_ref[...].astype(F32) * row_ref[...])
    o_ref = refs[pos]
    o_ref[...] = acc.astype(o_ref.dtype)


def _mm(a, b, *, bn, out_dtype, name, bias=None, epi=None, a_silu=False, b_layer=None, precise=False):
    m, k = a.shape
    n = b.shape[-1]
    bn = min(bn, n)
    ops = [a, b]
    if b_layer is None:
        bspec = pl.BlockSpec((k, bn), lambda j: (0, j))
    else:
        bspec = pl.BlockSpec((None, k, bn), lambda j: (b_layer, 0, j))
    specs = [_resident((m, k)), bspec]
    if bias is not None:
        ops.append(bias)
        specs.append(pl.BlockSpec((1, bn), lambda j: (0, j)))
    if epi is not None:
        gate, u, row = epi
        ops += [gate, u, row]
        specs += [pl.BlockSpec((m, bn), lambda j: (0, j)),
                  pl.BlockSpec((m, bn), lambda j: (0, j)),
                  pl.BlockSpec((1, bn), lambda j: (0, j))]
    kern = functools.partial(_mm_kernel, a_silu=a_silu, has_bias=bias is not None,
                             has_epi=epi is not None, precise=precise)
    return pl.pallas_call(
        kern, grid=(n // bn,), in_specs=specs,
        out_specs=pl.BlockSpec((m, bn), lambda j: (0, j)),
        out_shape=jax.ShapeDtypeStruct((m, n), out_dtype),
        compiler_params=_cparams(1), name=name)(*ops)


def _modulated(x, mod, m0):
    shift = mod[m0:m0 + 1]
    scale = mod[m0 + 1:m0 + 2]
    return _ln(x) * (1.0 + scale) + shift


def _ffn_kernel(x_ref, mod_ref, wi_ref, wo_ref, g_ref, b_ref, o_ref, u_ref, *, m0, alpha):
    x = x_ref[...]
    mod = mod_ref[0]
    h = _modulated(x, mod, m0).astype(BF)
    for c in range(D_FF // FFN_CHUNK):
        lo = c * FFN_CHUNK
        a = _dot(h, wi_ref[:, lo:lo + FFN_CHUNK])
        g = _dot(h, wi_ref[:, D_FF + lo:D_FF + lo + FFN_CHUNK])
        u_ref[:, lo:lo + FFN_CHUNK] = (g * _sigmoid(g) * a).astype(BF)
    acc = _dot(u_ref[...], wo_ref[...])
    y = alpha * x + (0.5 * mod[m0 + 2:m0 + 3]) * acc
    o_ref[...] = _ln(y) * g_ref[...] + b_ref[...]


def _ffn(x, mod, w_in, w_out, g, b, *, layer, half, m0, tm, group_tiles, alpha):
    n = x.shape[0]

    def stacked(w):
        return pl.BlockSpec((None, None) + w.shape[2:], lambda i: (layer, half, 0, 0),
                            pipeline_mode=pl.Buffered(1))

    kern = functools.partial(_ffn_kernel, m0=m0, alpha=alpha)
    return pl.pallas_call(
        kern, grid=(n // tm,),
        in_specs=[pl.BlockSpec((tm, D_MODEL), lambda i: (i, 0)),
                  pl.BlockSpec((1, N_MOD, D_MODEL), lambda i: (i // group_tiles, 0, 0)),
                  stacked(w_in), stacked(w_out),
                  _resident((1, D_MODEL)), _resident((1, D_MODEL))],
        out_specs=pl.BlockSpec((tm, D_MODEL), lambda i: (i, 0)),
        out_shape=jax.ShapeDtypeStruct((n, D_MODEL), F32),
        scratch_shapes=[pltpu.VMEM((tm, D_FF), BF)],
        compiler_params=_cparams(1), name="ffn")(x, mod, w_in, w_out, g, b)


def _inproj_kernel(x_ref, mod_ref, w_ref, b_ref, o_ref, *, m0):
    h = _modulated(x_ref[...], mod_ref[0], m0).astype(BF)
    lo = 0
    for width in INPROJ_CHUNKS:
        y = _dot(h, w_ref[:, lo:lo + width]) + b_ref[:, lo:lo + width]
        if lo < NEW_CONV:
            y = _sigmoid(y)
        o_ref[:, lo:lo + width] = y.astype(BF)
        lo += width


def _inproj(x, mod, w, b, *, tm, group_tiles):
    n = x.shape[0]
    return pl.pallas_call(
        functools.partial(_inproj_kernel, m0=3), grid=(n // tm,),
        in_specs=[pl.BlockSpec((tm, D_MODEL), lambda i: (i, 0)),
                  pl.BlockSpec((1, N_MOD, D_MODEL), lambda i: (i // group_tiles, 0, 0)),
                  _resident(w.shape), _resident((1, P_IN))],
        out_specs=pl.BlockSpec((tm, P_IN), lambda i: (i, 0)),
        out_shape=jax.ShapeDtypeStruct((n, P_IN), BF),
        compiler_params=_cparams(1), name="inproj")(x, mod, w, b)


def _prep_kernel(x_ref, prev_ref, next_ref, cw_ref, cb_ref, lg_ref, lb_ref, sw_ref, hw_ref, hb_ref,
                 ya_ref, yc_ref, hv_ref, h1_ref, h2_ref, buf_a, buf_c, buf_h, shifted, *, tm, seq_tiles):
    i = pl.program_id(0)
    pos = i % seq_tiles
    keep_prev = jnp.where(pos == 0, 0.0, 1.0)
    keep_next = jnp.where(pos == seq_tiles - 1, 0.0, 1.0)

    def fill(t, lo, hi):
        glu = t[:, 0:WA] * _sigmoid(t[:, WA:2 * WA])
        buf_a[lo:hi, :] = glu
        buf_c[lo:hi, :] = t[:, 3 * WA:4 * WA] * t[:, 4 * WA:5 * WA]
        buf_h[lo:hi, :] = t[:, 5 * WA:8 * WA]

    main = x_ref[...].astype(F32)
    fill(prev_ref[...].astype(F32) * keep_prev, 0, HALO)
    fill(main, HALO, HALO + tm)
    fill(next_ref[...].astype(F32) * keep_next, HALO + tm, 2 * HALO + tm)

    span = tm + SHIFT_SPAN
    for r in range(8):
        shifted[r] = buf_a[r:r + span, :]
    acc = jnp.zeros((tm, WA), F32) + cb_ref[...]
    for j in range(CONF_K):
        off = HALO - CONF_K // 2 + j
        base = off - off % 8
        acc = acc + cw_ref[j:j + 1, :] * shifted[off % 8, base:base + tm, :]
    u = _ln(acc) * lg_ref[...] + lb_ref[...]
    ya_ref[...] = (u * _sigmoid(u)).astype(BF)

    acc = jnp.zeros((tm, WC), F32)
    for j in range(SC_K):
        off = HALO - SC_K // 2 + j
        acc = acc + sw_ref[j:j + 1, :] * buf_c[off:off + tm, :]
    yc_ref[...] = (main[:, 2 * WA:3 * WA] * acc).astype(BF)

    acc = jnp.zeros((tm, 3 * WD), F32) + hb_ref[...]
    for j in range(HY_SHORT_K):
        off = HALO - HY_SHORT_K // 2 + j
        acc = acc + hw_ref[j:j + 1, :] * buf_h[off:off + tm, :]
    hv_ref[...] = acc[:, 0:WD].astype(BF)
    h1_ref[...] = acc[:, WD:2 * WD].astype(BF)
    h2_ref[...] = acc[:, 2 * WD:3 * WD].astype(BF)


def _prep(p, cw, cb, lg, lb, sw, hw, hb, *, tm, seq_len):
    n = p.shape[0]
    seq_tiles = seq_len // tm
    hb_per_tile = tm // HALO
    n_halo_blocks = n // HALO
    cblk = NEW_CONV // CONV_W
    kern = functools.partial(_prep_kernel, tm=tm, seq_tiles=seq_tiles)
    out_bf = jax.ShapeDtypeStruct((n, WD), BF)
    ospec = pl.BlockSpec((tm, WD), lambda i: (i, 0))
    return pl.pallas_call(
        kern, grid=(n // tm,),
        in_specs=[pl.BlockSpec((tm, CONV_W), lambda i: (i, cblk)),
                  pl.BlockSpec((HALO, CONV_W),
                               lambda i: (jnp.maximum(i * hb_per_tile - 1, 0), cblk)),
                  pl.BlockSpec((HALO, CONV_W),
                               lambda i: (jnp.minimum((i + 1) * hb_per_tile, n_halo_blocks - 1), cblk)),
                  _resident(cw.shape), _resident(cb.shape), _resident(lg.shape), _resident(lb.shape),
                  _resident(sw.shape), _resident(hw.shape), _resident(hb.shape)],
        out_specs=[ospec] * 5, out_shape=[out_bf] * 5,
        scratch_shapes=[pltpu.VMEM((tm + 2 * HALO, WA), F32),
                        pltpu.VMEM((tm + 2 * HALO, WC), F32),
                        pltpu.VMEM((tm + 2 * HALO, 3 * WD), F32),
                        pltpu.VMEM((8, tm + SHIFT_SPAN, WA), F32)],
        compiler_params=_cparams(1), name="prep")(p, p, p, cw, cb, lg, lb, sw, hw, hb)


def _na_kernel(q_ref, k0_ref, k1_ref, k2_ref, v0_ref, v1_ref, v2_ref, kc_ref, vc_ref, bias_ref, o_ref):
    q = q_ref[...] * ATTN_SCALE
    k = jnp.concatenate([k0_ref[...], k1_ref[...], k2_ref[...]], axis=0)
    v = jnp.concatenate([v0_ref[...], v1_ref[...], v2_ref[...]], axis=0)
    kc = kc_ref[...]
    vc = vc_ref[...]
    ones_nb = jnp.ones((v.shape[0], NA_HEAD_DIM), BF)
    ones_cx = jnp.ones((vc.shape[0], NA_HEAD_DIM), BF)

    def scores(h):
        sl = slice(h * NA_HEAD_DIM, (h + 1) * NA_HEAD_DIM)
        qh = q[:, sl]
        return _dot_nt(qh, k[:, sl]) + bias_ref[0, h], _dot_nt(qh, kc[:, sl])

    def weighted(p_nb, p_cx, sl):
        o = (_dot(p_nb, jnp.concatenate([v[:, sl], ones_nb], axis=1))
             + _dot(p_cx, jnp.concatenate([vc[:, sl], ones_cx], axis=1)))
        return o[:, :NA_HEAD_DIM] / o[:, NA_HEAD_DIM:NA_HEAD_DIM + 1]

    outs = []
    sc = {0: scores(0), 1: scores(1)}
    pending = None
    for h in range(NA_HEADS):
        sl = slice(h * NA_HEAD_DIM, (h + 1) * NA_HEAD_DIM)
        s_nb, s_cx = sc.pop(h)
        if h + 2 < NA_HEADS:
            sc[h + 2] = scores(h + 2)
        m = jnp.maximum(jnp.max(s_nb, axis=-1, keepdims=True), jnp.max(s_cx, axis=-1, keepdims=True))
        p_nb = jnp.exp((s_nb - m).astype(BF))
        p_cx = jnp.exp((s_cx - m).astype(BF))
        if pending is not None:
            outs.append(pending())
        pending = functools.partial(weighted, p_nb, p_cx, sl)
    outs.append(pending())
    o_ref[...] = jnp.concatenate(outs, axis=-1).astype(BF)


def _rpb_expand_kernel(rpb_ref, onehot_ref, o_ref):
    o_ref[...] = jnp.dot(rpb_ref[...], onehot_ref[...], precision=lax.Precision.HIGHEST,
                         preferred_element_type=F32)


def _na_bias_table(rpb, rows):
    assert rows >= NA_WIN_ROWS and rows % NA_QROWS == 0 and rows // NA_QROWS >= 3
    n_dr, n_dc = 2 * NA_WIN_ROWS - 1, 2 * NA_WIN_COLS - 1
    cols = np.arange(GRID_W)
    c0 = np.clip(cols - NA_WIN_COLS // 2, 0, GRID_W - NA_WIN_COLS)
    dc = cols[None, :] - cols[:, None] + (NA_WIN_COLS - 1)
    ok_c = (cols[None, :] >= c0[:, None]) & (cols[None, :] < c0[:, None] + NA_WIN_COLS)
    onehot = np.zeros((LANES, GRID_W, GRID_W), np.float32)
    qq, kk = np.nonzero(ok_c)
    onehot[dc[qq, kk], qq, kk] = 1.0
    rpb2 = jnp.pad(rpb.reshape(NA_HEADS * n_dr, n_dc).astype(F32),
                   ((0, 64 - NA_HEADS * n_dr), (0, LANES - n_dc)))
    t = pl.pallas_call(
        _rpb_expand_kernel, grid=(1,),
        in_specs=[_resident((64, LANES)), _resident((LANES, GRID_W * GRID_W))],
        out_specs=pl.BlockSpec((64, GRID_W * GRID_W), lambda i: (0, 0)),
        out_shape=jax.ShapeDtypeStruct((64, GRID_W * GRID_W), F32),
        compiler_params=_cparams(1), name="rpb_expand")(rpb2, jnp.asarray(onehot.reshape(LANES, GRID_W * GRID_W)))
    t = t[:NA_HEADS * n_dr].reshape(NA_HEADS, n_dr, GRID_W, GRID_W)
    full = jnp.concatenate(
        [jnp.concatenate([t[:, s - a + NA_WIN_ROWS - 1 - NA_QROWS] for s in range(3 * NA_QROWS)], axis=-1)
         for a in range(NA_QROWS)], axis=1)
    wr = NA_WIN_ROWS
    n_blk = rows // NA_QROWS
    tabs = []
    for blk in (0, 1, n_blk - 1):
        qr = blk * NA_QROWS + np.arange(NA_QROWS)
        kr = (blk - 1) * NA_QROWS + np.arange(3 * NA_QROWS)
        r0 = np.clip(qr - wr // 2, 0, rows - wr)
        ok_r = ((kr[None, :] >= r0[:, None]) & (kr[None, :] < r0[:, None] + wr)
                & (kr[None, :] >= 0) & (kr[None, :] < rows))
        ok = (ok_r[:, None, :, None] & ok_c[None, :, None, :]).reshape(NA_TOK, 3 * NA_TOK)
        tabs.append(jnp.where(jnp.asarray(ok)[None], full, NEG_INF))
    return jnp.stack(tabs)


def _na(p, pc, bias_tab, *, bsz, seq_len):
    t = seq_len // NA_TOK
    qb, kb, vb = NEW_Q // WB, NEW_Q // WB + 1, NEW_Q // WB + 2
    blk = (NA_TOK, WB)
    cblk = (pc.shape[0] // bsz, WB)

    def nbr(j, col):
        return pl.BlockSpec(blk, lambda b, i: (b * t + jnp.clip(i - 1 + j, 0, t - 1), col))

    return pl.pallas_call(
        _na_kernel, grid=(bsz, t),
        in_specs=[pl.BlockSpec(blk, lambda b, i: (b * t + i, qb)),
                  nbr(0, kb), nbr(1, kb), nbr(2, kb), nbr(0, vb), nbr(1, vb), nbr(2, vb),
                  pl.BlockSpec(cblk, lambda b, i: (b, kb)),
                  pl.BlockSpec(cblk, lambda b, i: (b, vb)),
                  pl.BlockSpec((1, NA_HEADS, NA_TOK, 3 * NA_TOK),
                               lambda b, i: (jnp.where(i == 0, 0, jnp.where(i == t - 1, 2, 1)), 0, 0, 0))],
        out_specs=pl.BlockSpec(blk, lambda b, i: (b * t + i, 0)),
        out_shape=jax.ShapeDtypeStruct((bsz * seq_len, WB), BF),
        compiler_params=_cparams(2), name="na")(p, p, p, p, p, p, p, pc, pc, bias_tab)


def _ctx_attn_kernel(q_ref, k_ref, v_ref, o_ref):
    q = q_ref[...] * ATTN_SCALE
    k = k_ref[...]
    v = v_ref[...]
    outs = []
    for h in range(NA_HEADS):
        sl = slice(h * NA_HEAD_DIM, (h + 1) * NA_HEAD_DIM)
        s = _dot_nt(q[:, sl], k[:, sl])
        m = jnp.max(s, axis=-1, keepdims=True)
        p = jnp.exp(s - m)
        den = jnp.sum(p, axis=-1, keepdims=True)
        outs.append(_dot(p.astype(BF), v[:, sl]) / den)
    o_ref[...] = jnp.concatenate(outs, axis=-1).astype(BF)


def _ctx_attn(pc, *, bsz, ctx_len):
    qb, kb, vb = NEW_Q // WB, NEW_Q // WB + 1, NEW_Q // WB + 2
    blk = (ctx_len, WB)
    return pl.pallas_call(
        _ctx_attn_kernel, grid=(bsz,),
        in_specs=[pl.BlockSpec(blk, lambda b: (b, qb)), pl.BlockSpec(blk, lambda b: (b, kb)),
                  pl.BlockSpec(blk, lambda b: (b, vb))],
        out_specs=pl.BlockSpec(blk, lambda b: (b, 0)),
        out_shape=jax.ShapeDtypeStruct((bsz * ctx_len, WB), BF),
        compiler_params=_cparams(1), name="ctx_attn")(pc, pc, pc)


def _filt_kernel(zf_ref, zb_ref, w1_ref, b1_ref, w2_ref, b2_ref, fr_ref, w3_ref, dl_ref, k_ref, ss_ref,
                 *, tm):
    i = pl.program_id(0)
    hp = lax.Precision.HIGHEST
    zf = zf_ref[...]
    zb = zb_ref[...]
    w1 = w1_ref[...]
    pre = jnp.concatenate([jnp.dot(zf, w1, precision=hp, preferred_element_type=F32),
                           jnp.dot(zb, w1, precision=hp, preferred_element_type=F32)], axis=-1)
    h = jnp.sin(fr_ref[0:1, :] * (pre + b1_ref[...]))
    h = jnp.sin(fr_ref[1:2, :] * (jnp.dot(h, w2_ref[...], precision=hp, preferred_element_type=F32)
                                  + b2_ref[...]))
    k = _dot(h.astype(BF), w3_ref[...])
    dec_f = jnp.exp(-zf[:, 0:1] * dl_ref[...])
    dec_b = jnp.exp(-zb[:, 0:1] * dl_ref[...])
    k = k * jnp.concatenate([dec_f] * HY_ORDER + [dec_b] * HY_ORDER, axis=-1)
    is_row0 = (i * tm + lax.broadcasted_iota(jnp.int32, (tm, 1), 0)) == 0
    kf = k[:, :HY_ORDER * WD]
    kb = jnp.where(is_row0, 0.0, k[:, HY_ORDER * WD:])
    k_ref[0] = kf.astype(BF)
    k_ref[1] = kb.astype(BF)

    @pl.when(i == 0)
    def _():
        ss_ref[...] = jnp.zeros_like(ss_ref)

    ss_ref[...] += jnp.sum(kf * kf + kb * kb, axis=0, keepdims=True)


def _pos_features(t, length):
    t_norm = t / max(length - 1, 1)
    bands = jnp.linspace(1e-4, HY_PE_BANDS - 1, HY_PE_BANDS, dtype=F32)
    ang = (2.0 * math.pi / length) * t[:, None] * bands[None, :]
    z = jnp.concatenate([t_norm[:, None], jnp.cos(ang), -jnp.sin(ang)], axis=-1)
    return jnp.pad(z, ((0, 0), (0, LANES - HY_PE_DIM)))


def _block_diag(a, b):
    return jnp.concatenate([jnp.concatenate([a, jnp.zeros((a.shape[0], b.shape[1]), a.dtype)], axis=1),
                            jnp.concatenate([jnp.zeros((b.shape[0], a.shape[1]), a.dtype), b], axis=1)],
                           axis=0)


def _hyena_filters(length, w1, b1, w2, b2, freq, w3):
    t = jnp.arange(length, dtype=F32)
    zf = _pos_features(t, length)
    zb = _pos_features(length - t, length)
    w1p = jnp.pad(w1, ((0, LANES - HY_PE_DIM), (0, 0)))
    w3r = w3.reshape(HY_FILT_HID, HY_ORDER, 2, WD)
    w3bd = _block_diag(w3r[:, :, 0].reshape(HY_FILT_HID, HY_ORDER * WD),
                       w3r[:, :, 1].reshape(HY_FILT_HID, HY_ORDER * WD)).astype(BF)
    w2bd = _block_diag(w2, w2)
    b1d = jnp.concatenate([b1, b1])[None]
    b2d = jnp.concatenate([b2, b2])[None]
    frd = jnp.concatenate([freq, freq], axis=1)
    deltas = jnp.abs(jnp.linspace(math.log(HY_DECAY_TARGET) / HY_SLOW_DECAY,
                                  math.log(HY_DECAY_TARGET) / HY_FAST_DECAY, WD, dtype=F32))[None]
    tm = min(512, length)
    nf = HY_ORDER * WD
    zspec = pl.BlockSpec((tm, LANES), lambda i: (i, 0))
    k, ss = pl.pallas_call(
        functools.partial(_filt_kernel, tm=tm), grid=(length // tm,),
        in_specs=[zspec, zspec, _resident(w1p.shape), _resident(b1d.shape), _resident(w2bd.shape),
                  _resident(b2d.shape), _resident(frd.shape), _resident(w3bd.shape),
                  _resident(deltas.shape)],
        out_specs=[pl.BlockSpec((2, tm, nf), lambda i: (0, i, 0)), pl.BlockSpec((1, nf), lambda i: (0, 0))],
        out_shape=[jax.ShapeDtypeStruct((2, length, nf), BF), jax.ShapeDtypeStruct((1, nf), F32)],
        compiler_params=_cparams(1, "arbitrary"), name="hyena_filter")(zf, zb, w1p, b1d, w2bd, b2d, frd, w3bd, deltas)
    return k.reshape(2 * length, nf), ss


def _cis(num, den):
    ang = (-2.0 * math.pi / den) * (num % den).astype(F32)
    return jnp.cos(ang), jnp.sin(ang)


def _stack(re, im):
    return jnp.concatenate([jnp.concatenate([re, -im], axis=1), jnp.concatenate([im, re], axis=1)], axis=0)


def _dft_tables(n1):
    n = n1 * FFT_N2
    i1 = jnp.arange(n1)
    fr, fi = _cis(i1[:, None] * i1[None, :], n1)
    half = n1 // 2
    w_fwd = _stack(fr[:, :half], fi[:, :half]).astype(BF)
    kept = half + min(DFT_B_KB, n1)
    w_real = jnp.concatenate([fr[:kept], fi[:kept]], axis=0).astype(BF)
    w_inv = (_stack(fr[:half, :], -fi[:half, :]) / n).astype(BF)
    i2 = jnp.arange(FFT_N2)
    f2r, f2i = _cis(i2[:, None] * i2[None, :], FFT_N2)
    twr, twi = _cis(i1[:, None] * i2[None, :], n)
    gr = f2r[None] * twr[:, None, :] - f2i[None] * twi[:, None, :]
    gi = f2r[None] * twi[:, None, :] + f2i[None] * twr[:, None, :]
    gs = jnp.concatenate([jnp.concatenate([gr, -gi], axis=2), jnp.concatenate([gi, gr], axis=2)], axis=1)
    gs = gs.astype(BF)
    return w_fwd, w_real, w_inv, gs


def _dft_tables_direct(length):
    n = 2 * length
    i = jnp.arange(n)
    fr, fi = _cis(i[:, None] * i[None, :], n)
    w_fwd = _stack(fr[:, :length], fi[:, :length]).astype(BF)
    w_real = jnp.concatenate([fr, fi], axis=0).astype(BF)
    w_inv = (_stack(fr[:length, :], -fi[:length, :]) / n).astype(BF)
    return w_fwd, w_real, w_inv


def _stage_a_kernel(*refs, nb, n_half, has_epi):
    w_ref, x_ref = refs[0], refs[1]
    m, k = w_ref.shape
    if has_epi:
        g_ref, u_ref, row_ref, o_ref = refs[2:6]
        scratch = refs[6:]
    else:
        o_ref = refs[2]
        scratch = refs[3:]
    xs, os_ = scratch[:n_half], scratch[n_half:]
    w = w_ref[...]
    for l in range(n_half):
        lanes = slice(l * LANES, (l + 1) * LANES)
        xs[l][...] = x_ref[:, :, lanes].astype(F32).reshape(k * nb, LANES)
        for j in range(nb):
            os_[l][pl.ds(j, m, stride=nb), :] = _dot(w, xs[l][pl.ds(j, k, stride=nb), :].astype(BF))
        acc = os_[l][...].reshape(m, nb, LANES)
        if has_epi:
            acc = g_ref[:, :, lanes].astype(F32) * (
                acc + u_ref[:, :, lanes].astype(F32) * row_ref[:, lanes].reshape(1, 1, LANES))
        o_ref[:, :, lanes] = acc.astype(o_ref.dtype)


def _stage_a(w, x, *, name, epi=None, nb=16, lane_blk=256):
    m, k = w.shape
    c = x.shape[2]
    n_half = lane_blk // LANES
    xblk = pl.BlockSpec((k, nb, lane_blk), lambda j, l: (0, j, l))
    oblk = pl.BlockSpec((m, nb, lane_blk), lambda j, l: (0, j, l))
    ops, specs = [w, x], [_resident((m, k)), xblk]
    if epi is not None:
        ops += list(epi)
        specs += [oblk, oblk, pl.BlockSpec((1, lane_blk), lambda j, l: (0, l))]
    scratch = ([pltpu.VMEM((k * nb, LANES), F32)] * n_half + [pltpu.VMEM((m * nb, LANES), F32)] * n_half)
    return pl.pallas_call(
        functools.partial(_stage_a_kernel, nb=nb, n_half=n_half, has_epi=epi is not None),
        grid=(FFT_N2 // nb, c // lane_blk), in_specs=specs, out_specs=oblk, scratch_shapes=scratch,
        out_shape=jax.ShapeDtypeStruct((m, FFT_N2, c), BF), compiler_params=_cparams(2), name=name)(*ops)


def _stage_b_kernel(a_ref, fa_ref, fb_ref, gs_ref, ss_ref, o_ref, *, kb, n_blk):
    scale = lax.rsqrt(ss_ref[...] + 1e-6)

    def run(filt):
        yhs = []
        for j in range(kb):
            a = jnp.concatenate([a_ref[0, j], a_ref[1, j]], axis=0)
            yhs.append(_dot(gs_ref[j], jnp.concatenate([a, filt(j)], axis=1)))
        zs = []
        for yh in yhs:
            y, h = yh[:, :WD], yh[:, WD:] * scale
            yr, yi = y[:FFT_N2], y[FFT_N2:]
            hr, hi = h[:FFT_N2], h[FFT_N2:]
            zs.append(jnp.concatenate([yr * hr - yi * hi, yr * hi + yi * hr], axis=0).astype(BF))
        for j in range(kb):
            b = lax.dot_general(gs_ref[j], zs[j], (((0,), (0,)), ((), ())), preferred_element_type=F32)
            o_ref[0, j] = b[:FFT_N2].astype(BF)
            o_ref[1, j] = b[FFT_N2:].astype(BF)

    def direct(j):
        return jnp.concatenate([fa_ref[0, j], fa_ref[1, j]], axis=0)

    def mirrored(j):
        ref, r = (fb_ref, 0) if j == 0 else (fa_ref, kb - j)
        return jnp.concatenate([ref[0, r], -ref[1, r]], axis=0)

    i = pl.program_id(0)

    @pl.when(i < n_blk // 2)
    def _():
        run(direct)

    @pl.when(i >= n_blk // 2)
    def _():
        run(mirrored)


def _stage_b(a, filt_a, gs, ss, *, order):
    n1 = a.shape[1]
    kb = min(DFT_B_KB, n1)
    n_blk = n1 // kb
    assert n_blk % 2 == 0 and filt_a.shape[1] == n1 // 2 + kb
    dblk = pl.BlockSpec((2, kb, FFT_N2, WD), lambda i: (0, i, 0, 0))
    fa = pl.BlockSpec((2, kb, FFT_N2, WD), lambda i: (0, jnp.where(i < n_blk // 2, i, n_blk - 1 - i), 0, order))
    fb = pl.BlockSpec((2, kb, FFT_N2, WD), lambda i: (0, jnp.where(i < n_blk // 2, i, n_blk - i), 0, order))
    gblk = pl.BlockSpec((kb, 2 * FFT_N2, 2 * FFT_N2), lambda i: (i, 0, 0))
    return pl.pallas_call(
        functools.partial(_stage_b_kernel, kb=kb, n_blk=n_blk), grid=(n_blk,),
        in_specs=[dblk, fa, fb, gblk, pl.BlockSpec((1, WD), lambda i: (0, order))], out_specs=dblk,
        out_shape=jax.ShapeDtypeStruct((2, n1, FFT_N2, WD), BF),
        compiler_params=_cparams(1), name="dft_b")(a, filt_a, filt_a, gs, ss)


def _cmul_kernel(x_ref, h_ref, ss_ref, o_ref, *, half):
    scale = lax.rsqrt(ss_ref[...] + 1e-6)
    xr, xi = x_ref[:half, :], x_ref[half:, :]
    hr, hi = h_ref[:half, :] * scale, h_ref[half:, :] * scale
    o_ref[:half, :] = (xr * hr - xi * hi).astype(BF)
    o_ref[half:, :] = (xr * hi + xi * hr).astype(BF)


def _cmul(x, h, ss, lane_blk):
    rows = x.shape[0]
    return pl.pallas_call(
        functools.partial(_cmul_kernel, half=rows // 2), grid=(1,),
        in_specs=[_resident(x.shape), pl.BlockSpec((rows, WD), lambda i: (0, lane_blk)),
                  pl.BlockSpec((1, WD), lambda i: (0, lane_blk))],
        out_specs=pl.BlockSpec(x.shape, lambda i: (0, 0)),
        out_shape=jax.ShapeDtypeStruct(x.shape, BF), compiler_params=_cparams(1), name="ctx_cmul")(x, h, ss)


def _hyena_long(hv, hx, kern, ss, bias, tables, *, bsz, seq_len):
    assert bsz == 2
    n1 = 2 * seq_len // FFT_N2
    w_fwd, w_real, w_inv, gs = tables
    view = (bsz * seq_len // FFT_N2, FFT_N2, WD)
    filt_a = _stage_a(w_real, kern.reshape(n1, FFT_N2, HY_ORDER * WD), name="dft_a_filter")
    filt_a = filt_a.reshape(2, filt_a.shape[0] // 2, FFT_N2, HY_ORDER * WD)
    u = hv.reshape(view)
    for o in range(HY_ORDER):
        a = _stage_a(w_fwd, u, name="dft_a_fwd").reshape(2, n1, FFT_N2, WD)
        b = _stage_b(a, filt_a, gs, ss, order=o).reshape(2 * n1, FFT_N2, WD)
        u = _stage_a(w_inv, b, name="dft_a_inv", epi=(hx[o].reshape(view), u, bias[o][None]))
    return u.reshape(bsz * seq_len, WD)


def _hyena_short_seq(hv, hx, kern, ss, bias, tables, *, bsz):
    assert bsz == 2
    w_fwd, w_real, w_inv = tables
    spec = _mm(w_real, kern, bn=HY_ORDER * WD, out_dtype=F32, name="ctx_dft_filter")
    u = hv
    for o in range(HY_ORDER):
        x = _mm(w_fwd, u, bn=WD, out_dtype=F32, name="ctx_dft_fwd")
        z = _cmul(x, spec, ss, o)
        u = _mm(w_inv, z, bn=WD, out_dtype=BF, name="ctx_dft_inv", epi=(hx[o], u, bias[o][None]))
    return u


def _merge_kernel(x_ref, mod_ref, ya_ref, yb_ref, yc_ref, yd_ref, gate_ref, wp_ref, wo_ref, bo_ref,
                  g_ref, b_ref, o_ref, *, alpha):
    x = x_ref[...]
    m = jnp.zeros(x.shape, F32)
    for br, y_ref in enumerate((ya_ref, yb_ref, yc_ref, yd_ref)):
        gate = gate_ref[:, br * D_MODEL:(br + 1) * D_MODEL].astype(F32)
        m = m + gate * _dot(y_ref[...], wp_ref[br])
    out = _dot(m.astype(BF), wo_ref[...]) + bo_ref[...]
    y = alpha * x + mod_ref[0][5:6] * out
    o_ref[...] = _ln(y) * g_ref[...] + b_ref[...]


def _merge(x, mod, ya, yb, yc, yd, p, wp, wo, bo, g, b, *, tm, group_tiles, alpha):
    n = x.shape[0]
    yspec = pl.BlockSpec((tm, WD), lambda i: (i, 0))
    return pl.pallas_call(
        functools.partial(_merge_kernel, alpha=alpha), grid=(n // tm,),
        in_specs=[pl.BlockSpec((tm, D_MODEL), lambda i: (i, 0)),
                  pl.BlockSpec((1, N_MOD, D_MODEL), lambda i: (i // group_tiles, 0, 0)),
                  yspec, yspec, yspec, yspec,
                  pl.BlockSpec((tm, N_BRANCH * D_MODEL), lambda i: (i, 0)),
                  _resident(wp.shape), _resident(wo.shape), _resident((1, D_MODEL)),
                  _resident((1, D_MODEL)), _resident((1, D_MODEL))],
        out_specs=pl.BlockSpec((tm, D_MODEL), lambda i: (i, 0)),
        out_shape=jax.ShapeDtypeStruct((n, D_MODEL), F32),
        compiler_params=_cparams(1), name="merge")(x, mod, ya, yb, yc, yd, p, wp, wo, bo, g, b)


def kernel(x, c, ctx, c_ctx, w_mod, b_mod, post_ln_g, post_ln_b, ffn_w_in, ffn_w_out, w_in, b_in,
           conf_dw_w, conf_dw_b, conf_ln_g, conf_ln_b, conf_w_proj, na_rpb, na_w_proj, sc_conv_w,
           sc_w_proj, hy_sconv_w, hy_sconv_b, hy_w1, hy_b1, hy_w2, hy_b2, hy_freq, hy_w3, hy_bias,
           hy_w_proj, w_out, b_out):
    bsz, n_lat, _ = x.shape
    ctx_len = ctx.shape[1]
    depth = w_mod.shape[0]
    alpha = (2 * depth) ** 0.25
    rows = n_lat // GRID_W

    tm = 512
    tmc = min(tm, ctx_len)
    tf = 1024 if n_lat % 1024 == 0 else tm
    tfc = min(tf, bsz * ctx_len)
    tp = 512
    tpc = min(tp, ctx_len)

    xl = x.reshape(bsz * n_lat, D_MODEL)
    xc = ctx.reshape(bsz * ctx_len, D_MODEL)
    lat_tiles = n_lat // tm
    ctx_tiles = bsz * ctx_len // tmc

    tables = _dft_tables(2 * n_lat // FFT_N2)
    tables_c = _dft_tables_direct(ctx_len)

    cond = jnp.concatenate([c, c_ctx[None], jnp.zeros((SUBLANES - bsz - 1, D_MODEL), F32)], axis=0)

    ffn_wi = ffn_w_in.astype(BF)
    ffn_wo = ffn_w_out.astype(BF)

    for l in range(depth):
        last = l == depth - 1
        mod_all = _mm(cond, w_mod, bn=1024, out_dtype=F32, name="adaln_mod", bias=b_mod[l][None], a_silu=True,
                      b_layer=l, precise=True)
        mod_all = mod_all.reshape(SUBLANES, N_MOD, D_MODEL)
        mod = mod_all[:bsz]
        mod_c = mod_all[bsz:bsz + 1]

        wl = w_in[l]
        w_perm = jnp.concatenate([wl[:, OFF_G:], wl[:, OFF_A:OFF_Q], wl[:, OFF_SB:OFF_G],
                                  wl[:, OFF_Q:OFF_SB]], axis=1).astype(BF)
        bl = b_in[l]
        b_perm = jnp.concatenate([bl[OFF_G:], bl[OFF_A:OFF_Q], bl[OFF_SB:OFF_G], bl[OFF_Q:OFF_SB]])[None]
        ln_g = post_ln_g[l][:, None, :]
        ln_b = post_ln_b[l][:, None, :]
        wp = jnp.stack([conf_w_proj[l], na_w_proj[l], sc_w_proj[l], hy_w_proj[l]]).astype(BF)
        wo = w_out[l].astype(BF)
        conv_args = (conf_dw_w[l], conf_dw_b[l][None], conf_ln_g[l][None], conf_ln_b[l][None],
                     sc_conv_w[l], hy_sconv_w[l], hy_sconv_b[l][None])
        filt_args = (hy_w1[l], hy_b1[l], hy_w2[l], hy_b2[l], hy_freq[l], hy_w3[l])

        xl = _ffn(xl, mod, ffn_wi, ffn_wo, ln_g[0], ln_b[0], layer=l, half=0, m0=0, tm=tf,
                  group_tiles=n_lat // tf, alpha=alpha)
        xc = _ffn(xc, mod_c, ffn_wi, ffn_wo, ln_g[0], ln_b[0], layer=l, half=0, m0=0, tm=tfc,
                  group_tiles=bsz * ctx_len // tfc, alpha=alpha)

        p = _inproj(xl, mod, w_perm, b_perm, tm=tm, group_tiles=lat_tiles)
        pc = _inproj(xc, mod_c, w_perm, b_perm, tm=tmc, group_tiles=ctx_tiles)

        ya, yc, hv, h1, h2 = _prep(p, *conv_args, tm=tp, seq_len=n_lat)
        attn = _na(p, pc, _na_bias_table(na_rpb[l], rows), bsz=bsz, seq_len=n_lat)
        kern, ss = _hyena_filters(n_lat, *filt_args)
        yd = _hyena_long(hv, (h1, h2), kern, ss, hy_bias[l], tables, bsz=bsz, seq_len=n_lat)
        xl = _merge(xl, mod, ya, attn, yc, yd, p, wp, wo, b_out[l][None], ln_g[1], ln_b[1],
                    tm=tm, group_tiles=lat_tiles, alpha=alpha)

        if not last:
            ya, yc, hv, h1, h2 = _prep(pc, *conv_args, tm=tpc, seq_len=ctx_len)
            attn_c = _ctx_attn(pc, bsz=bsz, ctx_len=ctx_len)
            kern_c, ss_c = _hyena_filters(ctx_len, *filt_args)
            yd = _hyena_short_seq(hv, (h1, h2), kern_c, ss_c, hy_bias[l], tables_c, bsz=bsz)
            xc = _merge(xc, mod_c, ya, attn_c, yc, yd, pc, wp, wo, b_out[l][None], ln_g[1], ln_b[1],
                        tm=tmc, group_tiles=ctx_tiles, alpha=alpha)

        xl = _ffn(xl, mod, ffn_wi, ffn_wo, ln_g[2], ln_b[2], layer=l, half=1, m0=6, tm=tf,
                  group_tiles=n_lat // tf, alpha=alpha)
        if not last:
            xc = _ffn(xc, mod_c, ffn_wi, ffn_wo, ln_g[2], ln_b[2], layer=l, half=1, m0=6, tm=tfc,
                      group_tiles=bsz * ctx_len // tfc, alpha=alpha)

    return xl.reshape(bsz, n_lat, D_MODEL)
```

```python
import functools
import math

import jax
import jax.numpy as jnp
import numpy as np
from jax import lax
from jax.experimental import pallas as pl
from jax.experimental.pallas import tpu as pltpu

D_MODEL = 1024
GRID_W = 64
N_BRANCH = 4
WA = 256
CONF_K = 31
NA_HEADS = 4
NA_HEAD_DIM = 64
WB = NA_HEADS * NA_HEAD_DIM
NA_WIN_ROWS = 8
NA_WIN_COLS = 16
ATTN_SCALE = NA_HEAD_DIM ** -0.5
WC = 256
SC_K = 3
WD = 256
HY_ORDER = 2
HY_SHORT_K = 3
HY_PE_BANDS = 16
HY_PE_DIM = 1 + 2 * HY_PE_BANDS
HY_FILT_HID = 64
HY_FAST_DECAY = 0.3
HY_SLOW_DECAY = 1.5
HY_DECAY_TARGET = 1e-2
D_FF = 2816
N_MOD = 9
LN_EPS = 1e-5

OFF_A = 0
OFF_Q = OFF_A + 2 * WA
OFF_K = OFF_Q + WB
OFF_V = OFF_K + WB
OFF_SB = OFF_V + WB
OFF_SC = OFF_SB + WC
OFF_SX = OFF_SC + WC
OFF_HV = OFF_SX + WC
OFF_G = OFF_HV + (1 + HY_ORDER) * WD
P_IN = OFF_G + N_BRANCH * D_MODEL

NEW_G = 0
NEW_CONV = N_BRANCH * D_MODEL
CONV_W = 2 * WA + 3 * WC + 3 * WD
NEW_Q = NEW_CONV + CONV_W

BF = jnp.bfloat16
F32 = jnp.float32

LANES = 128
SUBLANES = 8
VMEM_BYTES = 64 * 1024 * 1024
VMEM_LIMIT_BYTES = VMEM_BYTES - 6 * 1024 * 1024
FFN_CHUNK = 256
INPROJ_CHUNKS = (1024,) * 4 + (768, 768, 768, 512)
HALO = 16
SHIFT_SPAN = 24
NA_QROWS = 4
NA_TOK = NA_QROWS * GRID_W
FFT_N2 = 128
DFT_B_KB = 16
NEG_INF = -1e30


def _cparams(n_axes, semantics="parallel"):
    return pltpu.CompilerParams(dimension_semantics=(semantics,) * n_axes,
                                vmem_limit_bytes=VMEM_LIMIT_BYTES)


def _resident(shape):
    nd = len(shape)
    return pl.BlockSpec(shape, lambda *_: (0,) * nd, pipeline_mode=pl.Buffered(1))


def _ln(x):
    mu = jnp.mean(x, axis=-1, keepdims=True)
    xc = x - mu
    var = jnp.mean(xc * xc, axis=-1, keepdims=True)
    return xc * lax.rsqrt(var + LN_EPS)


def _sigmoid(x):
    return 0.5 * jnp.tanh(0.5 * x) + 0.5


def _dot(a, b):
    return jnp.dot(a, b, preferred_element_type=F32)


def _dot_nt(a, b):
    return lax.dot_general(a, b, (((1,), (1,)), ((), ())), preferred_element_type=F32)


def _mm_kernel(*refs, a_silu, has_bias, has_epi, precise):
    a_ref, b_ref = refs[0], refs[1]
    pos = 2
    a = a_ref[...]
    if a_silu:
        a = a.astype(F32)
        a = a * _sigmoid(a)
    if precise:
        acc = jnp.dot(a.astype(F32), b_ref[...].astype(F32), precision=lax.Precision.HIGHEST,
                      preferred_element_type=F32)
    else:
        acc = _dot(a.astype(BF), b_ref[...].astype(BF))
    if has_bias:
        acc = acc + refs[pos][...]
        pos += 1
    if has_epi:
        gate_ref, u_ref, row_ref = refs[pos], refs[pos + 1], refs[pos + 2]
        pos += 3
        acc = gate_ref[...].astype(F32) * (acc + u_ref[...].astype(F32) * row_ref[...])
    o_ref = refs[pos]
    o_ref[...] = acc.astype(o_ref.dtype)


def _mm(a, b, *, bn, out_dtype, name, bias=None, epi=None, a_silu=False, b_layer=None, precise=False):
    m, k = a.shape
    n = b.shape[-1]
    bn = min(bn, n)
    ops = [a, b]
    if b_layer is None:
        bspec = pl.BlockSpec((k, bn), lambda j: (0, j))
    else:
        bspec = pl.BlockSpec((None, k, bn), lambda j: (b_layer, 0, j))
    specs = [_resident((m, k)), bspec]
    if bias is not None:
        ops.append(bias)
        specs.append(pl.BlockSpec((1, bn), lambda j: (0, j)))
    if epi is not None:
        gate, u, row = epi
        ops += [gate, u, row]
        specs += [pl.BlockSpec((m, bn), lambda j: (0, j)),
                  pl.BlockSpec((m, bn), lambda j: (0, j)),
                  pl.BlockSpec((1, bn), lambda j: (0, j))]
    kern = functools.partial(_mm_kernel, a_silu=a_silu, has_bias=bias is not None,
                             has_epi=epi is not None, precise=precise)
    return pl.pallas_call(
        kern, grid=(n // bn,), in_specs=specs,
        out_specs=pl.BlockSpec((m, bn), lambda j: (0, j)),
        out_shape=jax.ShapeDtypeStruct((m, n), out_dtype),
        compiler_params=_cparams(1), name=name)(*ops)


def _modulated(x, mod, m0):
    shift = mod[m0:m0 + 1]
    scale = mod[m0 + 1:m0 + 2]
    return _ln(x) * (1.0 + scale) + shift


def _ffn_kernel(x_ref, mod_ref, wi_ref, wo_ref, g_ref, b_ref, o_ref, u_ref, *, m0, alpha):
    x = x_ref[...]
    mod = mod_ref[0]
    h = _modulated(x, mod, m0).astype(BF)
    for c in range(D_FF // FFN_CHUNK):
        lo = c * FFN_CHUNK
        a = _dot(h, wi_ref[:, lo:lo + FFN_CHUNK])
        g = _dot(h, wi_ref[:, D_FF + lo:D_FF + lo + FFN_CHUNK])
        u_ref[:, lo:lo + FFN_CHUNK] = (g * _sigmoid(g) * a).astype(BF)
    acc = _dot(u_ref[...], wo_ref[...])
    y = alpha * x + (0.5 * mod[m0 + 2:m0 + 3]) * acc
    o_ref[...] = _ln(y) * g_ref[...] + b_ref[...]


def _ffn(x, mod, w_in, w_out, g, b, *, layer, half, m0, tm, group_tiles, alpha):
    n = x.shape[0]

    def stacked(w):
        return pl.BlockSpec((None, None) + w.shape[2:], lambda i: (layer, half, 0, 0),
                            pipeline_mode=pl.Buffered(1))

    kern = functools.partial(_ffn_kernel, m0=m0, alpha=alpha)
    return pl.pallas_call(
        kern, grid=(n // tm,),
        in_specs=[pl.BlockSpec((tm, D_MODEL), lambda i: (i, 0)),
                  pl.BlockSpec((1, N_MOD, D_MODEL), lambda i: (i // group_tiles, 0, 0)),
                  stacked(w_in), stacked(w_out),
                  _resident((1, D_MODEL)), _resident((1, D_MODEL))],
        out_specs=pl.BlockSpec((tm, D_MODEL), lambda i: (i, 0)),
        out_shape=jax.ShapeDtypeStruct((n, D_MODEL), F32),
        scratch_shapes=[pltpu.VMEM((tm, D_FF), BF)],
        compiler_params=_cparams(1), name="ffn")(x, mod, w_in, w_out, g, b)


def _inproj_kernel(x_ref, mod_ref, w_ref, b_ref, o_ref, *, m0):
    h = _modulated(x_ref[...], mod_ref[0], m0).astype(BF)
    lo = 0
    for width in INPROJ_CHUNKS:
        y = _dot(h, w_ref[:, lo:lo + width]) + b_ref[:, lo:lo + width]
        if lo < NEW_CONV:
            y = _sigmoid(y)
        o_ref[:, lo:lo + width] = y.astype(BF)
        lo += width


def _inproj(x, mod, w, b, *, tm, group_tiles):
    n = x.shape[0]
    return pl.pallas_call(
        functools.partial(_inproj_kernel, m0=3), grid=(n // tm,),
        in_specs=[pl.BlockSpec((tm, D_MODEL), lambda i: (i, 0)),
                  pl.BlockSpec((1, N_MOD, D_MODEL), lambda i: (i // group_tiles, 0, 0)),
                  _resident(w.shape), _resident((1, P_IN))],
        out_specs=pl.BlockSpec((tm, P_IN), lambda i: (i, 0)),
        out_shape=jax.ShapeDtypeStruct((n, P_IN), BF),
        compiler_params=_cparams(1), name="inproj")(x, mod, w, b)


def _prep_kernel(x_ref, prev_ref, next_ref, cw_ref, cb_ref, lg_ref, lb_ref, sw_ref, hw_ref, hb_ref,
                 ya_ref, yc_ref, hv_ref, h1_ref, h2_ref, buf_a, buf_c, buf_h, shifted, *, tm, seq_tiles):
    i = pl.program_id(0)
    pos = i % seq_tiles
    keep_prev = jnp.where(pos == 0, 0.0, 1.0)
    keep_next = jnp.where(pos == seq_tiles - 1, 0.0, 1.0)

    def fill(t, lo, hi):
        glu = t[:, 0:WA] * _sigmoid(t[:, WA:2 * WA])
        buf_a[lo:hi, :] = glu
        buf_c[lo:hi, :] = t[:, 3 * WA:4 * WA] * t[:, 4 * WA:5 * WA]
        buf_h[lo:hi, :] = t[:, 5 * WA:8 * WA]

    main = x_ref[...].astype(F32)
    fill(prev_ref[...].astype(F32) * keep_prev, 0, HALO)
    fill(main, HALO, HALO + tm)
    fill(next_ref[...].astype(F32) * keep_next, HALO + tm, 2 * HALO + tm)

    span = tm + SHIFT_SPAN
    for r in range(8):
        shifted[r] = buf_a[r:r + span, :]
    acc = jnp.zeros((tm, WA), F32) + cb_ref[...]
    for j in range(CONF_K):
        off = HALO - CONF_K // 2 + j
        base = off - off % 8
        acc = acc + cw_ref[j:j + 1, :] * shifted[off % 8, base:base + tm, :]
    u = _ln(acc) * lg_ref[...] + lb_ref[...]
    ya_ref[...] = (u * _sigmoid(u)).astype(BF)

    acc = jnp.zeros((tm, WC), F32)
    for j in range(SC_K):
        off = HALO - SC_K // 2 + j
        acc = acc + sw_ref[j:j + 1, :] * buf_c[off:off + tm, :]
    yc_ref[...] = (main[:, 2 * WA:3 * WA] * acc).astype(BF)

    acc = jnp.zeros((tm, 3 * WD), F32) + hb_ref[...]
    for j in range(HY_SHORT_K):
        off = HALO - HY_SHORT_K // 2 + j
        acc = acc + hw_ref[j:j + 1, :] * buf_h[off:off + tm, :]
    hv_ref[...] = acc[:, 0:WD].astype(BF)
    h1_ref[...] = acc[:, WD:2 * WD].astype(BF)
    h2_ref[...] = acc[:, 2 * WD:3 * WD].astype(BF)


def _prep(p, cw, cb, lg, lb, sw, hw, hb, *, tm, seq_len):
    n = p.shape[0]
    seq_tiles = seq_len // tm
    hb_per_tile = tm // HALO
    n_halo_blocks = n // HALO
    cblk = NEW_CONV // CONV_W
    kern = functools.partial(_prep_kernel, tm=tm, seq_tiles=seq_tiles)
    out_bf = jax.ShapeDtypeStruct((n, WD), BF)
    ospec = pl.BlockSpec((tm, WD), lambda i: (i, 0))
    return pl.pallas_call(
        kern, grid=(n // tm,),
        in_specs=[pl.BlockSpec((tm, CONV_W), lambda i: (i, cblk)),
                  pl.BlockSpec((HALO, CONV_W),
                               lambda i: (jnp.maximum(i * hb_per_tile - 1, 0), cblk)),
                  pl.BlockSpec((HALO, CONV_W),
                               lambda i: (jnp.minimum((i + 1) * hb_per_tile, n_halo_blocks - 1), cblk)),
                  _resident(cw.shape), _resident(cb.shape), _resident(lg.shape), _resident(lb.shape),
                  _resident(sw.shape), _resident(hw.shape), _resident(hb.shape)],
        out_specs=[ospec] * 5, out_shape=[out_bf] * 5,
        scratch_shapes=[pltpu.VMEM((tm + 2 * HALO, WA), F32),
                        pltpu.VMEM((tm + 2 * HALO, WC), F32),
                        pltpu.VMEM((tm + 2 * HALO, 3 * WD), F32),
                        pltpu.VMEM((8, tm + SHIFT_SPAN, WA), F32)],
        compiler_params=_cparams(1), name="prep")(p, p, p, cw, cb, lg, lb, sw, hw, hb)


def _na_kernel(q_ref, k0_ref, k1_ref, k2_ref, v0_ref, v1_ref, v2_ref, kc_ref, vc_ref, bias_ref, o_ref):
    q = q_ref[...] * ATTN_SCALE
    k = jnp.concatenate([k0_ref[...], k1_ref[...], k2_ref[...]], axis=0)
    v = jnp.concatenate([v0_ref[...], v1_ref[...], v2_ref[...]], axis=0)
    kc = kc_ref[...]
    vc = vc_ref[...]
    ones_nb = jnp.ones((v.shape[0], NA_HEAD_DIM), BF)
    ones_cx = jnp.ones((vc.shape[0], NA_HEAD_DIM), BF)

    def scores(h):
        sl = slice(h * NA_HEAD_DIM, (h + 1) * NA_HEAD_DIM)
        qh = q[:, sl]
        return _dot_nt(qh, k[:, sl]) + bias_ref[0, h], _dot_nt(qh, kc[:, sl])

    def weighted(p_nb, p_cx, sl):
        o = (_dot(p_nb, jnp.concatenate([v[:, sl], ones_nb], axis=1))
             + _dot(p_cx, jnp.concatenate([vc[:, sl], ones_cx], axis=1)))
        return o[:, :NA_HEAD_DIM] / o[:, NA_HEAD_DIM:NA_HEAD_DIM + 1]

    outs = []
    sc = {0: scores(0), 1: scores(1)}
    pending = None
    for h in range(NA_HEADS):
        sl = slice(h * NA_HEAD_DIM, (h + 1) * NA_HEAD_DIM)
        s_nb, s_cx = sc.pop(h)
        if h + 2 < NA_HEADS:
            sc[h + 2] = scores(h + 2)
        m = jnp.maximum(jnp.max(s_nb, axis=-1, keepdims=True), jnp.max(s_cx, axis=-1, keepdims=True))
        p_nb = jnp.exp((s_nb - m).astype(BF))
        p_cx = jnp.exp((s_cx - m).astype(BF))
        if pending is not None:
            outs.append(pending())
        pending = functools.partial(weighted, p_nb, p_cx, sl)
    outs.append(pending())
    o_ref[...] = jnp.concatenate(outs, axis=-1).astype(BF)


def _rpb_expand_kernel(rpb_ref, onehot_ref, o_ref):
    o_ref[...] = jnp.dot(rpb_ref[...], onehot_ref[...], precision=lax.Precision.HIGHEST,
                         preferred_element_type=F32)


def _na_bias_table(rpb, rows):
    assert rows >= NA_WIN_ROWS and rows % NA_QROWS == 0 and rows // NA_QROWS >= 3
    n_dr, n_dc = 2 * NA_WIN_ROWS - 1, 2 * NA_WIN_COLS - 1
    cols = np.arange(GRID_W)
    c0 = np.clip(cols - NA_WIN_COLS // 2, 0, GRID_W - NA_WIN_COLS)
    dc = cols[None, :] - cols[:, None] + (NA_WIN_COLS - 1)
    ok_c = (cols[None, :] >= c0[:, None]) & (cols[None, :] < c0[:, None] + NA_WIN_COLS)
    onehot = np.zeros((LANES, GRID_W, GRID_W), np.float32)
    qq, kk = np.nonzero(ok_c)
    onehot[dc[qq, kk], qq, kk] = 1.0
    rpb2 = jnp.pad(rpb.reshape(NA_HEADS * n_dr, n_dc).astype(F32),
                   ((0, 64 - NA_HEADS * n_dr), (0, LANES - n_dc)))
    t = pl.pallas_call(
        _rpb_expand_kernel, grid=(1,),
        in_specs=[_resident((64, LANES)), _resident((LANES, GRID_W * GRID_W))],
        out_specs=pl.BlockSpec((64, GRID_W * GRID_W), lambda i: (0, 0)),
        out_shape=jax.ShapeDtypeStruct((64, GRID_W * GRID_W), F32),
        compiler_params=_cparams(1), name="rpb_expand")(rpb2, jnp.asarray(onehot.reshape(LANES, GRID_W * GRID_W)))
    t = t[:NA_HEADS * n_dr].reshape(NA_HEADS, n_dr, GRID_W, GRID_W)
    full = jnp.concatenate(
        [jnp.concatenate([t[:, s - a + NA_WIN_ROWS - 1 - NA_QROWS] for s in range(3 * NA_QROWS)], axis=-1)
         for a in range(NA_QROWS)], axis=1)
    wr = NA_WIN_ROWS
    n_blk = rows // NA_QROWS
    tabs = []
    for blk in (0, 1, n_blk - 1):
        qr = blk * NA_QROWS + np.arange(NA_QROWS)
        kr = (blk - 1) * NA_QROWS + np.arange(3 * NA_QROWS)
        r0 = np.clip(qr - wr // 2, 0, rows - wr)
        ok_r = ((kr[None, :] >= r0[:, None]) & (kr[None, :] < r0[:, None] + wr)
                & (kr[None, :] >= 0) & (kr[None, :] < rows))
        ok = (ok_r[:, None, :, None] & ok_c[None, :, None, :]).reshape(NA_TOK, 3 * NA_TOK)
        tabs.append(jnp.where(jnp.asarray(ok)[None], full, NEG_INF))
    return jnp.stack(tabs)


def _na(p, pc, bias_tab, *, bsz, seq_len):
    t = seq_len // NA_TOK
    qb, kb, vb = NEW_Q // WB, NEW_Q // WB + 1, NEW_Q // WB + 2
    blk = (NA_TOK, WB)
    cblk = (pc.shape[0] // bsz, WB)

    def nbr(j, col):
        return pl.BlockSpec(blk, lambda b, i: (b * t + jnp.clip(i - 1 + j, 0, t - 1), col))

    return pl.pallas_call(
        _na_kernel, grid=(bsz, t),
        in_specs=[pl.BlockSpec(blk, lambda b, i: (b * t + i, qb)),
                  nbr(0, kb), nbr(1, kb), nbr(2, kb), nbr(0, vb), nbr(1, vb), nbr(2, vb),
                  pl.BlockSpec(cblk, lambda b, i: (b, kb)),
                  pl.BlockSpec(cblk, lambda b, i: (b, vb)),
                  pl.BlockSpec((1, NA_HEADS, NA_TOK, 3 * NA_TOK),
                               lambda b, i: (jnp.where(i == 0, 0, jnp.where(i == t - 1, 2, 1)), 0, 0, 0))],
        out_specs=pl.BlockSpec(blk, lambda b, i: (b * t + i, 0)),
        out_shape=jax.ShapeDtypeStruct((bsz * seq_len, WB), BF),
        compiler_params=_cparams(2), name="na")(p, p, p, p, p, p, p, pc, pc, bias_tab)


def _ctx_attn_kernel(q_ref, k_ref, v_ref, o_ref):
    q = q_ref[...] * ATTN_SCALE
    k = k_ref[...]
    v = v_ref[...]
    outs = []
    for h in range(NA_HEADS):
        sl = slice(h * NA_HEAD_DIM, (h + 1) * NA_HEAD_DIM)
        s = _dot_nt(q[:, sl], k[:, sl])
        m = jnp.max(s, axis=-1, keepdims=True)
        p = jnp.exp(s - m)
        den = jnp.sum(p, axis=-1, keepdims=True)
        outs.append(_dot(p.astype(BF), v[:, sl]) / den)
    o_ref[...] = jnp.concatenate(outs, axis=-1).astype(BF)


def _ctx_attn(pc, *, bsz, ctx_len):
    qb, kb, vb = NEW_Q // WB, NEW_Q // WB + 1, NEW_Q // WB + 2
    blk = (ctx_len, WB)
    return pl.pallas_call(
        _ctx_attn_kernel, grid=(bsz,),
        in_specs=[pl.BlockSpec(blk, lambda b: (b, qb)), pl.BlockSpec(blk, lambda b: (b, kb)),
                  pl.BlockSpec(blk, lambda b: (b, vb))],
        out_specs=pl.BlockSpec(blk, lambda b: (b, 0)),
        out_shape=jax.ShapeDtypeStruct((bsz * ctx_len, WB), BF),
        compiler_params=_cparams(1), name="ctx_attn")(pc, pc, pc)


def _filt_kernel(zf_ref, zb_ref, w1_ref, b1_ref, w2_ref, b2_ref, fr_ref, w3_ref, dl_ref, k_ref, ss_ref,
                 *, tm):
    i = pl.program_id(0)
    hp = lax.Precision.HIGHEST
    zf = zf_ref[...]
    zb = zb_ref[...]
    w1 = w1_ref[...]
    pre = jnp.concatenate([jnp.dot(zf, w1, precision=hp, preferred_element_type=F32),
                           jnp.dot(zb, w1, precision=hp, preferred_element_type=F32)], axis=-1)
    h = jnp.sin(fr_ref[0:1, :] * (pre + b1_ref[...]))
    h = jnp.sin(fr_ref[1:2, :] * (jnp.dot(h, w2_ref[...], precision=hp, preferred_element_type=F32)
                                  + b2_ref[...]))
    k = _dot(h.astype(BF), w3_ref[...])
    dec_f = jnp.exp(-zf[:, 0:1] * dl_ref[...])
    dec_b = jnp.exp(-zb[:, 0:1] * dl_ref[...])
    k = k * jnp.concatenate([dec_f] * HY_ORDER + [dec_b] * HY_ORDER, axis=-1)
    is_row0 = (i * tm + lax.broadcasted_iota(jnp.int32, (tm, 1), 0)) == 0
    kf = k[:, :HY_ORDER * WD]
    kb = jnp.where(is_row0, 0.0, k[:, HY_ORDER * WD:])
    k_ref[0] = kf.astype(BF)
    k_ref[1] = kb.astype(BF)

    @pl.when(i == 0)
    def _():
        ss_ref[...] = jnp.zeros_like(ss_ref)

    ss_ref[...] += jnp.sum(kf * kf + kb * kb, axis=0, keepdims=True)


def _pos_features(t, length):
    t_norm = t / max(length - 1, 1)
    bands = jnp.linspace(1e-4, HY_PE_BANDS - 1, HY_PE_BANDS, dtype=F32)
    ang = (2.0 * math.pi / length) * t[:, None] * bands[None, :]
    z = jnp.concatenate([t_norm[:, None], jnp.cos(ang), -jnp.sin(ang)], axis=-1)
    return jnp.pad(z, ((0, 0), (0, LANES - HY_PE_DIM)))


def _block_diag(a, b):
    return jnp.concatenate([jnp.concatenate([a, jnp.zeros((a.shape[0], b.shape[1]), a.dtype)], axis=1),
                            jnp.concatenate([jnp.zeros((b.shape[0], a.shape[1]), a.dtype), b], axis=1)],
                           axis=0)


def _hyena_filters(length, w1, b1, w2, b2, freq, w3):
    t = jnp.arange(length, dtype=F32)
    zf = _pos_features(t, length)
    zb = _pos_features(length - t, length)
    w1p = jnp.pad(w1, ((0, LANES - HY_PE_DIM), (0, 0)))
    w3r = w3.reshape(HY_FILT_HID, HY_ORDER, 2, WD)
    w3bd = _block_diag(w3r[:, :, 0].reshape(HY_FILT_HID, HY_ORDER * WD),
                       w3r[:, :, 1].reshape(HY_FILT_HID, HY_ORDER * WD)).astype(BF)
    w2bd = _block_diag(w2, w2)
    b1d = jnp.concatenate([b1, b1])[None]
    b2d = jnp.concatenate([b2, b2])[None]
    frd = jnp.concatenate([freq, freq], axis=1)
    deltas = jnp.abs(jnp.linspace(math.log(HY_DECAY_TARGET) / HY_SLOW_DECAY,
                                  math.log(HY_DECAY_TARGET) / HY_FAST_DECAY, WD, dtype=F32))[None]
    tm = min(512, length)
    nf = HY_ORDER * WD
    zspec = pl.BlockSpec((tm, LANES), lambda i: (i, 0))
    k, ss = pl.pallas_call(
        functools.partial(_filt_kernel, tm=tm), grid=(length // tm,),
        in_specs=[zspec, zspec, _resident(w1p.shape), _resident(b1d.shape), _resident(w2bd.shape),
                  _resident(b2d.shape), _resident(frd.shape), _resident(w3bd.shape),
                  _resident(deltas.shape)],
        out_specs=[pl.BlockSpec((2, tm, nf), lambda i: (0, i, 0)), pl.BlockSpec((1, nf), lambda i: (0, 0))],
        out_shape=[jax.ShapeDtypeStruct((2, length, nf), BF), jax.ShapeDtypeStruct((1, nf), F32)],
        compiler_params=_cparams(1, "arbitrary"), name="hyena_filter")(zf, zb, w1p, b1d, w2bd, b2d, frd, w3bd, deltas)
    return k.reshape(2 * length, nf), ss


def _cis(num, den):
    ang = (-2.0 * math.pi / den) * (num % den).astype(F32)
    return jnp.cos(ang), jnp.sin(ang)


def _stack(re, im):
    return jnp.concatenate([jnp.concatenate([re, -im], axis=1), jnp.concatenate([im, re], axis=1)], axis=0)


def _dft_tables(n1):
    n = n1 * FFT_N2
    i1 = jnp.arange(n1)
    fr, fi = _cis(i1[:, None] * i1[None, :], n1)
    half = n1 // 2
    w_fwd = _stack(fr[:, :half], fi[:, :half]).astype(BF)
    kept = half + min(DFT_B_KB, n1)
    w_real = jnp.concatenate([fr[:kept], fi[:kept]], axis=0).astype(BF)
    w_inv = (_stack(fr[:half, :], -fi[:half, :]) / n).astype(BF)
    i2 = jnp.arange(FFT_N2)
    f2r, f2i = _cis(i2[:, None] * i2[None, :], FFT_N2)
    twr, twi = _cis(i1[:, None] * i2[None, :], n)
    gr = f2r[None] * twr[:, None, :] - f2i[None] * twi[:, None, :]
    gi = f2r[None] * twi[:, None, :] + f2i[None] * twr[:, None, :]
    gs = jnp.concatenate([jnp.concatenate([gr, -gi], axis=2), jnp.concatenate([gi, gr], axis=2)], axis=1)
    gs = gs.astype(BF)
    return w_fwd, w_real, w_inv, gs


def _dft_tables_direct(length):
    n = 2 * length
    i = jnp.arange(n)
    fr, fi = _cis(i[:, None] * i[None, :], n)
    w_fwd = _stack(fr[:, :length], fi[:, :length]).astype(BF)
    w_real = jnp.concatenate([fr, fi], axis=0).astype(BF)
    w_inv = (_stack(fr[:length, :], -fi[:length, :]) / n).astype(BF)
    return w_fwd, w_real, w_inv


def _stage_a_kernel(*refs, nb, n_half, has_epi):
    w_ref, x_ref = refs[0], refs[1]
    m, k = w_ref.shape
    if has_epi:
        g_ref, u_ref, row_ref, o_ref = refs[2:6]
        scratch = refs[6:]
    else:
        o_ref = refs[2]
        scratch = refs[3:]
    xs, os_ = scratch[:n_half], scratch[n_half:]
    w = w_ref[...]
    for l in range(n_half):
        lanes = slice(l * LANES, (l + 1) * LANES)
        xs[l][...] = x_ref[:, :, lanes].astype(F32).reshape(k * nb, LANES)
        for j in range(nb):
            os_[l][pl.ds(j, m, stride=nb), :] = _dot(w, xs[l][pl.ds(j, k, stride=nb), :].astype(BF))
        acc = os_[l][...].reshape(m, nb, LANES)
        if has_epi:
            acc = g_ref[:, :, lanes].astype(F32) * (
                acc + u_ref[:, :, lanes].astype(F32) * row_ref[:, lanes].reshape(1, 1, LANES))
        o_ref[:, :, lanes] = acc.astype(o_ref.dtype)


def _stage_a(w, x, *, name, epi=None, nb=16, lane_blk=256):
    m, k = w.shape
    c = x.shape[2]
    n_half = lane_blk // LANES
    xblk = pl.BlockSpec((k, nb, lane_blk), lambda j, l: (0, j, l))
    oblk = pl.BlockSpec((m, nb, lane_blk), lambda j, l: (0, j, l))
    ops, specs = [w, x], [_resident((m, k)), xblk]
    if epi is not None:
        ops += list(epi)
        specs += [oblk, oblk, pl.BlockSpec((1, lane_blk), lambda j, l: (0, l))]
    scratch = ([pltpu.VMEM((k * nb, LANES), F32)] * n_half + [pltpu.VMEM((m * nb, LANES), F32)] * n_half)
    return pl.pallas_call(
        functools.partial(_stage_a_kernel, nb=nb, n_half=n_half, has_epi=epi is not None),
        grid=(FFT_N2 // nb, c // lane_blk), in_specs=specs, out_specs=oblk, scratch_shapes=scratch,
        out_shape=jax.ShapeDtypeStruct((m, FFT_N2, c), BF), compiler_params=_cparams(2), name=name)(*ops)


def _stage_b_kernel(a_ref, fa_ref, fb_ref, gs_ref, ss_ref, o_ref, *, kb, n_blk):
    scale = lax.rsqrt(ss_ref[...] + 1e-6)

    def run(filt):
        yhs = []
        for j in range(kb):
            a = jnp.concatenate([a_ref[0, j], a_ref[1, j]], axis=0)
            yhs.append(_dot(gs_ref[j], jnp.concatenate([a, filt(j)], axis=1)))
        zs = []
        for yh in yhs:
            y, h = yh[:, :WD], yh[:, WD:] * scale
            yr, yi = y[:FFT_N2], y[FFT_N2:]
            hr, hi = h[:FFT_N2], h[FFT_N2:]
            zs.append(jnp.concatenate([yr * hr - yi * hi, yr * hi + yi * hr], axis=0).astype(BF))
        for j in range(kb):
            b = lax.dot_general(gs_ref[j], zs[j], (((0,), (0,)), ((), ())), preferred_element_type=F32)
            o_ref[0, j] = b[:FFT_N2].astype(BF)
            o_ref[1, j] = b[FFT_N2:].astype(BF)

    def direct(j):
        return jnp.concatenate([fa_ref[0, j], fa_ref[1, j]], axis=0)

    def mirrored(j):
        ref, r = (fb_ref, 0) if j == 0 else (fa_ref, kb - j)
        return jnp.concatenate([ref[0, r], -ref[1, r]], axis=0)

    i = pl.program_id(0)

    @pl.when(i < n_blk // 2)
    def _():
        run(direct)

    @pl.when(i >= n_blk // 2)
    def _():
        run(mirrored)


def _stage_b(a, filt_a, gs, ss, *, order):
    n1 = a.shape[1]
    kb = min(DFT_B_KB, n1)
    n_blk = n1 // kb
    assert n_blk % 2 == 0 and filt_a.shape[1] == n1 // 2 + kb
    dblk = pl.BlockSpec((2, kb, FFT_N2, WD), lambda i: (0, i, 0, 0))
    fa = pl.BlockSpec((2, kb, FFT_N2, WD), lambda i: (0, jnp.where(i < n_blk // 2, i, n_blk - 1 - i), 0, order))
    fb = pl.BlockSpec((2, 1, FFT_N2, WD), lambda i: (0, jnp.where(i < n_blk // 2, 0, n1 - kb * i), 0, order))
    gblk = pl.BlockSpec((kb, 2 * FFT_N2, 2 * FFT_N2), lambda i: (i, 0, 0))
    return pl.pallas_call(
        functools.partial(_stage_b_kernel, kb=kb, n_blk=n_blk), grid=(n_blk,),
        in_specs=[dblk, fa, fb, gblk, pl.BlockSpec((1, WD), lambda i: (0, order))], out_specs=dblk,
        out_shape=jax.ShapeDtypeStruct((2, n1, FFT_N2, WD), BF),
        compiler_params=_cparams(1), name="dft_b")(a, filt_a, filt_a, gs, ss)


def _cmul_kernel(x_ref, h_ref, ss_ref, o_ref, *, half):
    scale = lax.rsqrt(ss_ref[...] + 1e-6)
    xr, xi = x_ref[:half, :], x_ref[half:, :]
    hr, hi = h_ref[:half, :] * scale, h_ref[half:, :] * scale
    o_ref[:half, :] = (xr * hr - xi * hi).astype(BF)
    o_ref[half:, :] = (xr * hi + xi * hr).astype(BF)


def _cmul(x, h, ss, lane_blk):
    rows = x.shape[0]
    return pl.pallas_call(
        functools.partial(_cmul_kernel, half=rows // 2), grid=(1,),
        in_specs=[_resident(x.shape), pl.BlockSpec((rows, WD), lambda i: (0, lane_blk)),
                  pl.BlockSpec((1, WD), lambda i: (0, lane_blk))],
        out_specs=pl.BlockSpec(x.shape, lambda i: (0, 0)),
        out_shape=jax.ShapeDtypeStruct(x.shape, BF), compiler_params=_cparams(1), name="ctx_cmul")(x, h, ss)


def _hyena_long(hv, hx, kern, ss, bias, tables, *, bsz, seq_len):
    assert bsz == 2
    n1 = 2 * seq_len // FFT_N2
    w_fwd, w_real, w_inv, gs = tables
    view = (bsz * seq_len // FFT_N2, FFT_N2, WD)
    filt_a = _stage_a(w_real, kern.reshape(n1, FFT_N2, HY_ORDER * WD), name="dft_a_filter")
    filt_a = filt_a.reshape(2, filt_a.shape[0] // 2, FFT_N2, HY_ORDER * WD)
    u = hv.reshape(view)
    for o in range(HY_ORDER):
        a = _stage_a(w_fwd, u, name="dft_a_fwd").reshape(2, n1, FFT_N2, WD)
        b = _stage_b(a, filt_a, gs, ss, order=o).reshape(2 * n1, FFT_N2, WD)
        u = _stage_a(w_inv, b, name="dft_a_inv", epi=(hx[o].reshape(view), u, bias[o][None]))
    return u.reshape(bsz * seq_len, WD)


def _hyena_short_seq(hv, hx, kern, ss, bias, tables, *, bsz):
    assert bsz == 2
    w_fwd, w_real, w_inv = tables
    spec = _mm(w_real, kern, bn=HY_ORDER * WD, out_dtype=F32, name="ctx_dft_filter")
    u = hv
    for o in range(HY_ORDER):
        x = _mm(w_fwd, u, bn=WD, out_dtype=F32, name="ctx_dft_fwd")
        z = _cmul(x, spec, ss, o)
        u = _mm(w_inv, z, bn=WD, out_dtype=BF, name="ctx_dft_inv", epi=(hx[o], u, bias[o][None]))
    return u


def _merge_kernel(x_ref, mod_ref, ya_ref, yb_ref, yc_ref, yd_ref, gate_ref, wp_ref, wo_ref, bo_ref,
                  g_ref, b_ref, o_ref, *, alpha):
    x = x_ref[...]
    m = jnp.zeros(x.shape, F32)
    for br, y_ref in enumerate((ya_ref, yb_ref, yc_ref, yd_ref)):
        gate = gate_ref[:, br * D_MODEL:(br + 1) * D_MODEL].astype(F32)
        m = m + gate * _dot(y_ref[...], wp_ref[br])
    out = _dot(m.astype(BF), wo_ref[...]) + bo_ref[...]
    y = alpha * x + mod_ref[0][5:6] * out
    o_ref[...] = _ln(y) * g_ref[...] + b_ref[...]


def _merge(x, mod, ya, yb, yc, yd, p, wp, wo, bo, g, b, *, tm, group_tiles, alpha):
    n = x.shape[0]
    yspec = pl.BlockSpec((tm, WD), lambda i: (i, 0))
    return pl.pallas_call(
        functools.partial(_merge_kernel, alpha=alpha), grid=(n // tm,),
        in_specs=[pl.BlockSpec((tm, D_MODEL), lambda i: (i, 0)),
                  pl.BlockSpec((1, N_MOD, D_MODEL), lambda i: (i // group_tiles, 0, 0)),
                  yspec, yspec, yspec, yspec,
                  pl.BlockSpec((tm, N_BRANCH * D_MODEL), lambda i: (i, 0)),
                  _resident(wp.shape), _resident(wo.shape), _resident((1, D_MODEL)),
                  _resident((1, D_MODEL)), _resident((1, D_MODEL))],
        out_specs=pl.BlockSpec((tm, D_MODEL), lambda i: (i, 0)),
        out_shape=jax.ShapeDtypeStruct((n, D_MODEL), F32),
        compiler_params=_cparams(1), name="merge")(x, mod, ya, yb, yc, yd, p, wp, wo, bo, g, b)


def kernel(x, c, ctx, c_ctx, w_mod, b_mod, post_ln_g, post_ln_b, ffn_w_in, ffn_w_out, w_in, b_in,
           conf_dw_w, conf_dw_b, conf_ln_g, conf_ln_b, conf_w_proj, na_rpb, na_w_proj, sc_conv_w,
           sc_w_proj, hy_sconv_w, hy_sconv_b, hy_w1, hy_b1, hy_w2, hy_b2, hy_freq, hy_w3, hy_bias,
           hy_w_proj, w_out, b_out):
    bsz, n_lat, _ = x.shape
    ctx_len = ctx.shape[1]
    depth = w_mod.shape[0]
    alpha = (2 * depth) ** 0.25
    rows = n_lat // GRID_W

    tm = 512
    tmc = min(tm, ctx_len)
    tf = 1024 if n_lat % 1024 == 0 else tm
    tfc = min(tf, bsz * ctx_len)
    tp = 512
    tpc = min(tp, ctx_len)

    xl = x.reshape(bsz * n_lat, D_MODEL)
    xc = ctx.reshape(bsz * ctx_len, D_MODEL)
    lat_tiles = n_lat // tm
    ctx_tiles = bsz * ctx_len // tmc

    tables = _dft_tables(2 * n_lat // FFT_N2)
    tables_c = _dft_tables_direct(ctx_len)

    cond = jnp.concatenate([c, c_ctx[None], jnp.zeros((SUBLANES - bsz - 1, D_MODEL), F32)], axis=0)

    ffn_wi = ffn_w_in.astype(BF)
    ffn_wo = ffn_w_out.astype(BF)

    for l in range(depth):
        last = l == depth - 1
        mod_all = _mm(cond, w_mod, bn=1024, out_dtype=F32, name="adaln_mod", bias=b_mod[l][None], a_silu=True,
                      b_layer=l, precise=True)
        mod_all = mod_all.reshape(SUBLANES, N_MOD, D_MODEL)
        mod = mod_all[:bsz]
        mod_c = mod_all[bsz:bsz + 1]

        wl = w_in[l]
        w_perm = jnp.concatenate([wl[:, OFF_G:], wl[:, OFF_A:OFF_Q], wl[:, OFF_SB:OFF_G],
                                  wl[:, OFF_Q:OFF_SB]], axis=1).astype(BF)
        bl = b_in[l]
        b_perm = jnp.concatenate([bl[OFF_G:], bl[OFF_A:OFF_Q], bl[OFF_SB:OFF_G], bl[OFF_Q:OFF_SB]])[None]
        ln_g = post_ln_g[l][:, None, :]
        ln_b = post_ln_b[l][:, None, :]
        wp = jnp.stack([conf_w_proj[l], na_w_proj[l], sc_w_proj[l], hy_w_proj[l]]).astype(BF)
        wo = w_out[l].astype(BF)
        conv_args = (conf_dw_w[l], conf_dw_b[l][None], conf_ln_g[l][None], conf_ln_b[l][None],
                     sc_conv_w[l], hy_sconv_w[l], hy_sconv_b[l][None])
        filt_args = (hy_w1[l], hy_b1[l], hy_w2[l], hy_b2[l], hy_freq[l], hy_w3[l])

        xl = _ffn(xl, mod, ffn_wi, ffn_wo, ln_g[0], ln_b[0], layer=l, half=0, m0=0, tm=tf,
                  group_tiles=n_lat // tf, alpha=alpha)
        xc = _ffn(xc, mod_c, ffn_wi, ffn_wo, ln_g[0], ln_b[0], layer=l, half=0, m0=0, tm=tfc,
                  group_tiles=bsz * ctx_len // tfc, alpha=alpha)

        p = _inproj(xl, mod, w_perm, b_perm, tm=tm, group_tiles=lat_tiles)
        pc = _inproj(xc, mod_c, w_perm, b_perm, tm=tmc, group_tiles=ctx_tiles)

        ya, yc, hv, h1, h2 = _prep(p, *conv_args, tm=tp, seq_len=n_lat)
        attn = _na(p, pc, _na_bias_table(na_rpb[l], rows), bsz=bsz, seq_len=n_lat)
        kern, ss = _hyena_filters(n_lat, *filt_args)
        yd = _hyena_long(hv, (h1, h2), kern, ss, hy_bias[l], tables, bsz=bsz, seq_len=n_lat)
        xl = _merge(xl, mod, ya, attn, yc, yd, p, wp, wo, b_out[l][None], ln_g[1], ln_b[1],
                    tm=tm, group_tiles=lat_tiles, alpha=alpha)

        if not last:
            ya, yc, hv, h1, h2 = _prep(pc, *conv_args, tm=tpc, seq_len=ctx_len)
            attn_c = _ctx_attn(pc, bsz=bsz, ctx_len=ctx_len)
            kern_c, ss_c = _hyena_filters(ctx_len, *filt_args)
            yd = _hyena_short_seq(hv, (h1, h2), kern_c, ss_c, hy_bias[l], tables_c, bsz=bsz)
            xc = _merge(xc, mod_c, ya, attn_c, yc, yd, pc, wp, wo, b_out[l][None], ln_g[1], ln_b[1],
                        tm=tmc, group_tiles=ctx_tiles, alpha=alpha)

        xl = _ffn(xl, mod, ffn_wi, ffn_wo, ln_g[2], ln_b[2], layer=l, half=1, m0=6, tm=tf,
                  group_tiles=n_lat // tf, alpha=alpha)
        if not last:
            xc = _ffn(xc, mod_c, ffn_wi, ffn_wo, ln_g[2], ln_b[2], layer=l, half=1, m0=6, tm=tfc,
                      group_tiles=bsz * ctx_len // tfc, alpha=alpha)

    return xl.reshape(bsz, n_lat, D_MODEL)
```

```python
import functools
import math

import jax
import jax.numpy as jnp
import numpy as np
from jax import lax
from jax.experimental import pallas as pl
from jax.experimental.pallas import tpu as pltpu

D_MODEL = 1024
GRID_W = 64
N_BRANCH = 4
WA = 256
CONF_K = 31
NA_HEADS = 4
NA_HEAD_DIM = 64
WB = NA_HEADS * NA_HEAD_DIM
NA_WIN_ROWS = 8
NA_WIN_COLS = 16
ATTN_SCALE = NA_HEAD_DIM ** -0.5
WC = 256
SC_K = 3
WD = 256
HY_ORDER = 2
HY_SHORT_K = 3
HY_PE_BANDS = 16
HY_PE_DIM = 1 + 2 * HY_PE_BANDS
HY_FILT_HID = 64
HY_FAST_DECAY = 0.3
HY_SLOW_DECAY = 1.5
HY_DECAY_TARGET = 1e-2
D_FF = 2816
N_MOD = 9
LN_EPS = 1e-5

OFF_A = 0
OFF_Q = OFF_A + 2 * WA
OFF_K = OFF_Q + WB
OFF_V = OFF_K + WB
OFF_SB = OFF_V + WB
OFF_SC = OFF_SB + WC
OFF_SX = OFF_SC + WC
OFF_HV = OFF_SX + WC
OFF_G = OFF_HV + (1 + HY_ORDER) * WD
P_IN = OFF_G + N_BRANCH * D_MODEL

NEW_G = 0
NEW_CONV = N_BRANCH * D_MODEL
CONV_W = 2 * WA + 3 * WC + 3 * WD
NEW_Q = NEW_CONV + CONV_W

BF = jnp.bfloat16
F32 = jnp.float32

LANES = 128
SUBLANES = 8
VMEM_BYTES = 64 * 1024 * 1024
VMEM_LIMIT_BYTES = VMEM_BYTES - 6 * 1024 * 1024
FFN_CHUNK = 256
INPROJ_CHUNKS = (1024,) * 4 + (768, 768, 768, 512)
HALO = 16
SHIFT_SPAN = 24
NA_QROWS = 4
NA_TOK = NA_QROWS * GRID_W
FFT_N2 = 128
DFT_B_KB = 16
NEG_INF = -1e30


def _cparams(n_axes, semantics="parallel"):
    return pltpu.CompilerParams(dimension_semantics=(semantics,) * n_axes,
                                vmem_limit_bytes=VMEM_LIMIT_BYTES)


def _resident(shape):
    nd = len(shape)
    return pl.BlockSpec(shape, lambda *_: (0,) * nd, pipeline_mode=pl.Buffered(1))


def _ln(x):
    mu = jnp.mean(x, axis=-1, keepdims=True)
    xc = x - mu
    var = jnp.mean(xc * xc, axis=-1, keepdims=True)
    return xc * lax.rsqrt(var + LN_EPS)


def _sigmoid(x):
    return 0.5 * jnp.tanh(0.5 * x) + 0.5


def _dot(a, b):
    return jnp.dot(a, b, preferred_element_type=F32)


def _dot_nt(a, b):
    return lax.dot_general(a, b, (((1,), (1,)), ((), ())), preferred_element_type=F32)


def _mm_kernel(*refs, a_silu, has_bias, has_epi, precise):
    a_ref, b_ref = refs[0], refs[1]
    pos = 2
    a = a_ref[...]
    if a_silu:
        a = a.astype(F32)
        a = a * _sigmoid(a)
    if precise:
        acc = jnp.dot(a.astype(F32), b_ref[...].astype(F32), precision=lax.Precision.HIGHEST,
                      preferred_element_type=F32)
    else:
        acc = _dot(a.astype(BF), b_ref[...].astype(BF))
    if has_bias:
        acc = acc + refs[pos][...]
        pos += 1
    if has_epi:
        gate_ref, u_ref, row_ref = refs[pos], refs[pos + 1], refs[pos + 2]
        pos += 3
        acc = gate_ref[...].astype(F32) * (acc + u_ref[...].astype(F32) * row_ref[...])
    o_ref = refs[pos]
    o_ref[...] = acc.astype(o_ref.dtype)


def _mm(a, b, *, bn, out_dtype, name, bias=None, epi=None, a_silu=False, b_layer=None, precise=False):
    m, k = a.shape
    n = b.shape[-1]
    bn = min(bn, n)
    ops = [a, b]
    if b_layer is None:
        bspec = pl.BlockSpec((k, bn), lambda j: (0, j))
    else:
        bspec = pl.BlockSpec((None, k, bn), lambda j: (b_layer, 0, j))
    specs = [_resident((m, k)), bspec]
    if bias is not None:
        ops.append(bias)
        specs.append(pl.BlockSpec((1, bn), lambda j: (0, j)))
    if epi is not None:
        gate, u, row = epi
        ops += [gate, u, row]
        specs += [pl.BlockSpec((m, bn), lambda j: (0, j)),
                  pl.BlockSpec((m, bn), lambda j: (0, j)),
                  pl.BlockSpec((1, bn), lambda j: (0, j))]
    kern = functools.partial(_mm_kernel, a_silu=a_silu, has_bias=bias is not None,
                             has_epi=epi is not None, precise=precise)
    return pl.pallas_call(
        kern, grid=(n // bn,), in_specs=specs,
        out_specs=pl.BlockSpec((m, bn), lambda j: (0, j)),
        out_shape=jax.ShapeDtypeStruct((m, n), out_dtype),
        compiler_params=_cparams(1), name=name)(*ops)


def _modulated(x, mod, m0):
    shift = mod[m0:m0 + 1]
    scale = mod[m0 + 1:m0 + 2]
    return _ln(x) * (1.0 + scale) + shift


def _ffn_kernel(x_ref, mod_ref, wi_ref, wo_ref, g_ref, b_ref, o_ref, u_ref, *, m0, alpha):
    x = x_ref[...]
    mod = mod_ref[0]
    h = _modulated(x, mod, m0).astype(BF)
    for c in range(D_FF // FFN_CHUNK):
        lo = c * FFN_CHUNK
        a = _dot(h, wi_ref[:, lo:lo + FFN_CHUNK])
        g = _dot(h, wi_ref[:, D_FF + lo:D_FF + lo + FFN_CHUNK])
        u_ref[:, lo:lo + FFN_CHUNK] = (g * _sigmoid(g) * a).astype(BF)
    acc = _dot(u_ref[...], wo_ref[...])
    y = alpha * x + (0.5 * mod[m0 + 2:m0 + 3]) * acc
    o_ref[...] = _ln(y) * g_ref[...] + b_ref[...]


def _ffn(x, mod, w_in, w_out, g, b, *, layer, half, m0, tm, group_tiles, alpha):
    n = x.shape[0]

    def stacked(w):
        return pl.BlockSpec((None, None) + w.shape[2:], lambda i: (layer, half, 0, 0),
                            pipeline_mode=pl.Buffered(1))

    kern = functools.partial(_ffn_kernel, m0=m0, alpha=alpha)
    return pl.pallas_call(
        kern, grid=(n // tm,),
        in_specs=[pl.BlockSpec((tm, D_MODEL), lambda i: (i, 0)),
                  pl.BlockSpec((1, N_MOD, D_MODEL), lambda i: (i // group_tiles, 0, 0)),
                  stacked(w_in), stacked(w_out),
                  _resident((1, D_MODEL)), _resident((1, D_MODEL))],
        out_specs=pl.BlockSpec((tm, D_MODEL), lambda i: (i, 0)),
        out_shape=jax.ShapeDtypeStruct((n, D_MODEL), F32),
        scratch_shapes=[pltpu.VMEM((tm, D_FF), BF)],
        compiler_params=_cparams(1), name="ffn")(x, mod, w_in, w_out, g, b)


def _inproj_kernel(x_ref, mod_ref, w_ref, b_ref, o_ref, *, m0):
    h = _modulated(x_ref[...], mod_ref[0], m0).astype(BF)
    lo = 0
    for width in INPROJ_CHUNKS:
        y = _dot(h, w_ref[:, lo:lo + width]) + b_ref[:, lo:lo + width]
        if lo < NEW_CONV:
            y = _sigmoid(y)
        o_ref[:, lo:lo + width] = y.astype(BF)
        lo += width


def _inproj(x, mod, w, b, *, tm, group_tiles):
    n = x.shape[0]
    return pl.pallas_call(
        functools.partial(_inproj_kernel, m0=3), grid=(n // tm,),
        in_specs=[pl.BlockSpec((tm, D_MODEL), lambda i: (i, 0)),
                  pl.BlockSpec((1, N_MOD, D_MODEL), lambda i: (i // group_tiles, 0, 0)),
                  _resident(w.shape), _resident((1, P_IN))],
        out_specs=pl.BlockSpec((tm, P_IN), lambda i: (i, 0)),
        out_shape=jax.ShapeDtypeStruct((n, P_IN), BF),
        compiler_params=_cparams(1), name="inproj")(x, mod, w, b)


def _prep_kernel(x_ref, prev_ref, next_ref, cw_ref, cb_ref, lg_ref, lb_ref, sw_ref, hw_ref, hb_ref,
                 ya_ref, yc_ref, hv_ref, h1_ref, h2_ref, buf_a, buf_c, buf_h, shifted, *, tm, seq_tiles):
    i = pl.program_id(0)
    pos = i % seq_tiles
    keep_prev = jnp.where(pos == 0, 0.0, 1.0)
    keep_next = jnp.where(pos == seq_tiles - 1, 0.0, 1.0)

    def fill(t, lo, hi):
        glu = t[:, 0:WA] * _sigmoid(t[:, WA:2 * WA])
        buf_a[lo:hi, :] = glu
        buf_c[lo:hi, :] = t[:, 3 * WA:4 * WA] * t[:, 4 * WA:5 * WA]
        buf_h[lo:hi, :] = t[:, 5 * WA:8 * WA]

    main = x_ref[...].astype(F32)
    fill(prev_ref[...].astype(F32) * keep_prev, 0, HALO)
    fill(main, HALO, HALO + tm)
    fill(next_ref[...].astype(F32) * keep_next, HALO + tm, 2 * HALO + tm)

    span = tm + SHIFT_SPAN
    for r in range(8):
        shifted[r] = buf_a[r:r + span, :]
    acc = jnp.zeros((tm, WA), F32) + cb_ref[...]
    for j in range(CONF_K):
        off = HALO - CONF_K // 2 + j
        base = off - off % 8
        acc = acc + cw_ref[j:j + 1, :] * shifted[off % 8, base:base + tm, :]
    u = _ln(acc) * lg_ref[...] + lb_ref[...]
    ya_ref[...] = (u * _sigmoid(u)).astype(BF)

    acc = jnp.zeros((tm, WC), F32)
    for j in range(SC_K):
        off = HALO - SC_K // 2 + j
        acc = acc + sw_ref[j:j + 1, :] * buf_c[off:off + tm, :]
    yc_ref[...] = (main[:, 2 * WA:3 * WA] * acc).astype(BF)

    acc = jnp.zeros((tm, 3 * WD), F32) + hb_ref[...]
    for j in range(HY_SHORT_K):
        off = HALO - HY_SHORT_K // 2 + j
        acc = acc + hw_ref[j:j + 1, :] * buf_h[off:off + tm, :]
    hv_ref[...] = acc[:, 0:WD].astype(BF)
    h1_ref[...] = acc[:, WD:2 * WD].astype(BF)
    h2_ref[...] = acc[:, 2 * WD:3 * WD].astype(BF)


def _prep(p, cw, cb, lg, lb, sw, hw, hb, *, tm, seq_len):
    n = p.shape[0]
    seq_tiles = seq_len // tm
    hb_per_tile = tm // HALO
    n_halo_blocks = n // HALO
    cblk = NEW_CONV // CONV_W
    kern = functools.partial(_prep_kernel, tm=tm, seq_tiles=seq_tiles)
    out_bf = jax.ShapeDtypeStruct((n, WD), BF)
    ospec = pl.BlockSpec((tm, WD), lambda i: (i, 0))
    return pl.pallas_call(
        kern, grid=(n // tm,),
        in_specs=[pl.BlockSpec((tm, CONV_W), lambda i: (i, cblk)),
                  pl.BlockSpec((HALO, CONV_W),
                               lambda i: (jnp.maximum(i * hb_per_tile - 1, 0), cblk)),
                  pl.BlockSpec((HALO, CONV_W),
                               lambda i: (jnp.minimum((i + 1) * hb_per_tile, n_halo_blocks - 1), cblk)),
                  _resident(cw.shape), _resident(cb.shape), _resident(lg.shape), _resident(lb.shape),
                  _resident(sw.shape), _resident(hw.shape), _resident(hb.shape)],
        out_specs=[ospec] * 5, out_shape=[out_bf] * 5,
        scratch_shapes=[pltpu.VMEM((tm + 2 * HALO, WA), F32),
                        pltpu.VMEM((tm + 2 * HALO, WC), F32),
                        pltpu.VMEM((tm + 2 * HALO, 3 * WD), F32),
                        pltpu.VMEM((8, tm + SHIFT_SPAN, WA), F32)],
        compiler_params=_cparams(1), name="prep")(p, p, p, cw, cb, lg, lb, sw, hw, hb)


def _na_kernel(q_ref, k0_ref, k1_ref, k2_ref, v0_ref, v1_ref, v2_ref, kc_ref, vc_ref, bias_ref, o_ref):
    q = q_ref[...] * ATTN_SCALE
    k = jnp.concatenate([k0_ref[...], k1_ref[...], k2_ref[...]], axis=0)
    v = jnp.concatenate([v0_ref[...], v1_ref[...], v2_ref[...]], axis=0)
    kc = kc_ref[...]
    vc = vc_ref[...]
    ones_nb = jnp.ones((v.shape[0], NA_HEAD_DIM), BF)
    ones_cx = jnp.ones((vc.shape[0], NA_HEAD_DIM), BF)

    def scores(h):
        sl = slice(h * NA_HEAD_DIM, (h + 1) * NA_HEAD_DIM)
        qh = q[:, sl]
        return _dot_nt(qh, k[:, sl]) + bias_ref[0, h], _dot_nt(qh, kc[:, sl])

    def weighted(p_nb, p_cx, sl):
        o = (_dot(p_nb, jnp.concatenate([v[:, sl], ones_nb], axis=1))
             + _dot(p_cx, jnp.concatenate([vc[:, sl], ones_cx], axis=1)))
        return o[:, :NA_HEAD_DIM] / o[:, NA_HEAD_DIM:NA_HEAD_DIM + 1]

    outs = []
    sc = {0: scores(0), 1: scores(1)}
    pending = None
    for h in range(NA_HEADS):
        sl = slice(h * NA_HEAD_DIM, (h + 1) * NA_HEAD_DIM)
        s_nb, s_cx = sc.pop(h)
        if h + 2 < NA_HEADS:
            sc[h + 2] = scores(h + 2)
        m = jnp.maximum(jnp.max(s_nb, axis=-1, keepdims=True), jnp.max(s_cx, axis=-1, keepdims=True))
        p_nb = jnp.exp((s_nb - m).astype(BF))
        p_cx = jnp.exp((s_cx - m).astype(BF))
        if pending is not None:
            outs.append(pending())
        pending = functools.partial(weighted, p_nb, p_cx, sl)
    outs.append(pending())
    o_ref[...] = jnp.concatenate(outs, axis=-1).astype(BF)


def _rpb_expand_kernel(rpb_ref, onehot_ref, o_ref):
    o_ref[...] = jnp.dot(rpb_ref[...], onehot_ref[...], precision=lax.Precision.HIGHEST,
                         preferred_element_type=F32)


def _na_bias_table(rpb, rows):
    assert rows >= NA_WIN_ROWS and rows % NA_QROWS == 0 and rows // NA_QROWS >= 3
    n_dr, n_dc = 2 * NA_WIN_ROWS - 1, 2 * NA_WIN_COLS - 1
    cols = np.arange(GRID_W)
    c0 = np.clip(cols - NA_WIN_COLS // 2, 0, GRID_W - NA_WIN_COLS)
    dc = cols[None, :] - cols[:, None] + (NA_WIN_COLS - 1)
    ok_c = (cols[None, :] >= c0[:, None]) & (cols[None, :] < c0[:, None] + NA_WIN_COLS)
    onehot = np.zeros((LANES, GRID_W, GRID_W), np.float32)
    qq, kk = np.nonzero(ok_c)
    onehot[dc[qq, kk], qq, kk] = 1.0
    rpb2 = jnp.pad(rpb.reshape(NA_HEADS * n_dr, n_dc).astype(F32),
                   ((0, 64 - NA_HEADS * n_dr), (0, LANES - n_dc)))
    t = pl.pallas_call(
        _rpb_expand_kernel, grid=(1,),
        in_specs=[_resident((64, LANES)), _resident((LANES, GRID_W * GRID_W))],
        out_specs=pl.BlockSpec((64, GRID_W * GRID_W), lambda i: (0, 0)),
        out_shape=jax.ShapeDtypeStruct((64, GRID_W * GRID_W), F32),
        compiler_params=_cparams(1), name="rpb_expand")(rpb2, jnp.asarray(onehot.reshape(LANES, GRID_W * GRID_W)))
    t = t[:NA_HEADS * n_dr].reshape(NA_HEADS, n_dr, GRID_W, GRID_W)
    full = jnp.concatenate(
        [jnp.concatenate([t[:, s - a + NA_WIN_ROWS - 1 - NA_QROWS] for s in range(3 * NA_QROWS)], axis=-1)
         for a in range(NA_QROWS)], axis=1)
    wr = NA_WIN_ROWS
    n_blk = rows // NA_QROWS
    tabs = []
    for blk in (0, 1, n_blk - 1):
        qr = blk * NA_QROWS + np.arange(NA_QROWS)
        kr = (blk - 1) * NA_QROWS + np.arange(3 * NA_QROWS)
        r0 = np.clip(qr - wr // 2, 0, rows - wr)
        ok_r = ((kr[None, :] >= r0[:, None]) & (kr[None, :] < r0[:, None] + wr)
                & (kr[None, :] >= 0) & (kr[None, :] < rows))
        ok = (ok_r[:, None, :, None] & ok_c[None, :, None, :]).reshape(NA_TOK, 3 * NA_TOK)
        tabs.append(jnp.where(jnp.asarray(ok)[None], full, NEG_INF))
    return jnp.stack(tabs)


def _na(p, pc, bias_tab, *, bsz, seq_len):
    t = seq_len // NA_TOK
    qb, kb, vb = NEW_Q // WB, NEW_Q // WB + 1, NEW_Q // WB + 2
    blk = (NA_TOK, WB)
    cblk = (pc.shape[0] // bsz, WB)

    def nbr(j, col):
        return pl.BlockSpec(blk, lambda b, i: (b * t + jnp.clip(i - 1 + j, 0, t - 1), col))

    return pl.pallas_call(
        _na_kernel, grid=(bsz, t),
        in_specs=[pl.BlockSpec(blk, lambda b, i: (b * t + i, qb)),
                  nbr(0, kb), nbr(1, kb), nbr(2, kb), nbr(0, vb), nbr(1, vb), nbr(2, vb),
                  pl.BlockSpec(cblk, lambda b, i: (b, kb)),
                  pl.BlockSpec(cblk, lambda b, i: (b, vb)),
                  pl.BlockSpec((1, NA_HEADS, NA_TOK, 3 * NA_TOK),
                               lambda b, i: (jnp.where(i == 0, 0, jnp.where(i == t - 1, 2, 1)), 0, 0, 0))],
        out_specs=pl.BlockSpec(blk, lambda b, i: (b * t + i, 0)),
        out_shape=jax.ShapeDtypeStruct((bsz * seq_len, WB), BF),
        compiler_params=_cparams(2), name="na")(p, p, p, p, p, p, p, pc, pc, bias_tab)


def _ctx_attn_kernel(q_ref, k_ref, v_ref, o_ref):
    q = q_ref[...] * ATTN_SCALE
    k = k_ref[...]
    v = v_ref[...]
    outs = []
    for h in range(NA_HEADS):
        sl = slice(h * NA_HEAD_DIM, (h + 1) * NA_HEAD_DIM)
        s = _dot_nt(q[:, sl], k[:, sl])
        m = jnp.max(s, axis=-1, keepdims=True)
        p = jnp.exp(s - m)
        den = jnp.sum(p, axis=-1, keepdims=True)
        outs.append(_dot(p.astype(BF), v[:, sl]) / den)
    o_ref[...] = jnp.concatenate(outs, axis=-1).astype(BF)


def _ctx_attn(pc, *, bsz, ctx_len):
    qb, kb, vb = NEW_Q // WB, NEW_Q // WB + 1, NEW_Q // WB + 2
    blk = (ctx_len, WB)
    return pl.pallas_call(
        _ctx_attn_kernel, grid=(bsz,),
        in_specs=[pl.BlockSpec(blk, lambda b: (b, qb)), pl.BlockSpec(blk, lambda b: (b, kb)),
                  pl.BlockSpec(blk, lambda b: (b, vb))],
        out_specs=pl.BlockSpec(blk, lambda b: (b, 0)),
        out_shape=jax.ShapeDtypeStruct((bsz * ctx_len, WB), BF),
        compiler_params=_cparams(1), name="ctx_attn")(pc, pc, pc)


def _filt_kernel(zf_ref, zb_ref, w1_ref, b1_ref, w2_ref, b2_ref, fr_ref, w3_ref, dl_ref, k_ref, ss_ref,
                 *, tm):
    i = pl.program_id(0)
    hp = lax.Precision.HIGHEST
    zf = zf_ref[...]
    zb = zb_ref[...]
    w1 = w1_ref[...]
    pre = jnp.concatenate([jnp.dot(zf, w1, precision=hp, preferred_element_type=F32),
                           jnp.dot(zb, w1, precision=hp, preferred_element_type=F32)], axis=-1)
    h = jnp.sin(fr_ref[0:1, :] * (pre + b1_ref[...]))
    h = jnp.sin(fr_ref[1:2, :] * (jnp.dot(h, w2_ref[...], precision=hp, preferred_element_type=F32)
                                  + b2_ref[...]))
    k = _dot(h.astype(BF), w3_ref[...])
    dec_f = jnp.exp(-zf[:, 0:1] * dl_ref[...])
    dec_b = jnp.exp(-zb[:, 0:1] * dl_ref[...])
    k = k * jnp.concatenate([dec_f] * HY_ORDER + [dec_b] * HY_ORDER, axis=-1)
    is_row0 = (i * tm + lax.broadcasted_iota(jnp.int32, (tm, 1), 0)) == 0
    kf = k[:, :HY_ORDER * WD]
    kb = jnp.where(is_row0, 0.0, k[:, HY_ORDER * WD:])
    k_ref[0] = kf.astype(BF)
    k_ref[1] = kb.astype(BF)

    @pl.when(i == 0)
    def _():
        ss_ref[...] = jnp.zeros_like(ss_ref)

    ss_ref[...] += jnp.sum(kf * kf + kb * kb, axis=0, keepdims=True)


def _pos_features(t, length):
    t_norm = t / max(length - 1, 1)
    bands = jnp.linspace(1e-4, HY_PE_BANDS - 1, HY_PE_BANDS, dtype=F32)
    ang = (2.0 * math.pi / length) * t[:, None] * bands[None, :]
    z = jnp.concatenate([t_norm[:, None], jnp.cos(ang), -jnp.sin(ang)], axis=-1)
    return jnp.pad(z, ((0, 0), (0, LANES - HY_PE_DIM)))


def _block_diag(a, b):
    return jnp.concatenate([jnp.concatenate([a, jnp.zeros((a.shape[0], b.shape[1]), a.dtype)], axis=1),
                            jnp.concatenate([jnp.zeros((b.shape[0], a.shape[1]), a.dtype), b], axis=1)],
                           axis=0)


def _hyena_filters(length, w1, b1, w2, b2, freq, w3):
    t = jnp.arange(length, dtype=F32)
    zf = _pos_features(t, length)
    zb = _pos_features(length - t, length)
    w1p = jnp.pad(w1, ((0, LANES - HY_PE_DIM), (0, 0)))
    w3r = w3.reshape(HY_FILT_HID, HY_ORDER, 2, WD)
    w3bd = _block_diag(w3r[:, :, 0].reshape(HY_FILT_HID, HY_ORDER * WD),
                       w3r[:, :, 1].reshape(HY_FILT_HID, HY_ORDER * WD)).astype(BF)
    w2bd = _block_diag(w2, w2)
    b1d = jnp.concatenate([b1, b1])[None]
    b2d = jnp.concatenate([b2, b2])[None]
    frd = jnp.concatenate([freq, freq], axis=1)
    deltas = jnp.abs(jnp.linspace(math.log(HY_DECAY_TARGET) / HY_SLOW_DECAY,
                                  math.log(HY_DECAY_TARGET) / HY_FAST_DECAY, WD, dtype=F32))[None]
    tm = min(1024, length)
    nf = HY_ORDER * WD
    zspec = pl.BlockSpec((tm, LANES), lambda i: (i, 0))
    k, ss = pl.pallas_call(
        functools.partial(_filt_kernel, tm=tm), grid=(length // tm,),
        in_specs=[zspec, zspec, _resident(w1p.shape), _resident(b1d.shape), _resident(w2bd.shape),
                  _resident(b2d.shape), _resident(frd.shape), _resident(w3bd.shape),
                  _resident(deltas.shape)],
        out_specs=[pl.BlockSpec((2, tm, nf), lambda i: (0, i, 0)), pl.BlockSpec((1, nf), lambda i: (0, 0))],
        out_shape=[jax.ShapeDtypeStruct((2, length, nf), BF), jax.ShapeDtypeStruct((1, nf), F32)],
        compiler_params=_cparams(1, "arbitrary"), name="hyena_filter")(zf, zb, w1p, b1d, w2bd, b2d, frd, w3bd, deltas)
    return k.reshape(2 * length, nf), ss


def _cis(num, den):
    ang = (-2.0 * math.pi / den) * (num % den).astype(F32)
    return jnp.cos(ang), jnp.sin(ang)


def _stack(re, im):
    return jnp.concatenate([jnp.concatenate([re, -im], axis=1), jnp.concatenate([im, re], axis=1)], axis=0)


def _dft_tables(n1):
    n = n1 * FFT_N2
    i1 = jnp.arange(n1)
    fr, fi = _cis(i1[:, None] * i1[None, :], n1)
    half = n1 // 2
    w_fwd = _stack(fr[:, :half], fi[:, :half]).astype(BF)
    kept = half + min(DFT_B_KB, n1)
    w_real = jnp.concatenate([fr[:kept], fi[:kept]], axis=0).astype(BF)
    w_inv = (_stack(fr[:half, :], -fi[:half, :]) / n).astype(BF)
    i2 = jnp.arange(FFT_N2)
    f2r, f2i = _cis(i2[:, None] * i2[None, :], FFT_N2)
    twr, twi = _cis(i1[:, None] * i2[None, :], n)
    gr = f2r[None] * twr[:, None, :] - f2i[None] * twi[:, None, :]
    gi = f2r[None] * twi[:, None, :] + f2i[None] * twr[:, None, :]
    gs = jnp.concatenate([jnp.concatenate([gr, -gi], axis=2), jnp.concatenate([gi, gr], axis=2)], axis=1)
    gs = gs.astype(BF)
    return w_fwd, w_real, w_inv, gs


def _dft_tables_direct(length):
    n = 2 * length
    i = jnp.arange(n)
    fr, fi = _cis(i[:, None] * i[None, :], n)
    w_fwd = _stack(fr[:, :length], fi[:, :length]).astype(BF)
    w_real = jnp.concatenate([fr, fi], axis=0).astype(BF)
    w_inv = (_stack(fr[:length, :], -fi[:length, :]) / n).astype(BF)
    return w_fwd, w_real, w_inv


def _stage_a_kernel(*refs, nb, n_half, has_epi):
    w_ref, x_ref = refs[0], refs[1]
    m, k = w_ref.shape
    if has_epi:
        g_ref, u_ref, row_ref, o_ref = refs[2:6]
        scratch = refs[6:]
    else:
        o_ref = refs[2]
        scratch = refs[3:]
    xs, os_ = scratch[:n_half], scratch[n_half:]
    w = w_ref[...]
    for l in range(n_half):
        lanes = slice(l * LANES, (l + 1) * LANES)
        xs[l][...] = x_ref[:, :, lanes].astype(F32).reshape(k * nb, LANES)
        for j in range(nb):
            os_[l][pl.ds(j, m, stride=nb), :] = _dot(w, xs[l][pl.ds(j, k, stride=nb), :].astype(BF))
        acc = os_[l][...].reshape(m, nb, LANES)
        if has_epi:
            acc = g_ref[:, :, lanes].astype(F32) * (
                acc + u_ref[:, :, lanes].astype(F32) * row_ref[:, lanes].reshape(1, 1, LANES))
        o_ref[:, :, lanes] = acc.astype(o_ref.dtype)


def _stage_a(w, x, *, name, epi=None, nb=16, lane_blk=256):
    m, k = w.shape
    c = x.shape[2]
    n_half = lane_blk // LANES
    xblk = pl.BlockSpec((k, nb, lane_blk), lambda j, l: (0, j, l))
    oblk = pl.BlockSpec((m, nb, lane_blk), lambda j, l: (0, j, l))
    ops, specs = [w, x], [_resident((m, k)), xblk]
    if epi is not None:
        ops += list(epi)
        specs += [oblk, oblk, pl.BlockSpec((1, lane_blk), lambda j, l: (0, l))]
    scratch = ([pltpu.VMEM((k * nb, LANES), F32)] * n_half + [pltpu.VMEM((m * nb, LANES), F32)] * n_half)
    return pl.pallas_call(
        functools.partial(_stage_a_kernel, nb=nb, n_half=n_half, has_epi=epi is not None),
        grid=(FFT_N2 // nb, c // lane_blk), in_specs=specs, out_specs=oblk, scratch_shapes=scratch,
        out_shape=jax.ShapeDtypeStruct((m, FFT_N2, c), BF), compiler_params=_cparams(2), name=name)(*ops)


def _stage_b_kernel(a_ref, fa_ref, fb_ref, gs_ref, ss_ref, o_ref, *, kb, n_blk):
    scale = lax.rsqrt(ss_ref[...] + 1e-6)

    def run(filt):
        yhs = []
        for j in range(kb):
            a = jnp.concatenate([a_ref[0, j], a_ref[1, j]], axis=0)
            yhs.append(_dot(gs_ref[j], jnp.concatenate([a, filt(j)], axis=1)))
        zs = []
        for yh in yhs:
            y, h = yh[:, :WD], yh[:, WD:] * scale
            yr, yi = y[:FFT_N2], y[FFT_N2:]
            hr, hi = h[:FFT_N2], h[FFT_N2:]
            zs.append(jnp.concatenate([yr * hr - yi * hi, yr * hi + yi * hr], axis=0).astype(BF))
        for j in range(kb):
            b = lax.dot_general(gs_ref[j], zs[j], (((0,), (0,)), ((), ())), preferred_element_type=F32)
            o_ref[0, j] = b[:FFT_N2].astype(BF)
            o_ref[1, j] = b[FFT_N2:].astype(BF)

    def direct(j):
        return jnp.concatenate([fa_ref[0, j], fa_ref[1, j]], axis=0)

    def mirrored(j):
        ref, r = (fb_ref, 0) if j == 0 else (fa_ref, kb - j)
        return jnp.concatenate([ref[0, r], -ref[1, r]], axis=0)

    i = pl.program_id(0)

    @pl.when(i < n_blk // 2)
    def _():
        run(direct)

    @pl.when(i >= n_blk // 2)
    def _():
        run(mirrored)


def _stage_b(a, filt_a, gs, ss, *, order):
    n1 = a.shape[1]
    kb = min(DFT_B_KB, n1)
    n_blk = n1 // kb
    assert n_blk % 2 == 0 and filt_a.shape[1] == n1 // 2 + kb
    dblk = pl.BlockSpec((2, kb, FFT_N2, WD), lambda i: (0, i, 0, 0))
    fa = pl.BlockSpec((2, kb, FFT_N2, WD), lambda i: (0, jnp.where(i < n_blk // 2, i, n_blk - 1 - i), 0, order))
    fb = pl.BlockSpec((2, 1, FFT_N2, WD), lambda i: (0, jnp.where(i < n_blk // 2, 0, n1 - kb * i), 0, order))
    gblk = pl.BlockSpec((kb, 2 * FFT_N2, 2 * FFT_N2), lambda i: (i, 0, 0))
    return pl.pallas_call(
        functools.partial(_stage_b_kernel, kb=kb, n_blk=n_blk), grid=(n_blk,),
        in_specs=[dblk, fa, fb, gblk, pl.BlockSpec((1, WD), lambda i: (0, order))], out_specs=dblk,
        out_shape=jax.ShapeDtypeStruct((2, n1, FFT_N2, WD), BF),
        compiler_params=_cparams(1), name="dft_b")(a, filt_a, filt_a, gs, ss)


def _cmul_kernel(x_ref, h_ref, ss_ref, o_ref, *, half):
    scale = lax.rsqrt(ss_ref[...] + 1e-6)
    xr, xi = x_ref[:half, :], x_ref[half:, :]
    hr, hi = h_ref[:half, :] * scale, h_ref[half:, :] * scale
    o_ref[:half, :] = (xr * hr - xi * hi).astype(BF)
    o_ref[half:, :] = (xr * hi + xi * hr).astype(BF)


def _cmul(x, h, ss, lane_blk):
    rows = x.shape[0]
    return pl.pallas_call(
        functools.partial(_cmul_kernel, half=rows // 2), grid=(1,),
        in_specs=[_resident(x.shape), pl.BlockSpec((rows, WD), lambda i: (0, lane_blk)),
                  pl.BlockSpec((1, WD), lambda i: (0, lane_blk))],
        out_specs=pl.BlockSpec(x.shape, lambda i: (0, 0)),
        out_shape=jax.ShapeDtypeStruct(x.shape, BF), compiler_params=_cparams(1), name="ctx_cmul")(x, h, ss)


def _hyena_long(hv, hx, kern, ss, bias, tables, *, bsz, seq_len):
    assert bsz == 2
    n1 = 2 * seq_len // FFT_N2
    w_fwd, w_real, w_inv, gs = tables
    view = (bsz * seq_len // FFT_N2, FFT_N2, WD)
    filt_a = _stage_a(w_real, kern.reshape(n1, FFT_N2, HY_ORDER * WD), name="dft_a_filter")
    filt_a = filt_a.reshape(2, filt_a.shape[0] // 2, FFT_N2, HY_ORDER * WD)
    u = hv.reshape(view)
    for o in range(HY_ORDER):
        a = _stage_a(w_fwd, u, name="dft_a_fwd").reshape(2, n1, FFT_N2, WD)
        b = _stage_b(a, filt_a, gs, ss, order=o).reshape(2 * n1, FFT_N2, WD)
        u = _stage_a(w_inv, b, name="dft_a_inv", epi=(hx[o].reshape(view), u, bias[o][None]))
    return u.reshape(bsz * seq_len, WD)


def _hyena_short_seq(hv, hx, kern, ss, bias, tables, *, bsz):
    assert bsz == 2
    w_fwd, w_real, w_inv = tables
    spec = _mm(w_real, kern, bn=HY_ORDER * WD, out_dtype=F32, name="ctx_dft_filter")
    u = hv
    for o in range(HY_ORDER):
        x = _mm(w_fwd, u, bn=WD, out_dtype=F32, name="ctx_dft_fwd")
        z = _cmul(x, spec, ss, o)
        u = _mm(w_inv, z, bn=WD, out_dtype=BF, name="ctx_dft_inv", epi=(hx[o], u, bias[o][None]))
    return u


def _merge_kernel(x_ref, mod_ref, ya_ref, yb_ref, yc_ref, yd_ref, gate_ref, wp_ref, wo_ref, bo_ref,
                  g_ref, b_ref, o_ref, *, alpha):
    x = x_ref[...]
    m = jnp.zeros(x.shape, F32)
    for br, y_ref in enumerate((ya_ref, yb_ref, yc_ref, yd_ref)):
        gate = gate_ref[:, br * D_MODEL:(br + 1) * D_MODEL].astype(F32)
        m = m + gate * _dot(y_ref[...], wp_ref[br])
    out = _dot(m.astype(BF), wo_ref[...]) + bo_ref[...]
    y = alpha * x + mod_ref[0][5:6] * out
    o_ref[...] = _ln(y) * g_ref[...] + b_ref[...]


def _merge(x, mod, ya, yb, yc, yd, p, wp, wo, bo, g, b, *, tm, group_tiles, alpha):
    n = x.shape[0]
    yspec = pl.BlockSpec((tm, WD), lambda i: (i, 0))
    return pl.pallas_call(
        functools.partial(_merge_kernel, alpha=alpha), grid=(n // tm,),
        in_specs=[pl.BlockSpec((tm, D_MODEL), lambda i: (i, 0)),
                  pl.BlockSpec((1, N_MOD, D_MODEL), lambda i: (i // group_tiles, 0, 0)),
                  yspec, yspec, yspec, yspec,
                  pl.BlockSpec((tm, N_BRANCH * D_MODEL), lambda i: (i, 0)),
                  _resident(wp.shape), _resident(wo.shape), _resident((1, D_MODEL)),
                  _resident((1, D_MODEL)), _resident((1, D_MODEL))],
        out_specs=pl.BlockSpec((tm, D_MODEL), lambda i: (i, 0)),
        out_shape=jax.ShapeDtypeStruct((n, D_MODEL), F32),
        compiler_params=_cparams(1), name="merge")(x, mod, ya, yb, yc, yd, p, wp, wo, bo, g, b)


def kernel(x, c, ctx, c_ctx, w_mod, b_mod, post_ln_g, post_ln_b, ffn_w_in, ffn_w_out, w_in, b_in,
           conf_dw_w, conf_dw_b, conf_ln_g, conf_ln_b, conf_w_proj, na_rpb, na_w_proj, sc_conv_w,
           sc_w_proj, hy_sconv_w, hy_sconv_b, hy_w1, hy_b1, hy_w2, hy_b2, hy_freq, hy_w3, hy_bias,
           hy_w_proj, w_out, b_out):
    bsz, n_lat, _ = x.shape
    ctx_len = ctx.shape[1]
    depth = w_mod.shape[0]
    alpha = (2 * depth) ** 0.25
    rows = n_lat // GRID_W

    tm = 512
    tmc = min(tm, ctx_len)
    tf = 1024 if n_lat % 1024 == 0 else tm
    tfc = min(tf, bsz * ctx_len)
    tp = 512
    tpc = min(tp, ctx_len)

    xl = x.reshape(bsz * n_lat, D_MODEL)
    xc = ctx.reshape(bsz * ctx_len, D_MODEL)
    lat_tiles = n_lat // tm
    ctx_tiles = bsz * ctx_len // tmc

    tables = _dft_tables(2 * n_lat // FFT_N2)
    tables_c = _dft_tables_direct(ctx_len)

    cond = jnp.concatenate([c, c_ctx[None], jnp.zeros((SUBLANES - bsz - 1, D_MODEL), F32)], axis=0)

    ffn_wi = ffn_w_in.astype(BF)
    ffn_wo = ffn_w_out.astype(BF)

    for l in range(depth):
        last = l == depth - 1
        mod_all = _mm(cond, w_mod, bn=1024, out_dtype=F32, name="adaln_mod", bias=b_mod[l][None], a_silu=True,
                      b_layer=l, precise=True)
        mod_all = mod_all.reshape(SUBLANES, N_MOD, D_MODEL)
        mod = mod_all[:bsz]
        mod_c = mod_all[bsz:bsz + 1]

        wl = w_in[l]
        w_perm = jnp.concatenate([wl[:, OFF_G:], wl[:, OFF_A:OFF_Q], wl[:, OFF_SB:OFF_G],
                                  wl[:, OFF_Q:OFF_SB]], axis=1).astype(BF)
        bl = b_in[l]
        b_perm = jnp.concatenate([bl[OFF_G:], bl[OFF_A:OFF_Q], bl[OFF_SB:OFF_G], bl[OFF_Q:OFF_SB]])[None]
        ln_g = post_ln_g[l][:, None, :]
        ln_b = post_ln_b[l][:, None, :]
        wp = jnp.stack([conf_w_proj[l], na_w_proj[l], sc_w_proj[l], hy_w_proj[l]]).astype(BF)
        wo = w_out[l].astype(BF)
        conv_args = (conf_dw_w[l], conf_dw_b[l][None], conf_ln_g[l][None], conf_ln_b[l][None],
                     sc_conv_w[l], hy_sconv_w[l], hy_sconv_b[l][None])
        filt_args = (hy_w1[l], hy_b1[l], hy_w2[l], hy_b2[l], hy_freq[l], hy_w3[l])

        xl = _ffn(xl, mod, ffn_wi, ffn_wo, ln_g[0], ln_b[0], layer=l, half=0, m0=0, tm=tf,
                  group_tiles=n_lat // tf, alpha=alpha)
        xc = _ffn(xc, mod_c, ffn_wi, ffn_wo, ln_g[0], ln_b[0], layer=l, half=0, m0=0, tm=tfc,
                  group_tiles=bsz * ctx_len // tfc, alpha=alpha)

        p = _inproj(xl, mod, w_perm, b_perm, tm=tm, group_tiles=lat_tiles)
        pc = _inproj(xc, mod_c, w_perm, b_perm, tm=tmc, group_tiles=ctx_tiles)

        ya, yc, hv, h1, h2 = _prep(p, *conv_args, tm=tp, seq_len=n_lat)
        attn = _na(p, pc, _na_bias_table(na_rpb[l], rows), bsz=bsz, seq_len=n_lat)
        kern, ss = _hyena_filters(n_lat, *filt_args)
        yd = _hyena_long(hv, (h1, h2), kern, ss, hy_bias[l], tables, bsz=bsz, seq_len=n_lat)
        xl = _merge(xl, mod, ya, attn, yc, yd, p, wp, wo, b_out[l][None], ln_g[1], ln_b[1],
                    tm=2 * tm, group_tiles=lat_tiles // 2, alpha=alpha)

        if not last:
            ya, yc, hv, h1, h2 = _prep(pc, *conv_args, tm=tpc, seq_len=ctx_len)
            attn_c = _ctx_attn(pc, bsz=bsz, ctx_len=ctx_len)
            kern_c, ss_c = _hyena_filters(ctx_len, *filt_args)
            yd = _hyena_short_seq(hv, (h1, h2), kern_c, ss_c, hy_bias[l], tables_c, bsz=bsz)
            xc = _merge(xc, mod_c, ya, attn_c, yc, yd, pc, wp, wo, b_out[l][None], ln_g[1], ln_b[1],
                        tm=tmc, group_tiles=ctx_tiles, alpha=alpha)

        xl = _ffn(xl, mod, ffn_wi, ffn_wo, ln_g[2], ln_b[2], layer=l, half=1, m0=6, tm=tf,
                  group_tiles=n_lat // tf, alpha=alpha)
        if not last:
            xc = _ffn(xc, mod_c, ffn_wi, ffn_wo, ln_g[2], ln_b[2], layer=l, half=1, m0=6, tm=tfc,
                      group_tiles=bsz * ctx_len // tfc, alpha=alpha)

    return xl.reshape(bsz, n_lat, D_MODEL)
```
